```python
import math
import jax
import jax.numpy as jnp
from jax import lax
import numpy as np

D_MODEL = 4096
BATCH = 16
SEQ = 256
DEPTH = 4
DEC_BATCH = 2
DEC_SEQ = 1024
PAST_LEN = 512

GRID_W = 64
N_MIXERS = 4
HEAD_DIM = 128
NA_HEADS = D_MODEL // HEAD_DIM
WIN_R = 8
WIN_C = 16
GQA_HEADS = D_MODEL // HEAD_DIM
GQA_KV_HEADS = GQA_HEADS // 4
ROPE_THETA = 10000.0
Q_BLOCK = 128
D_RNN = D_MODEL
LRU_BLOCKS = 16
LRU_BW = D_RNN // LRU_BLOCKS
CONV_W = 4
LRU_C = 8.0
ML_HEADS = 8
ML_DK = D_MODEL // (2 * ML_HEADS)
ML_DV = D_MODEL // ML_HEADS
ML_CHUNK = 128
D_FF = 4 * D_MODEL
EPS = 1e-6
N_NA = (DEPTH + 3) // N_MIXERS
N_LRU = (DEPTH + 2) // N_MIXERS
N_ML = (DEPTH + 1) // N_MIXERS
N_GQA = DEPTH // N_MIXERS

kernel_name = 'hybrid_diffusion_prefix_trunk_step'


def rmsnorm(x, g):
    xf = x.astype(jnp.float32)
    y = xf * lax.rsqrt(jnp.mean(xf * xf, axis=-1, keepdims=True) + EPS)
    return (y * g.astype(jnp.float32)).astype(x.dtype)


def adaln(cond, w, b):
    mod = jax.nn.silu(cond) @ w + b
    if mod.ndim == 2:
        mod = mod[:, None, :]
    return jnp.split(mod, 6, axis=-1)


def modulate(h, shift, scale):
    return h * (1 + scale) + shift


def sq_relu_mlp(h, w_up, w_down):
    u = jax.nn.relu(h @ w_up)
    return (u * u) @ w_down


def block_attention(q, k, v):
    b, lq, h, hd = q.shape
    kvh = k.shape[2]
    g = h // kvh
    nb = lq // Q_BLOCK
    qb = q.reshape(b, nb, Q_BLOCK, kvh, g, hd).transpose(1, 0, 2, 3, 4, 5)
    scale = hd ** -0.5

    def one_block(qblk):
        s = jnp.einsum('bqkgd,bskd->bkgqs', qblk, k).astype(jnp.float32) * scale
        p = jax.nn.softmax(s, axis=-1).astype(v.dtype)
        return jnp.einsum('bkgqs,bskd->bqkgd', p, v)

    out = lax.map(one_block, qb)
    return out.transpose(1, 0, 2, 3, 4, 5).reshape(b, lq, h * hd)


def axial_rope(x):
    l = x.shape[1]
    t = jnp.arange(l)
    half = HEAD_DIM // 2
    inv_freq = 1.0 / (ROPE_THETA ** (jnp.arange(0, half, 2, dtype=jnp.float32) / half))

    def rot(xh, pos):
        ang = pos.astype(jnp.float32)[:, None] * inv_freq
        cos = jnp.cos(ang)[None, :, None, :]
        sin = jnp.sin(ang)[None, :, None, :]
        x1, x2 = jnp.split(xh.astype(jnp.float32), 2, axis=-1)
        return jnp.concatenate([x1 * cos - x2 * sin, x2 * cos + x1 * sin], axis=-1)

    out = jnp.concatenate([rot(x[..., :half], t // GRID_W), rot(x[..., half:], t % GRID_W)], axis=-1)
    return out.astype(x.dtype)


def na_project(h, w_qkv):
    b, l, _ = h.shape
    qkv = (h @ w_qkv).reshape(b, l, 3, NA_HEADS, HEAD_DIM)
    return qkv[:, :, 0], qkv[:, :, 1], qkv[:, :, 2]


def na_context(h, w_qkv, w_o):
    q, k, v = na_project(h, w_qkv)
    return block_attention(q, k, v) @ w_o, k, v


def na_latent(h, k_ctx, v_ctx, w_qkv, rpb, w_o):
    b, l, _ = h.shape
    rows = l // GRID_W
    wr = min(WIN_R, rows)
    q, k, v = na_project(h, w_qkv)
    r = jnp.arange(rows)
    rs = jnp.clip(r - wr // 2, 0, rows - wr)
    key_rows = rs[:, None] + jnp.arange(wr)[None, :]
    col = jnp.arange(GRID_W)
    cs = jnp.clip(col - WIN_C // 2, 0, GRID_W - WIN_C)
    col_ok = (col[None, :] >= cs[:, None]) & (col[None, :] < cs[:, None] + WIN_C)
    dc = jnp.clip(col[None, :] - col[:, None] + WIN_C - 1, 0, 2 * WIN_C - 2)
    dr = key_rows - r[:, None] + WIN_R - 1
    bias = rpb[:, dr][..., dc]
    bias = bias.transpose(0, 1, 3, 2, 4).reshape(NA_HEADS, rows, GRID_W, wr * GRID_W).astype(jnp.float32)
    mask = jnp.broadcast_to(col_ok[:, None, :], (GRID_W, wr, GRID_W)).reshape(GRID_W, wr * GRID_W)
    qg = q.reshape(b, rows, GRID_W, NA_HEADS, HEAD_DIM)
    kg = k.reshape(b, rows, GRID_W, NA_HEADS, HEAD_DIM)[:, key_rows].reshape(b, rows, wr * GRID_W, NA_HEADS, HEAD_DIM)
    vg = v.reshape(b, rows, GRID_W, NA_HEADS, HEAD_DIM)[:, key_rows].reshape(b, rows, wr * GRID_W, NA_HEADS, HEAD_DIM)
    scale = HEAD_DIM ** -0.5
    s_lat = jnp.einsum('brqhd,brkhd->bhrqk', qg, kg).astype(jnp.float32) * scale + bias
    s_lat = jnp.where(mask, s_lat, -jnp.inf)
    s_ctx = jnp.einsum('brqhd,bshd->bhrqs', qg, k_ctx).astype(jnp.float32) * scale
    p = jax.nn.softmax(jnp.concatenate([s_lat, s_ctx], axis=-1), axis=-1).astype(v.dtype)
    nk = wr * GRID_W
    out = (jnp.einsum('bhrqk,brkhd->brqhd', p[..., :nk], vg)
           + jnp.einsum('bhrqs,bshd->brqhd', p[..., nk:], v_ctx))
    return out.reshape(b, l, NA_HEADS * HEAD_DIM) @ w_o


def centred_dwconv(x, w, bias):
    l = x.shape[1]
    left = CONV_W // 2
    xp = jnp.pad(x, ((0, 0), (left, CONV_W - 1 - left), (0, 0)))
    return sum(xp[:, j:j + l] * w[j] for j in range(CONV_W)) + bias


def linear_scan(a, u, h0):
    u = u.at[:, 0].add(a[:, 0] * h0)

    def comb(e1, e2):
        a1, u1 = e1
        a2, u2 = e2
        return a1 * a2, a2 * u1 + u2

    _, h = lax.associative_scan(comb, (a, u), axis=1)
    return h


def rglru_block(h, h0, w_in, conv_w, conv_b, w_gates, b_gates, lam, w_o):
    b, l, _ = h.shape
    xr, gate_br = jnp.split(h @ w_in, 2, axis=-1)
    xf = centred_dwconv(xr, conv_w, conv_b).astype(jnp.float32)
    xblk = xf.reshape(b, l, LRU_BLOCKS, LRU_BW)
    g = jnp.einsum('blnc,dgncj->bldgnj', xblk, w_gates.astype(jnp.float32)).reshape(b, l, 2, 2, D_RNN)
    g = g + b_gates.astype(jnp.float32)
    r_gate = jax.nn.sigmoid(g[:, :, :, 0])
    i_gate = jax.nn.sigmoid(g[:, :, :, 1])
    log_a = -LRU_C * r_gate * jax.nn.softplus(-lam.astype(jnp.float32))
    a = jnp.exp(log_a)
    u = jnp.sqrt(-jnp.expm1(2.0 * log_a)) * i_gate * xf[:, :, None, :]
    h0f = h0.astype(jnp.float32)
    hf = linear_scan(a[:, :, 0], u[:, :, 0], h0f[:, 0])
    hb = jnp.flip(linear_scan(jnp.flip(a[:, :, 1], 1), jnp.flip(u[:, :, 1], 1), h0f[:, 1]), 1)
    y = ((hf + hb) * jax.nn.gelu(gate_br.astype(jnp.float32))).astype(h.dtype) @ w_o
    final = jnp.stack([hf[:, -1], hb[:, 0]], axis=1)
    return y, final


def mlstm_chunkwise(q, k, v, li, lf, c0, n0, m0):
    b, l, nh, dk = q.shape
    dv = v.shape[-1]
    nc = l // ML_CHUNK

    def chunks(x):
        y = x.reshape((b, nc, ML_CHUNK) + x.shape[2:])
        return jnp.swapaxes(jnp.swapaxes(y, 0, 1), 2, 3)

    tril = jnp.tril(jnp.ones((ML_CHUNK, ML_CHUNK), dtype=bool))

    def body(carry, xs):
        c_prev, n_prev, m_prev = carry
        qb, kb, vb, lib, lfb = xs
        bcum = jnp.cumsum(lfb, axis=-1)
        dmat = bcum[..., :, None] - bcum[..., None, :] + lib[..., None, :]
        dmat = jnp.where(tril, dmat, -jnp.inf)
        m_inter = bcum + m_prev[..., None]
        m_t = jnp.maximum(m_inter, jnp.max(dmat, axis=-1))
        s = jnp.einsum('bhtd,bhsd->bhts', qb, kb) * jnp.exp(dmat - m_t[..., None])
        inter = jnp.exp(m_inter - m_t)
        num = jnp.einsum('bhts,bhse->bhte', s, vb) + inter[..., None] * jnp.einsum('bhtd,bhde->bhte', qb, c_prev)
        den = jnp.sum(s, axis=-1) + inter * jnp.einsum('bhtd,bhd->bht', qb, n_prev)
        h_out = num / jnp.maximum(jnp.abs(den), jnp.exp(-m_t))[..., None]
        b_last = bcum[..., -1]
        dec = b_last[..., None] - bcum + lib
        m_new = jnp.maximum(b_last + m_prev, jnp.max(dec, axis=-1))
        w_s = jnp.exp(dec - m_new[..., None])
        carry_scale = jnp.exp(b_last + m_prev - m_new)
        c_new = carry_scale[..., None, None] * c_prev + jnp.einsum('bhs,bhsd,bhse->bhde', w_s, kb, vb)
        n_new = carry_scale[..., None] * n_prev + jnp.einsum('bhs,bhsd->bhd', w_s, kb)
        return (c_new, n_new, m_new), h_out

    (c_f, n_f, m_f), hs = lax.scan(body, (c0, n0, m0), (chunks(q), chunks(k), chunks(v), chunks(li), chunks(lf)))
    h = jnp.swapaxes(jnp.swapaxes(hs, 2, 3), 0, 1).reshape(b, l, nh, dv)
    return h, (c_f, n_f, m_f)


def mlstm_block(h, c0, n0, m0, w_in, w_gates, b_gates, norm_g, w_o):
    b, l, _ = h.shape
    sq = ML_HEADS * ML_DK
    sv = ML_HEADS * ML_DV
    proj = h @ w_in
    q = proj[..., :sq].reshape(b, l, ML_HEADS, ML_DK).astype(jnp.float32)
    k = proj[..., sq:2 * sq].reshape(b, l, ML_HEADS, ML_DK).astype(jnp.float32) * (ML_DK ** -0.5)
    v = proj[..., 2 * sq:2 * sq + sv].reshape(b, l, ML_HEADS, ML_DV).astype(jnp.float32)
    o = proj[..., 2 * sq + sv:].astype(jnp.float32)
    gates = (h @ w_gates).reshape(b, l, 2, 2, ML_HEADS).astype(jnp.float32) + b_gates.astype(jnp.float32)
    li = gates[:, :, :, 0]
    lf = jax.nn.log_sigmoid(gates[:, :, :, 1])
    c0f, n0f, m0f = c0.astype(jnp.float32), n0.astype(jnp.float32), m0.astype(jnp.float32)
    hf, (cf, nf, mf) = mlstm_chunkwise(q, k, v, li[:, :, 0], lf[:, :, 0], c0f[:, 0], n0f[:, 0], m0f[:, 0])
    hb, (cb, nb, mb) = mlstm_chunkwise(jnp.flip(q, 1), jnp.flip(k, 1), jnp.flip(v, 1),
                                       jnp.flip(li[:, :, 1], 1), jnp.flip(lf[:, :, 1], 1),
                                       c0f[:, 1], n0f[:, 1], m0f[:, 1])
    hsum = hf + jnp.flip(hb, 1)
    hn = hsum * lax.rsqrt(jnp.mean(hsum * hsum, axis=-1, keepdims=True) + EPS)
    hn = hn.reshape(b, l, sv) * norm_g.astype(jnp.float32)
    y = (jax.nn.sigmoid(o) * hn).astype(h.dtype) @ w_o
    return y, jnp.stack([cf, cb], axis=1), jnp.stack([nf, nb], axis=1), jnp.stack([mf, mb], axis=1)


def gqa_project(h, w_qkv, q_g, k_g):
    b, l, _ = h.shape
    nq = GQA_HEADS * HEAD_DIM
    nk = GQA_KV_HEADS * HEAD_DIM
    qkv = h @ w_qkv
    q = rmsnorm(qkv[..., :nq].reshape(b, l, GQA_HEADS, HEAD_DIM), q_g)
    k = rmsnorm(qkv[..., nq:nq + nk].reshape(b, l, GQA_KV_HEADS, HEAD_DIM), k_g)
    v = qkv[..., nq + nk:].reshape(b, l, GQA_KV_HEADS, HEAD_DIM)
    return q, k, v


def gqa_context(h, w_qkv, q_g, k_g, w_o):
    q, k, v = gqa_project(h, w_qkv, q_g, k_g)
    return block_attention(q, k, v) @ w_o, k, v


def gqa_latent(h, k_ctx, v_ctx, w_qkv, q_g, k_g, w_o):
    q, k, v = gqa_project(h, w_qkv, q_g, k_g)
    q = axial_rope(q)
    k = axial_rope(k)
    kk = jnp.concatenate([k_ctx.astype(k.dtype), k], axis=1)
    vv = jnp.concatenate([v_ctx.astype(v.dtype), v], axis=1)
    return block_attention(q, kk, vv) @ w_o


def setup_inputs(seed: int = 0) -> dict:
    key = jax.random.key(seed)
    ks = iter(jax.random.split(key, 48))
    f32 = jnp.float32

    def nrm(shape, scale):
        return jax.random.normal(next(ks), shape, f32) * scale

    def gain(shape):
        return 1.0 + nrm(shape, 0.05)

    a_c = jax.random.uniform(next(ks), (N_LRU, 2, D_RNN), f32, minval=0.9, maxval=0.999)
    a_base = a_c ** (1.0 / LRU_C)
    lru_lambda = jnp.log(a_base) - jnp.log1p(-a_base)
    forget_offset = jnp.stack([jnp.zeros((ML_HEADS,), f32), jnp.linspace(3.0, 6.0, ML_HEADS, dtype=f32)])
    ml_b_gates = nrm((N_ML, 2, 2, ML_HEADS), 0.1) + forget_offset[None, None]
    return {
        'x_prompt': nrm((BATCH, SEQ, D_MODEL), 1.0),
        'x_sample': nrm((DEC_BATCH, DEC_SEQ, D_MODEL), 1.0),
        'cache_na_k': nrm((DEC_BATCH, N_NA, PAST_LEN, NA_HEADS, HEAD_DIM), 1.0),
        'cache_na_v': nrm((DEC_BATCH, N_NA, PAST_LEN, NA_HEADS, HEAD_DIM), 1.0),
        'state_lru': nrm((DEC_BATCH, N_LRU, 2, D_RNN), 0.5),
        'state_mlstm_C': nrm((DEC_BATCH, N_ML, 2, ML_HEADS, ML_DK, ML_DV), 0.1),
        'state_mlstm_n': nrm((DEC_BATCH, N_ML, 2, ML_HEADS, ML_DK), 0.1),
        'state_mlstm_m': nrm((DEC_BATCH, N_ML, 2, ML_HEADS), 0.5),
        'cache_gqa_k': nrm((DEC_BATCH, N_GQA, PAST_LEN, GQA_KV_HEADS, HEAD_DIM), 1.0),
        'cache_gqa_v': nrm((DEC_BATCH, N_GQA, PAST_LEN, GQA_KV_HEADS, HEAD_DIM), 1.0),
        'c': nrm((DEC_BATCH, D_MODEL), 1.0),
        'c_ctx': nrm((D_MODEL,), 1.0),
        'ada_w': nrm((DEPTH, D_MODEL, 6 * D_MODEL), 0.5 * D_MODEL ** -0.5),
        'ada_b': nrm((DEPTH, 6 * D_MODEL), 0.02),
        'norm_g': gain((DEPTH, 4, D_MODEL)),
        'mlp_up': nrm((DEPTH, D_MODEL, D_FF), D_MODEL ** -0.5),
        'mlp_down': nrm((DEPTH, D_FF, D_MODEL), D_FF ** -0.5),
        'na_w_qkv': nrm((N_NA, D_MODEL, 3 * NA_HEADS * HEAD_DIM), D_MODEL ** -0.5),
        'na_rpb': nrm((N_NA, NA_HEADS, 2 * WIN_R - 1, 2 * WIN_C - 1), 0.1),
        'na_w_o': nrm((N_NA, NA_HEADS * HEAD_DIM, D_MODEL), (NA_HEADS * HEAD_DIM) ** -0.5),
        'lru_w_in': nrm((N_LRU, D_MODEL, 2 * D_RNN), D_MODEL ** -0.5),
        'lru_conv_w': nrm((N_LRU, CONV_W, D_RNN), CONV_W ** -0.5),
        'lru_conv_b': nrm((N_LRU, D_RNN), 0.02),
        'lru_w_gates': nrm((N_LRU, 2, 2, LRU_BLOCKS, LRU_BW, LRU_BW), LRU_BW ** -0.5),
        'lru_b_gates': nrm((N_LRU, 2, 2, D_RNN), 0.1),
        'lru_lambda': lru_lambda,
        'lru_w_o': nrm((N_LRU, D_RNN, D_MODEL), D_RNN ** -0.5),
        'ml_w_in': nrm((N_ML, D_MODEL, 2 * ML_HEADS * ML_DK + 2 * ML_HEADS * ML_DV), D_MODEL ** -0.5),
        'ml_w_gates': nrm((N_ML, D_MODEL, 4 * ML_HEADS), D_MODEL ** -0.5),
        'ml_b_gates': ml_b_gates,
        'ml_norm_g': gain((N_ML, ML_HEADS * ML_DV)),
        'ml_w_o': nrm((N_ML, ML_HEADS * ML_DV, D_MODEL), (ML_HEADS * ML_DV) ** -0.5),
        'gqa_w_qkv': nrm((N_GQA, D_MODEL, (GQA_HEADS + 2 * GQA_KV_HEADS) * HEAD_DIM), D_MODEL ** -0.5),
        'gqa_q_norm': gain((N_GQA, HEAD_DIM)),
        'gqa_k_norm': gain((N_GQA, HEAD_DIM)),
        'gqa_w_o': nrm((N_GQA, GQA_HEADS * HEAD_DIM, D_MODEL), (GQA_HEADS * HEAD_DIM) ** -0.5),
    }


def reference(x_prompt, x_sample, cache_na_k, cache_na_v, state_lru, state_mlstm_C, state_mlstm_n,
              state_mlstm_m, cache_gqa_k, cache_gqa_v, c, c_ctx, ada_w, ada_b, norm_g, mlp_up, mlp_down,
              na_w_qkv, na_rpb, na_w_o, lru_w_in, lru_conv_w, lru_conv_b, lru_w_gates, lru_b_gates,
              lru_lambda, lru_w_o, ml_w_in, ml_w_gates, ml_b_gates, ml_norm_g, ml_w_o,
              gqa_w_qkv, gqa_q_norm, gqa_k_norm, gqa_w_o):
    xp, xs = x_prompt, x_sample
    bp = xp.shape[0]
    new_na_k, new_na_v, new_lru, new_mc, new_mn, new_mm, new_gk, new_gv = [], [], [], [], [], [], [], []
    for i in range(DEPTH):
        kind, j = i % N_MIXERS, i // N_MIXERS
        sh1p, sc1p, gt1p, sh2p, sc2p, gt2p = adaln(c_ctx, ada_w[i], ada_b[i])
        sh1s, sc1s, gt1s, sh2s, sc2s, gt2s = adaln(c, ada_w[i], ada_b[i])
        hp = modulate(rmsnorm(xp, norm_g[i, 0]), sh1p, sc1p)
        hs = modulate(rmsnorm(xs, norm_g[i, 0]), sh1s, sc1s)
        if kind == 0:
            yp, kp, vp = na_context(hp, na_w_qkv[j], na_w_o[j])
            ys = na_latent(hs, cache_na_k[:, j], cache_na_v[:, j], na_w_qkv[j], na_rpb[j], na_w_o[j])
            new_na_k.append(kp)
            new_na_v.append(vp)
        elif kind == 1:
            zero_state = jnp.zeros((bp, 2, D_RNN), jnp.float32)
            yp, sp = rglru_block(hp, zero_state, lru_w_in[j], lru_conv_w[j], lru_conv_b[j], lru_w_gates[j],
                                 lru_b_gates[j], lru_lambda[j], lru_w_o[j])
            ys, _ = rglru_block(hs, state_lru[:, j], lru_w_in[j], lru_conv_w[j], lru_conv_b[j], lru_w_gates[j],
                                lru_b_gates[j], lru_lambda[j], lru_w_o[j])
            new_lru.append(sp)
        elif kind == 2:
            zc = jnp.zeros((bp, 2, ML_HEADS, ML_DK, ML_DV), jnp.float32)
            zn = jnp.zeros((bp, 2, ML_HEADS, ML_DK), jnp.float32)
            zm = jnp.zeros((bp, 2, ML_HEADS), jnp.float32)
            yp, cp, np_, mp = mlstm_block(hp, zc, zn, zm, ml_w_in[j], ml_w_gates[j], ml_b_gates[j],
                                          ml_norm_g[j], ml_w_o[j])
            ys, _, _, _ = mlstm_block(hs, state_mlstm_C[:, j], state_mlstm_n[:, j], state_mlstm_m[:, j],
                                      ml_w_in[j], ml_w_gates[j], ml_b_gates[j], ml_norm_g[j], ml_w_o[j])
            new_mc.append(cp)
            new_mn.append(np_)
            new_mm.append(mp)
        else:
            yp, kp, vp = gqa_context(hp, gqa_w_qkv[j], gqa_q_norm[j], gqa_k_norm[j], gqa_w_o[j])
            ys = gqa_latent(hs, cache_gqa_k[:, j], cache_gqa_v[:, j], gqa_w_qkv[j], gqa_q_norm[j],
                            gqa_k_norm[j], gqa_w_o[j])
            new_gk.append(kp)
            new_gv.append(vp)
        xp = xp + gt1p * rmsnorm(yp, norm_g[i, 1])
        xs = xs + gt1s * rmsnorm(ys, norm_g[i, 1])
        mp_in = modulate(rmsnorm(xp, norm_g[i, 2]), sh2p, sc2p)
        ms_in = modulate(rmsnorm(xs, norm_g[i, 2]), sh2s, sc2s)
        xp = xp + gt2p * rmsnorm(sq_relu_mlp(mp_in, mlp_up[i], mlp_down[i]), norm_g[i, 3])
        xs = xs + gt2s * rmsnorm(sq_relu_mlp(ms_in, mlp_up[i], mlp_down[i]), norm_g[i, 3])
    return (xp, xs, jnp.stack(new_na_k, axis=1), jnp.stack(new_na_v, axis=1), jnp.stack(new_lru, axis=1),
            jnp.stack(new_mc, axis=1), jnp.stack(new_mn, axis=1), jnp.stack(new_mm, axis=1),
            jnp.stack(new_gk, axis=1), jnp.stack(new_gv, axis=1))
```

```python
import functools

import jax
import jax.numpy as jnp
from jax import lax
from jax.experimental import pallas as pl
from jax.experimental.pallas import tpu as pltpu

D_MODEL = 4096
BATCH = 16
SEQ = 256
DEPTH = 4
DEC_BATCH = 2
DEC_SEQ = 1024
PAST_LEN = 512
GRID_W = 64
HEAD_DIM = 128
NA_HEADS = 32
WIN_R = 8
WIN_C = 16
GQA_HEADS = 32
GQA_KV_HEADS = 8
ROPE_THETA = 10000.0
Q_BLOCK = 128
D_RNN = D_MODEL
LRU_BLOCKS = 16
LRU_BW = 256
CONV_W = 4
LRU_C = 8.0
ML_HEADS = 8
ML_DK = 256
ML_DV = 512
ML_CHUNK = 128
D_FF = 4 * D_MODEL
EPS = 1e-6

N_CTX = BATCH * SEQ
N_LAT = DEC_BATCH * DEC_SEQ
N_TOK = N_CTX + N_LAT
MOD_ROWS = 8

VMEM_LIMIT = 56 * 1024 * 1024

BF16 = jnp.bfloat16
F32 = jnp.float32


def _cparams(sem):
    return pltpu.CompilerParams(dimension_semantics=sem, vmem_limit_bytes=VMEM_LIMIT)


def _mm_kernel(a_ref, w_ref, o_ref, *, act):
    acc = jnp.dot(a_ref[...], w_ref[...].astype(BF16), preferred_element_type=F32)
    if act == "relu2":
        r = jnp.maximum(acc, 0.0)
        acc = r * r
    o_ref[...] = acc.astype(o_ref.dtype)


def _mm_kernel_kgrid(a_ref, w_ref, o_ref, acc_ref, *, nk):
    k = pl.program_id(2)
    part = jnp.dot(a_ref[...], w_ref[...].astype(BF16), preferred_element_type=F32)

    @pl.when(k == 0)
    def _():
        acc_ref[...] = part

    @pl.when(k > 0)
    def _():
        acc_ref[...] += part

    @pl.when(k == nk - 1)
    def _():
        o_ref[...] = acc_ref[...].astype(o_ref.dtype)


def matmul(a, w, *, out_dtype=F32, act=None, tm=1024, tn=512, tk=4096):
    m, k = a.shape
    k2, n = w.shape
    assert k == k2 and a.dtype == BF16
    tm, tn, tk = min(tm, m), min(tn, n), min(tk, k)
    assert m % tm == 0 and n % tn == 0 and k % tk == 0
    nk = k // tk
    if nk == 1:
        return pl.pallas_call(
            functools.partial(_mm_kernel, act=act),
            grid=(m // tm, n // tn),
            in_specs=[pl.BlockSpec((tm, k), lambda i, j: (i, 0)),
                      pl.BlockSpec((k, tn), lambda i, j: (0, j))],
            out_specs=pl.BlockSpec((tm, tn), lambda i, j: (i, j)),
            out_shape=jax.ShapeDtypeStruct((m, n), out_dtype),
            compiler_params=_cparams(("parallel", "arbitrary")),
        )(a, w)
    assert act is None
    return pl.pallas_call(
        functools.partial(_mm_kernel_kgrid, nk=nk),
        grid=(m // tm, n // tn, nk),
        in_specs=[pl.BlockSpec((tm, tk), lambda i, j, kk: (i, kk)),
                  pl.BlockSpec((tk, tn), lambda i, j, kk: (kk, j))],
        out_specs=pl.BlockSpec((tm, tn), lambda i, j, kk: (i, j)),
        out_shape=jax.ShapeDtypeStruct((m, n), out_dtype),
        scratch_shapes=[pltpu.VMEM((tm, tn), F32)],
        compiler_params=_cparams(("parallel", "arbitrary", "arbitrary")),
    )(a, w)


def _ada_kernel(c_ref, w_ref, b_ref, o_ref):
    c = c_ref[...]
    a = (c * jax.nn.sigmoid(c)).astype(BF16)
    o_ref[...] = jnp.dot(a, w_ref[...].astype(BF16), preferred_element_type=F32) + b_ref[...]


def adaln_all(cond, ada_w, ada_b, *, tn=512):
    n = 6 * D_MODEL
    out = pl.pallas_call(
        _ada_kernel,
        grid=(DEPTH, n // tn),
        in_specs=[pl.BlockSpec((MOD_ROWS, D_MODEL), lambda l, j: (0, 0)),
                  pl.BlockSpec((None, D_MODEL, tn), lambda l, j: (l, 0, j)),
                  pl.BlockSpec((None, 1, tn), lambda l, j: (l, 0, j))],
        out_specs=pl.BlockSpec((None, MOD_ROWS, tn), lambda l, j: (l, 0, j)),
        out_shape=jax.ShapeDtypeStruct((DEPTH, MOD_ROWS, n), F32),
        compiler_params=_cparams(("parallel", "arbitrary")),
    )(cond, ada_w, ada_b.reshape(DEPTH, 1, n))
    return out.reshape(DEPTH, MOD_ROWS, 6, 1, D_MODEL)


ROW_TILE = 256


def _row_group(i):
    return jnp.maximum(0, (i * ROW_TILE - N_CTX) // DEC_SEQ + 1)


def _rms(x, g):
    return x * lax.rsqrt(jnp.mean(x * x, axis=-1, keepdims=True) + EPS) * g


def _mod_spec(layer, which):
    return pl.BlockSpec((None, None, None, 1, D_MODEL),
                        lambda i: (layer, _row_group(i), which, 0, 0))


def _gain_spec(layer, which):
    return pl.BlockSpec((None, None, 1, D_MODEL), lambda i: (layer, which, 0, 0))


_ROWS_SPEC = pl.BlockSpec((ROW_TILE, D_MODEL), lambda i: (i, 0))


def _norm_mod_kernel(x_ref, g_ref, sh_ref, sc_ref, h_ref):
    h = _rms(x_ref[...], g_ref[...]) * (1.0 + sc_ref[...]) + sh_ref[...]
    h_ref[...] = h.astype(h_ref.dtype)


def norm_mod(x, gains, mods, layer):
    return pl.pallas_call(
        _norm_mod_kernel,
        grid=(N_TOK // ROW_TILE,),
        in_specs=[_ROWS_SPEC, _gain_spec(layer, 0), _mod_spec(layer, 0), _mod_spec(layer, 1)],
        out_specs=_ROWS_SPEC,
        out_shape=jax.ShapeDtypeStruct((N_TOK, D_MODEL), BF16),
        compiler_params=_cparams(("parallel",)),
    )(x, gains, mods, mods)


def _resid_kernel(x_ref, y_ref, ga_ref, gate_ref, *rest, with_h):
    xn = x_ref[...] + gate_ref[...] * _rms(y_ref[...], ga_ref[...])
    if with_h:
        gb_ref, sh_ref, sc_ref, xo_ref, h_ref = rest
        h = _rms(xn, gb_ref[...]) * (1.0 + sc_ref[...]) + sh_ref[...]
        h_ref[...] = h.astype(h_ref.dtype)
    else:
        (xo_ref,) = rest
    xo_ref[...] = xn


def resid_norm_mod(x, y, gains, mods, *, layer, ga, gate, nxt):
    in_specs = [_ROWS_SPEC, _ROWS_SPEC, _gain_spec(layer, ga), _mod_spec(layer, gate)]
    args = [x, y, gains, mods]
    out_specs = [_ROWS_SPEC]
    out_shape = [jax.ShapeDtypeStruct((N_TOK, D_MODEL), F32)]
    if nxt is not None:
        nl, ng, nsh, nsc = nxt
        in_specs += [_gain_spec(nl, ng), _mod_spec(nl, nsh), _mod_spec(nl, nsc)]
        args += [gains, mods, mods]
        out_specs.append(_ROWS_SPEC)
        out_shape.append(jax.ShapeDtypeStruct((N_TOK, D_MODEL), BF16))
    res = pl.pallas_call(
        functools.partial(_resid_kernel, with_h=nxt is not None),
        grid=(N_TOK // ROW_TILE,),
        in_specs=in_specs,
        out_specs=out_specs,
        out_shape=out_shape,
        compiler_params=_cparams(("parallel",)),
    )(*args)
    return (res[0], res[1]) if nxt is not None else (res[0], None)


def _block_attention(q, k, v):
    b, lq, h, hd = q.shape
    kvh = k.shape[2]
    g = h // kvh
    qg = q.reshape(b, lq, kvh, g, hd)
    s = jnp.einsum('bqkgd,bskd->bkgqs', qg, k).astype(F32) * (hd ** -0.5)
    p = jax.nn.softmax(s, axis=-1)
    out = jnp.einsum('bkgqs,bskd->bqkgd', p, v)
    return out.reshape(b, lq, h * hd)


def _na_latent_core(q, k, v, k_ctx, v_ctx, rpb):
    b, l = q.shape[0], q.shape[1]
    rows = l // GRID_W
    wr = min(WIN_R, rows)
    r = jnp.arange(rows)
    rs = jnp.clip(r - wr // 2, 0, rows - wr)
    key_rows = rs[:, None] + jnp.arange(wr)[None, :]
    col = jnp.arange(GRID_W)
    cs = jnp.clip(col - WIN_C // 2, 0, GRID_W - WIN_C)
    col_ok = (col[None, :] >= cs[:, None]) & (col[None, :] < cs[:, None] + WIN_C)
    dc = jnp.clip(col[None, :] - col[:, None] + WIN_C - 1, 0, 2 * WIN_C - 2)
    dr = key_rows - r[:, None] + WIN_R - 1
    bias = rpb[:, dr][..., dc]
    bias = bias.transpose(0, 1, 3, 2, 4).reshape(NA_HEADS, rows, GRID_W, wr * GRID_W).astype(F32)
    mask = jnp.broadcast_to(col_ok[:, None, :], (GRID_W, wr, GRID_W)).reshape(GRID_W, wr * GRID_W)
    qg = q.reshape(b, rows, GRID_W, NA_HEADS, HEAD_DIM)
    kg = k.reshape(b, rows, GRID_W, NA_HEADS, HEAD_DIM)[:, key_rows].reshape(b, rows, wr * GRID_W, NA_HEADS, HEAD_DIM)
    vg = v.reshape(b, rows, GRID_W, NA_HEADS, HEAD_DIM)[:, key_rows].reshape(b, rows, wr * GRID_W, NA_HEADS, HEAD_DIM)
    scale = HEAD_DIM ** -0.5
    s_lat = jnp.einsum('brqhd,brkhd->bhrqk', qg, kg).astype(F32) * scale + bias
    s_lat = jnp.where(mask, s_lat, -jnp.inf)
    s_ctx = jnp.einsum('brqhd,bshd->bhrqs', qg, k_ctx).astype(F32) * scale
    p = jax.nn.softmax(jnp.concatenate([s_lat, s_ctx], axis=-1), axis=-1)
    nk = wr * GRID_W
    out = (jnp.einsum('bhrqk,brkhd->brqhd', p[..., :nk], vg)
           + jnp.einsum('bhrqs,bshd->brqhd', p[..., nk:], v_ctx))
    return out.reshape(b, l, NA_HEADS * HEAD_DIM)


def _centred_dwconv(x, w, bias):
    l = x.shape[1]
    left = CONV_W // 2
    xp = jnp.pad(x, ((0, 0), (left, CONV_W - 1 - left), (0, 0)))
    return sum(xp[:, j:j + l] * w[j] for j in range(CONV_W)) + bias


def _linear_scan(a, u, h0):
    u = u.at[:, 0].add(a[:, 0] * h0)

    def comb(e1, e2):
        a1, u1 = e1
        a2, u2 = e2
        return a1 * a2, a2 * u1 + u2

    _, h = lax.associative_scan(comb, (a, u), axis=1)
    return h


def _rglru_core(proj, h0, conv_w, conv_b, w_gates, b_gates, lam):
    b, l, _ = proj.shape
    xr, gate_br = jnp.split(proj, 2, axis=-1)
    xf = _centred_dwconv(xr, conv_w, conv_b)
    xblk = xf.reshape(b, l, LRU_BLOCKS, LRU_BW)
    g = jnp.einsum('blnc,dgncj->bldgnj', xblk, w_gates).reshape(b, l, 2, 2, D_RNN)
    g = g + b_gates
    r_gate = jax.nn.sigmoid(g[:, :, :, 0])
    i_gate = jax.nn.sigmoid(g[:, :, :, 1])
    log_a = -LRU_C * r_gate * jax.nn.softplus(-lam)
    a = jnp.exp(log_a)
    u = jnp.sqrt(-jnp.expm1(2.0 * log_a)) * i_gate * xf[:, :, None, :]
    hf = _linear_scan(a[:, :, 0], u[:, :, 0], h0[:, 0])
    hb = jnp.flip(_linear_scan(jnp.flip(a[:, :, 1], 1), jnp.flip(u[:, :, 1], 1), h0[:, 1]), 1)
    y = (hf + hb) * jax.nn.gelu(gate_br)
    final = jnp.stack([hf[:, -1], hb[:, 0]], axis=1)
    return y, final


def _mlstm_chunkwise(q, k, v, li, lf, c0, n0, m0):
    b, l, nh, dk = q.shape
    dv = v.shape[-1]
    nc = l // ML_CHUNK

    def chunks(x):
        y = x.reshape((b, nc, ML_CHUNK) + x.shape[2:])
        return jnp.swapaxes(jnp.swapaxes(y, 0, 1), 2, 3)

    tril = jnp.tril(jnp.ones((ML_CHUNK, ML_CHUNK), dtype=bool))

    def body(carry, xs):
        c_prev, n_prev, m_prev = carry
        qb, kb, vb, lib, lfb = xs
        bcum = jnp.cumsum(lfb, axis=-1)
        dmat = bcum[..., :, None] - bcum[..., None, :] + lib[..., None, :]
        dmat = jnp.where(tril, dmat, -jnp.inf)
        m_inter = bcum + m_prev[..., None]
        m_t = jnp.maximum(m_inter, jnp.max(dmat, axis=-1))
        s = jnp.einsum('bhtd,bhsd->bhts', qb, kb) * jnp.exp(dmat - m_t[..., None])
        inter = jnp.exp(m_inter - m_t)
        num = jnp.einsum('bhts,bhse->bhte', s, vb) + inter[..., None] * jnp.einsum('bhtd,bhde->bhte', qb, c_prev)
        den = jnp.sum(s, axis=-1) + inter * jnp.einsum('bhtd,bhd->bht', qb, n_prev)
        h_out = num / jnp.maximum(jnp.abs(den), jnp.exp(-m_t))[..., None]
        b_last = bcum[..., -1]
        dec = b_last[..., None] - bcum + lib
        m_new = jnp.maximum(b_last + m_prev, jnp.max(dec, axis=-1))
        w_s = jnp.exp(dec - m_new[..., None])
        carry_scale = jnp.exp(b_last + m_prev - m_new)
        c_new = carry_scale[..., None, None] * c_prev + jnp.einsum('bhs,bhsd,bhse->bhde', w_s, kb, vb)
        n_new = carry_scale[..., None] * n_prev + jnp.einsum('bhs,bhsd->bhd', w_s, kb)
        return (c_new, n_new, m_new), h_out

    (c_f, n_f, m_f), hs = lax.scan(body, (c0, n0, m0), (chunks(q), chunks(k), chunks(v), chunks(li), chunks(lf)))
    h = jnp.swapaxes(jnp.swapaxes(hs, 2, 3), 0, 1).reshape(b, l, nh, dv)
    return h, (c_f, n_f, m_f)


def _mlstm_core(proj, gates, c0, n0, m0, b_gates, norm_g):
    b, l, _ = proj.shape
    sq = ML_HEADS * ML_DK
    sv = ML_HEADS * ML_DV
    q = proj[..., :sq].reshape(b, l, ML_HEADS, ML_DK)
    k = proj[..., sq:2 * sq].reshape(b, l, ML_HEADS, ML_DK) * (ML_DK ** -0.5)
    v = proj[..., 2 * sq:2 * sq + sv].reshape(b, l, ML_HEADS, ML_DV)
    o = proj[..., 2 * sq + sv:]
    gates = gates.reshape(b, l, 2, 2, ML_HEADS) + b_gates
    li = gates[:, :, :, 0]
    lf = jax.nn.log_sigmoid(gates[:, :, :, 1])
    hf, (cf, nf, mf) = _mlstm_chunkwise(q, k, v, li[:, :, 0], lf[:, :, 0], c0[:, 0], n0[:, 0], m0[:, 0])
    hb, (cb, nb, mb) = _mlstm_chunkwise(jnp.flip(q, 1), jnp.flip(k, 1), jnp.flip(v, 1),
                                        jnp.flip(li[:, :, 1], 1), jnp.flip(lf[:, :, 1], 1),
                                        c0[:, 1], n0[:, 1], m0[:, 1])
    hsum = hf + jnp.flip(hb, 1)
    hn = hsum * lax.rsqrt(jnp.mean(hsum * hsum, axis=-1, keepdims=True) + EPS)
    hn = hn.reshape(b, l, sv) * norm_g
    y = jax.nn.sigmoid(o) * hn
    return y, jnp.stack([cf, cb], axis=1), jnp.stack([nf, nb], axis=1), jnp.stack([mf, mb], axis=1)


def _head_rms(x, g):
    return x * lax.rsqrt(jnp.mean(x * x, axis=-1, keepdims=True) + EPS) * g


def _axial_rope(x):
    l = x.shape[1]
    t = jnp.arange(l)
    half = HEAD_DIM // 2
    inv_freq = 1.0 / (ROPE_THETA ** (jnp.arange(0, half, 2, dtype=F32) / half))

    def rot(xh, pos):
        ang = pos.astype(F32)[:, None] * inv_freq
        cos = jnp.cos(ang)[None, :, None, :]
        sin = jnp.sin(ang)[None, :, None, :]
        x1, x2 = jnp.split(xh, 2, axis=-1)
        return jnp.concatenate([x1 * cos - x2 * sin, x2 * cos + x1 * sin], axis=-1)

    return jnp.concatenate([rot(x[..., :half], t // GRID_W), rot(x[..., half:], t % GRID_W)], axis=-1)


def _split(t):
    return t[:N_CTX], t[N_CTX:]


def kernel(x_prompt, x_sample, cache_na_k, cache_na_v, state_lru, state_mlstm_C, state_mlstm_n, state_mlstm_m, cache_gqa_k, cache_gqa_v, c, c_ctx, ada_w, ada_b, norm_g, mlp_up, mlp_down, na_w_qkv, na_rpb, na_w_o, lru_w_in, lru_conv_w, lru_conv_b, lru_w_gates, lru_b_gates, lru_lambda, lru_w_o, ml_w_in, ml_w_gates, ml_b_gates, ml_norm_g, ml_w_o, gqa_w_qkv, gqa_q_norm, gqa_k_norm, gqa_w_o):
    x = jnp.concatenate([x_prompt.reshape(N_CTX, D_MODEL), x_sample.reshape(N_LAT, D_MODEL)], axis=0)
    cond = jnp.concatenate([c_ctx[None], c, jnp.zeros((MOD_ROWS - 1 - DEC_BATCH, D_MODEL), F32)], axis=0)
    mods = adaln_all(cond, ada_w, ada_b)
    gains = norm_g.reshape(DEPTH, 4, 1, D_MODEL)

    outs = {}
    h = norm_mod(x, gains, mods, 0)
    for i in range(DEPTH):
        kind = i % 4
        if kind == 0:
            qkv = matmul(h, na_w_qkv[0])
            qp, ql = _split(qkv[:, :D_MODEL])
            kp, kl = _split(qkv[:, D_MODEL:2 * D_MODEL])
            vp, vl = _split(qkv[:, 2 * D_MODEL:])
            shp = (BATCH, SEQ, NA_HEADS, HEAD_DIM)
            shl = (DEC_BATCH, DEC_SEQ, NA_HEADS, HEAD_DIM)
            kp, vp = kp.reshape(shp), vp.reshape(shp)
            yp = _block_attention(qp.reshape(shp), kp, vp)
            yl = _na_latent_core(ql.reshape(shl), kl.reshape(shl), vl.reshape(shl),
                                 cache_na_k[:, 0], cache_na_v[:, 0], na_rpb[0])
            outs['na_k'], outs['na_v'] = kp[:, None], vp[:, None]
            pre = jnp.concatenate([yp.reshape(N_CTX, D_MODEL), yl.reshape(N_LAT, D_MODEL)], axis=0)
            y = matmul(pre.astype(BF16), na_w_o[0])
        elif kind == 1:
            proj = matmul(h, lru_w_in[0])
            pp, pq = _split(proj)
            zero_state = jnp.zeros((BATCH, 2, D_RNN), F32)
            yp, sp = _rglru_core(pp.reshape(BATCH, SEQ, 2 * D_RNN), zero_state, lru_conv_w[0], lru_conv_b[0],
                                 lru_w_gates[0], lru_b_gates[0], lru_lambda[0])
            yl, _ = _rglru_core(pq.reshape(DEC_BATCH, DEC_SEQ, 2 * D_RNN), state_lru[:, 0], lru_conv_w[0],
                                lru_conv_b[0], lru_w_gates[0], lru_b_gates[0], lru_lambda[0])
            outs['lru'] = sp[:, None]
            pre = jnp.concatenate([yp.reshape(N_CTX, D_RNN), yl.reshape(N_LAT, D_RNN)], axis=0)
            y = matmul(pre.astype(BF16), lru_w_o[0])
        elif kind == 2:
            proj = matmul(h, ml_w_in[0])
            wg = jnp.pad(ml_w_gates[0], ((0, 0), (0, 128 - 4 * ML_HEADS)))
            gates = matmul(h, wg, tn=128)[:, :4 * ML_HEADS]
            pp, pq = _split(proj)
            gp, gq = _split(gates)
            zc = jnp.zeros((BATCH, 2, ML_HEADS, ML_DK, ML_DV), F32)
            zn = jnp.zeros((BATCH, 2, ML_HEADS, ML_DK), F32)
            zm = jnp.zeros((BATCH, 2, ML_HEADS), F32)
            yp, cp, np_, mp = _mlstm_core(pp.reshape(BATCH, SEQ, -1), gp.reshape(BATCH, SEQ, -1), zc, zn, zm,
                                          ml_b_gates[0], ml_norm_g[0])
            yl, _, _, _ = _mlstm_core(pq.reshape(DEC_BATCH, DEC_SEQ, -1), gq.reshape(DEC_BATCH, DEC_SEQ, -1),
                                      state_mlstm_C[:, 0], state_mlstm_n[:, 0], state_mlstm_m[:, 0],
                                      ml_b_gates[0], ml_norm_g[0])
            outs['mc'], outs['mn'], outs['mm'] = cp[:, None], np_[:, None], mp[:, None]
            pre = jnp.concatenate([yp.reshape(N_CTX, -1), yl.reshape(N_LAT, -1)], axis=0)
            y = matmul(pre.astype(BF16), ml_w_o[0])
        else:
            qkv = matmul(h, gqa_w_qkv[0])
            nq = GQA_HEADS * HEAD_DIM
            nk = GQA_KV_HEADS * HEAD_DIM
            qp, ql = _split(qkv[:, :nq])
            kp, kl = _split(qkv[:, nq:nq + nk])
            vp, vl = _split(qkv[:, nq + nk:])
            qp = _head_rms(qp.reshape(BATCH, SEQ, GQA_HEADS, HEAD_DIM), gqa_q_norm[0])
            kp = _head_rms(kp.reshape(BATCH, SEQ, GQA_KV_HEADS, HEAD_DIM), gqa_k_norm[0])
            vp = vp.reshape(BATCH, SEQ, GQA_KV_HEADS, HEAD_DIM)
            yp = _block_attention(qp, kp, vp)
            ql = _axial_rope(_head_rms(ql.reshape(DEC_BATCH, DEC_SEQ, GQA_HEADS, HEAD_DIM), gqa_q_norm[0]))
            kl = _axial_rope(_head_rms(kl.reshape(DEC_BATCH, DEC_SEQ, GQA_KV_HEADS, HEAD_DIM), gqa_k_norm[0]))
            vl = vl.reshape(DEC_BATCH, DEC_SEQ, GQA_KV_HEADS, HEAD_DIM)
            kk = jnp.concatenate([cache_gqa_k[:, 0], kl], axis=1)
            vv = jnp.concatenate([cache_gqa_v[:, 0], vl], axis=1)
            yl = _block_attention(ql, kk, vv)
            outs['gk'], outs['gv'] = kp[:, None], vp[:, None]
            pre = jnp.concatenate([yp.reshape(N_CTX, D_MODEL), yl.reshape(N_LAT, D_MODEL)], axis=0)
            y = matmul(pre.astype(BF16), gqa_w_o[0])

        x, h2 = resid_norm_mod(x, y, gains, mods, layer=i, ga=1, gate=2, nxt=(i, 2, 3, 4))
        u = matmul(h2, mlp_up[i], out_dtype=BF16, act="relu2")
        z = matmul(u, mlp_down[i])
        nxt = (i + 1, 0, 0, 1) if i + 1 < DEPTH else None
        x, h = resid_norm_mod(x, z, gains, mods, layer=i, ga=3, gate=5, nxt=nxt)

    xp, xl = _split(x)
    return (xp.reshape(BATCH, SEQ, D_MODEL), xl.reshape(DEC_BATCH, DEC_SEQ, D_MODEL),
            outs['na_k'], outs['na_v'], outs['lru'], outs['mc'], outs['mn'], outs['mm'], outs['gk'], outs['gv'])
```

```python
import functools

import jax
import jax.numpy as jnp
from jax import lax
from jax.experimental import pallas as pl
from jax.experimental.pallas import tpu as pltpu

D_MODEL = 4096
BATCH = 16
SEQ = 256
DEPTH = 4
DEC_BATCH = 2
DEC_SEQ = 1024
PAST_LEN = 512
GRID_W = 64
HEAD_DIM = 128
NA_HEADS = 32
WIN_R = 8
WIN_C = 16
GQA_HEADS = 32
GQA_KV_HEADS = 8
GQA_GROUP = GQA_HEADS // GQA_KV_HEADS
ROPE_THETA = 10000.0
D_RNN = D_MODEL
LRU_BLOCKS = 16
LRU_BW = 256
CONV_W = 4
LRU_C = 8.0
ML_HEADS = 8
ML_DK = 256
ML_DV = 512
ML_CHUNK = 128
D_FF = 4 * D_MODEL
EPS = 1e-6

N_CTX = BATCH * SEQ
N_LAT = DEC_BATCH * DEC_SEQ
N_TOK = N_CTX + N_LAT
MOD_ROWS = 8
LANES = 128
NEG_BIG = -1e30

VMEM_LIMIT = 56 * 1024 * 1024

BF16 = jnp.bfloat16
F32 = jnp.float32


def _cparams(sem):
    return pltpu.CompilerParams(dimension_semantics=sem, vmem_limit_bytes=VMEM_LIMIT)


def _dot(a, b):
    return jnp.dot(a, b, preferred_element_type=F32)


def _dot_nt(a, b):
    return lax.dot_general(a, b, (((1,), (1,)), ((), ())), preferred_element_type=F32)


def _dot_tn(a, b):
    return lax.dot_general(a, b, (((0,), (0,)), ((), ())), preferred_element_type=F32)


def _softplus(x):
    return jnp.maximum(x, 0.0) + jnp.log1p(jnp.exp(-jnp.abs(x)))


def _mm_kernel(a_ref, w_ref, o_ref, *, act):
    acc = _dot(a_ref[...], w_ref[...].astype(BF16))
    if act == "relu2":
        r = jnp.maximum(acc, 0.0)
        acc = r * r
    o_ref[...] = acc.astype(o_ref.dtype)


def _mm_kernel_kgrid(a_ref, w_ref, o_ref, acc_ref, *, nk):
    k = pl.program_id(2)
    part = _dot(a_ref[...], w_ref[...].astype(BF16))

    @pl.when(k == 0)
    def _():
        acc_ref[...] = part

    @pl.when(k > 0)
    def _():
        acc_ref[...] += part

    @pl.when(k == nk - 1)
    def _():
        o_ref[...] = acc_ref[...].astype(o_ref.dtype)


def matmul(a, w, *, out_dtype=F32, act=None, tm=1024, tn=512, tk=4096):
    m, k = a.shape
    k2, n = w.shape
    assert k == k2 and a.dtype == BF16
    tm, tn, tk = min(tm, m), min(tn, n), min(tk, k)
    assert m % tm == 0 and n % tn == 0 and k % tk == 0
    nk = k // tk
    if nk == 1:
        return pl.pallas_call(
            functools.partial(_mm_kernel, act=act),
            grid=(m // tm, n // tn),
            in_specs=[pl.BlockSpec((tm, k), lambda i, j: (i, 0)),
                      pl.BlockSpec((k, tn), lambda i, j: (0, j))],
            out_specs=pl.BlockSpec((tm, tn), lambda i, j: (i, j)),
            out_shape=jax.ShapeDtypeStruct((m, n), out_dtype),
            compiler_params=_cparams(("parallel", "arbitrary")),
        )(a, w)
    assert act is None
    return pl.pallas_call(
        functools.partial(_mm_kernel_kgrid, nk=nk),
        grid=(m // tm, n // tn, nk),
        in_specs=[pl.BlockSpec((tm, tk), lambda i, j, kk: (i, kk)),
                  pl.BlockSpec((tk, tn), lambda i, j, kk: (kk, j))],
        out_specs=pl.BlockSpec((tm, tn), lambda i, j, kk: (i, j)),
        out_shape=jax.ShapeDtypeStruct((m, n), out_dtype),
        scratch_shapes=[pltpu.VMEM((tm, tn), F32)],
        compiler_params=_cparams(("parallel", "arbitrary", "arbitrary")),
    )(a, w)


def _ada_kernel(c_ref, w_ref, b_ref, o_ref):
    c = c_ref[...]
    a = (c * jax.nn.sigmoid(c)).astype(BF16)
    o_ref[...] = _dot(a, w_ref[...].astype(BF16)) + b_ref[...]


def adaln_all(cond, ada_w, ada_b, *, tn=512):
    n = 6 * D_MODEL
    out = pl.pallas_call(
        _ada_kernel,
        grid=(DEPTH, n // tn),
        in_specs=[pl.BlockSpec((MOD_ROWS, D_MODEL), lambda l, j: (0, 0)),
                  pl.BlockSpec((None, D_MODEL, tn), lambda l, j: (l, 0, j)),
                  pl.BlockSpec((None, 1, tn), lambda l, j: (l, 0, j))],
        out_specs=pl.BlockSpec((None, MOD_ROWS, tn), lambda l, j: (l, 0, j)),
        out_shape=jax.ShapeDtypeStruct((DEPTH, MOD_ROWS, n), F32),
        compiler_params=_cparams(("parallel", "arbitrary")),
    )(cond, ada_w, ada_b.reshape(DEPTH, 1, n))
    return out.reshape(DEPTH, MOD_ROWS, 6, 1, D_MODEL)


ROW_TILE = 256


def _row_group(i):
    return jnp.maximum(0, (i * ROW_TILE - N_CTX) // DEC_SEQ + 1)


def _rms(x, g):
    return x * lax.rsqrt(jnp.mean(x * x, axis=-1, keepdims=True) + EPS) * g


def _mod_spec(layer, which):
    return pl.BlockSpec((None, None, None, 1, D_MODEL),
                        lambda i: (layer, _row_group(i), which, 0, 0))


def _gain_spec(layer, which):
    return pl.BlockSpec((None, None, 1, D_MODEL), lambda i: (layer, which, 0, 0))


_ROWS_SPEC = pl.BlockSpec((ROW_TILE, D_MODEL), lambda i: (i, 0))


def _norm_mod_kernel(x_ref, g_ref, sh_ref, sc_ref, h_ref):
    h = _rms(x_ref[...], g_ref[...]) * (1.0 + sc_ref[...]) + sh_ref[...]
    h_ref[...] = h.astype(h_ref.dtype)


def norm_mod(x, gains, mods, layer):
    return pl.pallas_call(
        _norm_mod_kernel,
        grid=(N_TOK // ROW_TILE,),
        in_specs=[_ROWS_SPEC, _gain_spec(layer, 0), _mod_spec(layer, 0), _mod_spec(layer, 1)],
        out_specs=_ROWS_SPEC,
        out_shape=jax.ShapeDtypeStruct((N_TOK, D_MODEL), BF16),
        compiler_params=_cparams(("parallel",)),
    )(x, gains, mods, mods)


def _resid_kernel(x_ref, y_ref, ga_ref, gate_ref, *rest, with_h):
    xn = x_ref[...] + gate_ref[...] * _rms(y_ref[...], ga_ref[...])
    if with_h:
        gb_ref, sh_ref, sc_ref, xo_ref, h_ref = rest
        h = _rms(xn, gb_ref[...]) * (1.0 + sc_ref[...]) + sh_ref[...]
        h_ref[...] = h.astype(h_ref.dtype)
    else:
        (xo_ref,) = rest
    xo_ref[...] = xn


def resid_norm_mod(x, y, gains, mods, *, layer, ga, gate, nxt):
    in_specs = [_ROWS_SPEC, _ROWS_SPEC, _gain_spec(layer, ga), _mod_spec(layer, gate)]
    args = [x, y, gains, mods]
    out_specs = [_ROWS_SPEC]
    out_shape = [jax.ShapeDtypeStruct((N_TOK, D_MODEL), F32)]
    if nxt is not None:
        nl, ng, nsh, nsc = nxt
        in_specs += [_gain_spec(nl, ng), _mod_spec(nl, nsh), _mod_spec(nl, nsc)]
        args += [gains, mods, mods]
        out_specs.append(_ROWS_SPEC)
        out_shape.append(jax.ShapeDtypeStruct((N_TOK, D_MODEL), BF16))
    res = pl.pallas_call(
        functools.partial(_resid_kernel, with_h=nxt is not None),
        grid=(N_TOK // ROW_TILE,),
        in_specs=in_specs,
        out_specs=out_specs,
        out_shape=out_shape,
        compiler_params=_cparams(("parallel",)),
    )(*args)
    return (res[0], res[1]) if nxt is not None else (res[0], None)


def _head_rms(x, g):
    return x * lax.rsqrt(jnp.mean(x * x, axis=-1, keepdims=True) + EPS) * g


def _rope(x, cos, sin):
    lane = lax.broadcasted_iota(jnp.int32, x.shape, 1)
    partner = jnp.where((lane & 63) < 32, pltpu.roll(x, LANES - 32, 1), pltpu.roll(x, 32, 1))
    return x * cos + partner * sin


def _rope_tables():
    t = jnp.arange(DEC_SEQ)
    half = HEAD_DIM // 2
    inv_freq = 1.0 / (ROPE_THETA ** (jnp.arange(0, half, 2, dtype=F32) / half))
    ang_r = (t // GRID_W).astype(F32)[:, None] * inv_freq
    ang_c = (t % GRID_W).astype(F32)[:, None] * inv_freq
    cos = jnp.concatenate([jnp.cos(ang_r)] * 2 + [jnp.cos(ang_c)] * 2, axis=-1)
    sin = jnp.concatenate([-jnp.sin(ang_r), jnp.sin(ang_r), -jnp.sin(ang_c), jnp.sin(ang_c)], axis=-1)
    return cos, sin


def _softmax_pv(scores, values):
    m = functools.reduce(jnp.maximum, [jnp.max(s, axis=-1, keepdims=True) for s in scores])
    es = [jnp.exp(s - m) for s in scores]
    l = functools.reduce(jnp.add, [jnp.sum(e, axis=-1, keepdims=True) for e in es])
    o = functools.reduce(jnp.add, [_dot(e.astype(BF16), v) for e, v in zip(es, values)])
    return o / l


def _ctx_attn_kernel(q_ref, k_ref, v_ref, *rest, n_kv, group, normed):
    if normed:
        qg_ref, kg_ref, o_ref, ko_ref, vo_ref = rest
    else:
        (o_ref,) = rest
    scale = HEAD_DIM ** -0.5
    for j in range(n_kv):
        ks = slice(j * HEAD_DIM, (j + 1) * HEAD_DIM)
        k = k_ref[:, ks]
        v = v_ref[:, ks]
        if normed:
            k = _head_rms(k, kg_ref[...])
            ko_ref[:, ks] = k
            vo_ref[:, ks] = v
        kb = k.astype(BF16)
        vb = v.astype(BF16)
        for g in range(group):
            qs = slice((j * group + g) * HEAD_DIM, (j * group + g + 1) * HEAD_DIM)
            q = q_ref[:, qs]
            if normed:
                q = _head_rms(q, qg_ref[...])
            s = _dot_nt(q.astype(BF16), kb) * scale
            o_ref[:, qs] = _softmax_pv([s], [vb]).astype(o_ref.dtype)


CTX_COLS = 1024


def na_context_attn(qkv):
    nb = D_MODEL // CTX_COLS
    return pl.pallas_call(
        functools.partial(_ctx_attn_kernel, n_kv=CTX_COLS // HEAD_DIM, group=1, normed=False),
        grid=(BATCH, nb),
        in_specs=[pl.BlockSpec((SEQ, CTX_COLS), lambda b, j: (b, j)),
                  pl.BlockSpec((SEQ, CTX_COLS), lambda b, j: (b, nb + j)),
                  pl.BlockSpec((SEQ, CTX_COLS), lambda b, j: (b, 2 * nb + j))],
        out_specs=pl.BlockSpec((SEQ, CTX_COLS), lambda b, j: (b, j)),
        out_shape=jax.ShapeDtypeStruct((N_CTX, D_MODEL), BF16),
        compiler_params=_cparams(("parallel", "arbitrary")),
    )(qkv, qkv, qkv)


def gqa_context_attn(qkv, q_gain, k_gain):
    n_kv = CTX_COLS // (GQA_GROUP * HEAD_DIM)
    kv_cols = n_kv * HEAD_DIM
    nb = D_MODEL // CTX_COLS
    k0 = D_MODEL // kv_cols
    v0 = k0 + GQA_KV_HEADS * HEAD_DIM // kv_cols
    kv_shape = jax.ShapeDtypeStruct((N_CTX, GQA_KV_HEADS * HEAD_DIM), F32)
    gain_spec = pl.BlockSpec((1, HEAD_DIM), lambda b, j: (0, 0))
    return pl.pallas_call(
        functools.partial(_ctx_attn_kernel, n_kv=n_kv, group=GQA_GROUP, normed=True),
        grid=(BATCH, nb),
        in_specs=[pl.BlockSpec((SEQ, CTX_COLS), lambda b, j: (b, j)),
                  pl.BlockSpec((SEQ, kv_cols), lambda b, j: (b, k0 + j)),
                  pl.BlockSpec((SEQ, kv_cols), lambda b, j: (b, v0 + j)),
                  gain_spec, gain_spec],
        out_specs=[pl.BlockSpec((SEQ, CTX_COLS), lambda b, j: (b, j)),
                   pl.BlockSpec((SEQ, kv_cols), lambda b, j: (b, j)),
                   pl.BlockSpec((SEQ, kv_cols), lambda b, j: (b, j))],
        out_shape=[jax.ShapeDtypeStruct((N_CTX, D_MODEL), BF16), kv_shape, kv_shape],
        compiler_params=_cparams(("parallel", "arbitrary")),
    )(qkv, qkv, qkv, q_gain.reshape(1, HEAD_DIM), k_gain.reshape(1, HEAD_DIM))


def _gqa_lat_kernel(q_ref, k_ref, v_ref, kc_ref, vc_ref, cos_ref, sin_ref, qg_ref, kg_ref, o_ref):
    scale = HEAD_DIM ** -0.5
    cos, sin = cos_ref[...], sin_ref[...]
    kc = kc_ref[...].astype(BF16)
    vc = vc_ref[...].astype(BF16)
    k = _rope(_head_rms(k_ref[...], kg_ref[...]), cos, sin).astype(BF16)
    v = v_ref[...].astype(BF16)
    for g in range(GQA_GROUP):
        qs = slice(g * HEAD_DIM, (g + 1) * HEAD_DIM)
        q = _rope(_head_rms(q_ref[:, qs], qg_ref[...]), cos, sin).astype(BF16)
        s_ctx = _dot_nt(q, kc) * scale
        s_lat = _dot_nt(q, k) * scale
        o_ref[:, qs] = _softmax_pv([s_ctx, s_lat], [vc, v]).astype(o_ref.dtype)


def gqa_latent_attn(qkv, cache_k, cache_v, q_gain, k_gain):
    cos, sin = _rope_tables()
    qcols = GQA_GROUP * HEAD_DIM
    rb0 = N_CTX // DEC_SEQ
    k0 = D_MODEL // HEAD_DIM
    v0 = k0 + GQA_KV_HEADS
    cache_spec = pl.BlockSpec((None, PAST_LEN, HEAD_DIM), lambda b, j: (b, 0, j))
    table_spec = pl.BlockSpec((DEC_SEQ, HEAD_DIM), lambda b, j: (0, 0))
    gain_spec = pl.BlockSpec((1, HEAD_DIM), lambda b, j: (0, 0))
    return pl.pallas_call(
        _gqa_lat_kernel,
        grid=(DEC_BATCH, GQA_KV_HEADS),
        in_specs=[pl.BlockSpec((DEC_SEQ, qcols), lambda b, j: (rb0 + b, j)),
                  pl.BlockSpec((DEC_SEQ, HEAD_DIM), lambda b, j: (rb0 + b, k0 + j)),
                  pl.BlockSpec((DEC_SEQ, HEAD_DIM), lambda b, j: (rb0 + b, v0 + j)),
                  cache_spec, cache_spec, table_spec, table_spec, gain_spec, gain_spec],
        out_specs=pl.BlockSpec((DEC_SEQ, qcols), lambda b, j: (b, j)),
        out_shape=jax.ShapeDtypeStruct((N_LAT, D_MODEL), BF16),
        compiler_params=_cparams(("parallel", "arbitrary")),
    )(qkv, qkv, qkv, cache_k, cache_v, cos, sin, q_gain.reshape(1, HEAD_DIM), k_gain.reshape(1, HEAD_DIM))


NA_ROWS = DEC_SEQ // GRID_W
NA_WR = min(WIN_R, NA_ROWS)
NA_WKEYS = NA_WR * GRID_W


def _na_bias_table(rpb):
    r = jnp.arange(NA_ROWS)
    rs = jnp.clip(r - NA_WR // 2, 0, NA_ROWS - NA_WR)
    key_rows = rs[:, None] + jnp.arange(NA_WR)[None, :]
    col = jnp.arange(GRID_W)
    cs = jnp.clip(col - WIN_C // 2, 0, GRID_W - WIN_C)
    col_ok = (col[None, :] >= cs[:, None]) & (col[None, :] < cs[:, None] + WIN_C)
    dc = jnp.clip(col[None, :] - col[:, None] + WIN_C - 1, 0, 2 * WIN_C - 2)
    dr = key_rows - r[:, None] + WIN_R - 1
    bias = rpb[:, dr][..., dc]
    bias = jnp.where(col_ok[None, None, None], bias, NEG_BIG)
    return bias.transpose(0, 1, 3, 2, 4).reshape(NA_HEADS, NA_ROWS, GRID_W, NA_WKEYS).astype(F32)


def _na_lat_kernel(q_ref, k_ref, v_ref, kc_ref, vc_ref, bias_ref, o_ref, kb_ref, vb_ref):
    scale = HEAD_DIM ** -0.5
    kc = kc_ref[...].astype(BF16)
    vc = vc_ref[...].astype(BF16)
    kb_ref[...] = k_ref[...].astype(BF16)
    vb_ref[...] = v_ref[...].astype(BF16)

    def row(r, carry):
        rs = jnp.clip(r - NA_WR // 2, 0, NA_ROWS - NA_WR)
        q_rows = pl.ds(pl.multiple_of(r * GRID_W, GRID_W), GRID_W)
        k_rows = pl.ds(pl.multiple_of(rs * GRID_W, GRID_W), NA_WKEYS)
        q = q_ref[q_rows, :].astype(BF16)
        s_lat = _dot_nt(q, kb_ref[k_rows, :]) * scale + bias_ref[r]
        s_ctx = _dot_nt(q, kc) * scale
        o = _softmax_pv([s_lat, s_ctx], [vb_ref[k_rows, :], vc])
        o_ref[q_rows, :] = o.astype(o_ref.dtype)
        return carry

    lax.fori_loop(0, NA_ROWS, row, 0)


def na_latent_attn(qkv, cache_k, cache_v, rpb):
    bias = _na_bias_table(rpb)
    rb0 = N_CTX // DEC_SEQ
    cache_spec = pl.BlockSpec((None, PAST_LEN, HEAD_DIM), lambda h, b: (b, 0, h))
    return pl.pallas_call(
        _na_lat_kernel,
        grid=(NA_HEADS, DEC_BATCH),
        in_specs=[pl.BlockSpec((DEC_SEQ, HEAD_DIM), lambda h, b: (rb0 + b, h)),
                  pl.BlockSpec((DEC_SEQ, HEAD_DIM), lambda h, b: (rb0 + b, NA_HEADS + h)),
                  pl.BlockSpec((DEC_SEQ, HEAD_DIM), lambda h, b: (rb0 + b, 2 * NA_HEADS + h)),
                  cache_spec, cache_spec,
                  pl.BlockSpec((None, NA_ROWS, GRID_W, NA_WKEYS), lambda h, b: (h, 0, 0, 0))],
        out_specs=pl.BlockSpec((DEC_SEQ, HEAD_DIM), lambda h, b: (b, h)),
        out_shape=jax.ShapeDtypeStruct((N_LAT, D_MODEL), BF16),
        scratch_shapes=[pltpu.VMEM((DEC_SEQ, HEAD_DIM), BF16), pltpu.VMEM((DEC_SEQ, HEAD_DIM), BF16)],
        compiler_params=_cparams(("parallel", "arbitrary")),
    )(qkv, qkv, qkv, cache_k, cache_v, bias)


LRU_ROWS = 2048
LRU_CTX_STEPS = N_CTX // LRU_ROWS
LRU_SEQ_PAD = 8


def _lru_body(xr_ref, gb_ref, cw_ref, cb_ref, wg_ref, bg_ref, lam_ref, h0_ref, y_ref, fin_ref,
              af_ref, uf_ref, ab_ref, ub_ref, *, n_seq, seq_len):
    rows = n_seq * seq_len
    halves = [slice(c * LANES, (c + 1) * LANES) for c in range(LRU_BW // LANES)]
    x = xr_ref[...]
    t = lax.broadcasted_iota(jnp.int32, (rows, LRU_BW), 0) & (seq_len - 1)
    cw = cw_ref[...]
    xf = (jnp.where(t >= 2, pltpu.roll(x, 2, 0), 0.0) * cw[0:1]
          + jnp.where(t >= 1, pltpu.roll(x, 1, 0), 0.0) * cw[1:2]
          + x * cw[2:3]
          + jnp.where(t < seq_len - 1, pltpu.roll(x, rows - 1, 0), 0.0) * cw[3:4]) + cb_ref[...]
    xb = xf.astype(BF16)
    sp = _softplus(-lam_ref[...])
    for d, (a_ref, u_ref) in enumerate(((af_ref, uf_ref), (ab_ref, ub_ref))):
        bg = bg_ref[d]
        r_gate = jax.nn.sigmoid(_dot(xb, wg_ref[d, 0].astype(BF16)) + bg[0:1])
        i_gate = jax.nn.sigmoid(_dot(xb, wg_ref[d, 1].astype(BF16)) + bg[1:2])
        a = jnp.exp(-LRU_C * r_gate * sp[d:d + 1])
        u = jnp.sqrt(1.0 - a * a) * i_gate * xf
        for c, cols in enumerate(halves):
            a_ref[c] = a[:, cols]
            u_ref[c] = u[:, cols]

    def step(i, carry):
        rf = pl.ds(i, n_seq, stride=seq_len)
        rb = pl.ds(seq_len - 1 - i, n_seq, stride=seq_len)
        out = []
        for c in range(len(halves)):
            hf = af_ref[c, rf, :] * carry[2 * c] + uf_ref[c, rf, :]
            uf_ref[c, rf, :] = hf
            hb = ab_ref[c, rb, :] * carry[2 * c + 1] + ub_ref[c, rb, :]
            ub_ref[c, rb, :] = hb
            out += [hf, hb]
        return tuple(out)

    init = tuple(h0_ref[d, 0:n_seq, cols] for cols in halves for d in range(2))
    fin = lax.fori_loop(0, seq_len, step, init, unroll=4)
    fin_ref[...] = jnp.zeros(fin_ref.shape, F32)
    gate = jax.nn.gelu(gb_ref[...])
    for c, cols in enumerate(halves):
        fin_ref[0, 0:n_seq, cols] = fin[2 * c]
        fin_ref[1, 0:n_seq, cols] = fin[2 * c + 1]
        y_ref[:, cols] = ((uf_ref[c] + ub_ref[c]) * gate[:, cols]).astype(y_ref.dtype)


def _lru_kernel(*refs):
    i = pl.program_id(0)

    @pl.when(i < LRU_CTX_STEPS)
    def _():
        _lru_body(*refs, n_seq=LRU_ROWS // SEQ, seq_len=SEQ)

    @pl.when(i >= LRU_CTX_STEPS)
    def _():
        _lru_body(*refs, n_seq=LRU_ROWS // DEC_SEQ, seq_len=DEC_SEQ)


def rglru_mix(proj, state, conv_w, conv_b, w_gates, b_gates, lam):
    assert LRU_ROWS // SEQ == LRU_SEQ_PAD and LRU_ROWS // DEC_SEQ == DEC_BATCH
    n_steps = N_TOK // LRU_ROWS
    h0 = jnp.zeros((2, n_steps * LRU_SEQ_PAD, D_RNN), F32)
    h0 = h0.at[:, BATCH:BATCH + DEC_BATCH].set(state.transpose(1, 0, 2))
    blk = lambda i, n: (i, n)
    chan = lambda i, n: (0, n)
    y, fin = pl.pallas_call(
        _lru_kernel,
        grid=(n_steps, LRU_BLOCKS),
        in_specs=[pl.BlockSpec((LRU_ROWS, LRU_BW), blk),
                  pl.BlockSpec((LRU_ROWS, LRU_BW), lambda i, n: (i, LRU_BLOCKS + n)),
                  pl.BlockSpec((CONV_W, LRU_BW), chan),
                  pl.BlockSpec((1, LRU_BW), chan),
                  pl.BlockSpec((2, 2, None, LRU_BW, LRU_BW), lambda i, n: (0, 0, n, 0, 0)),
                  pl.BlockSpec((2, 2, LRU_BW), lambda i, n: (0, 0, n)),
                  pl.BlockSpec((2, LRU_BW), chan),
                  pl.BlockSpec((2, LRU_SEQ_PAD, LRU_BW), lambda i, n: (0, i, n))],
        out_specs=[pl.BlockSpec((LRU_ROWS, LRU_BW), blk),
                   pl.BlockSpec((2, LRU_SEQ_PAD, LRU_BW), lambda i, n: (0, i, n))],
        out_shape=[jax.ShapeDtypeStruct((N_TOK, D_RNN), BF16),
                   jax.ShapeDtypeStruct((2, n_steps * LRU_SEQ_PAD, D_RNN), F32)],
        scratch_shapes=[pltpu.VMEM((LRU_BW // LANES, LRU_ROWS, LANES), F32)] * 4,
        compiler_params=_cparams(("parallel", "arbitrary")),
    )(proj, proj, conv_w, conv_b.reshape(1, D_RNN), w_gates, b_gates, lam, h0)
    return y, fin[:, :BATCH].transpose(1, 0, 2)


ML_GATE_COLS = 4 * ML_HEADS


def _pick_lane(x, idx):
    lane = lax.broadcasted_iota(jnp.int32, x.shape, 1)
    return jnp.sum(jnp.where(lane == idx, x, 0.0), axis=1, keepdims=True)


def _pick_row(x, idx):
    row = lax.broadcasted_iota(jnp.int32, x.shape, 0)
    return jnp.sum(jnp.where(row == idx, x, 0.0), axis=0, keepdims=True)


def _mlstm_kernel(q_ref, k_ref, v_ref, og_ref, g_ref, gt_ref, bg_ref, bgt_ref, ng_ref, *rest,
                  seq_len, zero_init, with_state):
    rest = list(rest)
    if not zero_init:
        c0_ref, n0_ref, m0_ref = rest[:3]
        rest = rest[3:]
    y_ref = rest.pop(0)
    if with_state:
        co_ref, no_ref, mo_ref = rest[:3]
        rest = rest[3:]
    hsum_ref, c_ref, n_ref = rest
    head = pl.program_id(1)
    n_chunks = seq_len // ML_CHUNK
    sub = lax.broadcasted_iota(jnp.int32, (ML_CHUNK, ML_CHUNK), 0)
    lane = lax.broadcasted_iota(jnp.int32, (ML_CHUNK, ML_CHUNK), 1)

    for d in range(2):
        valid = (lane <= sub) if d == 0 else (lane >= sub)
        valid_t = (sub <= lane) if d == 0 else (sub >= lane)
        if zero_init:
            c_ref[...] = jnp.zeros(c_ref.shape, F32)
            n_ref[...] = jnp.zeros(n_ref.shape, F32)
            m_init = jnp.zeros((1, 1), F32)
        else:
            c_ref[...] = c0_ref[d]
            n_ref[...] = n0_ref[d]
            m_init = m0_ref[d]
        i_col = d * 2 * ML_HEADS + head
        f_col = i_col + ML_HEADS

        def chunk(ci, m_prev, d=d, valid=valid, valid_t=valid_t, i_col=i_col, f_col=f_col):
            c_idx = ci if d == 0 else n_chunks - 1 - ci
            rows = pl.ds(pl.multiple_of(c_idx * ML_CHUNK, ML_CHUNK), ML_CHUNK)
            g = g_ref[rows, :] + bg_ref[...]
            gt = gt_ref[:, rows] + bgt_ref[...]
            li_col = _pick_lane(g, i_col)
            lf_col = -_softplus(-_pick_lane(g, f_col))
            li_row = _pick_row(gt, i_col)
            lf_row = -_softplus(-_pick_row(gt, f_col))
            bcum_col = jnp.sum(jnp.where(valid, lf_row, 0.0), axis=1, keepdims=True)
            bcum_row = jnp.sum(jnp.where(valid_t, lf_col, 0.0), axis=0, keepdims=True)
            b_last = jnp.sum(lf_row, axis=1, keepdims=True)
            dmat = jnp.where(valid, bcum_col - bcum_row + li_row, NEG_BIG)
            m_inter = bcum_col + m_prev
            m_t = jnp.maximum(m_inter, jnp.max(dmat, axis=1, keepdims=True))
            qf = q_ref[rows, :]
            qb = qf.astype(BF16)
            kf = k_ref[rows, :] * (ML_DK ** -0.5)
            vb = v_ref[rows, :].astype(BF16)
            s = _dot_nt(qb, kf.astype(BF16)) * jnp.exp(dmat - m_t)
            inter = jnp.exp(m_inter - m_t)
            num = _dot(s.astype(BF16), vb) + inter * _dot(qb, c_ref[...].astype(BF16))
            den = (jnp.sum(s, axis=1, keepdims=True)
                   + inter * jnp.sum(qf * n_ref[...], axis=1, keepdims=True))
            h_out = num / jnp.maximum(jnp.abs(den), jnp.exp(-m_t))
            if d == 0:
                hsum_ref[rows, :] = h_out
            else:
                hsum_ref[rows, :] += h_out
            dec = b_last - bcum_col + li_col
            m_new = jnp.maximum(b_last + m_prev, jnp.max(dec, axis=0, keepdims=True))
            kw = kf * jnp.exp(dec - m_new)
            carry_scale = jnp.exp(b_last + m_prev - m_new)
            c_ref[...] = carry_scale * c_ref[...] + _dot_tn(kw.astype(BF16), vb)
            n_ref[...] = carry_scale * n_ref[...] + jnp.sum(kw, axis=0, keepdims=True)
            return m_new

        m_fin = lax.fori_loop(0, n_chunks, chunk, m_init)
        if with_state:
            co_ref[d] = c_ref[...]
            no_ref[d] = n_ref[...]
            mo_ref[d] = m_fin

    hs = hsum_ref[...]
    hn = hs * lax.rsqrt(jnp.mean(hs * hs, axis=-1, keepdims=True) + EPS) * ng_ref[...]
    y_ref[...] = (jax.nn.sigmoid(og_ref[...]) * hn).astype(y_ref.dtype)


def mlstm_mix(proj, gates, gates_t, b_gates, norm_g, *, row0, n_seq, seq_len, state=None, with_state):
    rb0 = row0 // seq_len
    kq = ML_HEADS * ML_DK // ML_DK
    v0 = 2 * ML_HEADS * ML_DK // ML_DV
    o0 = v0 + ML_HEADS
    zero_init = state is None
    bias = jnp.pad(b_gates.reshape(1, ML_GATE_COLS), ((0, 0), (0, LANES - ML_GATE_COLS)))
    in_specs = [pl.BlockSpec((seq_len, ML_DK), lambda b, h: (rb0 + b, h)),
                pl.BlockSpec((seq_len, ML_DK), lambda b, h: (rb0 + b, kq + h)),
                pl.BlockSpec((seq_len, ML_DV), lambda b, h: (rb0 + b, v0 + h)),
                pl.BlockSpec((seq_len, ML_DV), lambda b, h: (rb0 + b, o0 + h)),
                pl.BlockSpec((seq_len, LANES), lambda b, h: (rb0 + b, 0)),
                pl.BlockSpec((ML_GATE_COLS, seq_len), lambda b, h: (0, rb0 + b)),
                pl.BlockSpec((1, LANES), lambda b, h: (0, 0)),
                pl.BlockSpec((ML_GATE_COLS, 1), lambda b, h: (0, 0)),
                pl.BlockSpec((1, ML_DV), lambda b, h: (0, h))]
    args = [proj, proj, proj, proj, gates, gates_t, bias, b_gates.reshape(ML_GATE_COLS, 1),
            norm_g.reshape(1, ML_HEADS * ML_DV)]
    c_spec = pl.BlockSpec((None, 2, None, ML_DK, ML_DV), lambda b, h: (b, 0, h, 0, 0))
    n_spec = pl.BlockSpec((None, 2, None, 1, ML_DK), lambda b, h: (b, 0, h, 0, 0))
    m_spec = pl.BlockSpec((None, 2, None, 1, 1), lambda b, h: (b, 0, h, 0, 0))
    if not zero_init:
        c0, n0, m0 = state
        in_specs += [c_spec, n_spec, m_spec]
        args += [c0, n0.reshape(n_seq, 2, ML_HEADS, 1, ML_DK), m0.reshape(n_seq, 2, ML_HEADS, 1, 1)]
    out_specs = [pl.BlockSpec((seq_len, ML_DV), lambda b, h: (b, h))]
    out_shape = [jax.ShapeDtypeStruct((n_seq * seq_len, ML_HEADS * ML_DV), BF16)]
    if with_state:
        out_specs += [c_spec, n_spec, m_spec]
        out_shape += [jax.ShapeDtypeStruct((n_seq, 2, ML_HEADS, ML_DK, ML_DV), F32),
                      jax.ShapeDtypeStruct((n_seq, 2, ML_HEADS, 1, ML_DK), F32),
                      jax.ShapeDtypeStruct((n_seq, 2, ML_HEADS, 1, 1), F32)]
    res = pl.pallas_call(
        functools.partial(_mlstm_kernel, seq_len=seq_len, zero_init=zero_init, with_state=with_state),
        grid=(n_seq, ML_HEADS),
        in_specs=in_specs,
        out_specs=out_specs,
        out_shape=out_shape,
        scratch_shapes=[pltpu.VMEM((seq_len, ML_DV), F32), pltpu.VMEM((ML_DK, ML_DV), F32),
                        pltpu.VMEM((1, ML_DK), F32)],
        compiler_params=_cparams(("parallel", "arbitrary")),
    )(*args)
    if not with_state:
        return res[0]
    y, c_fin, n_fin, m_fin = res
    return y, c_fin, n_fin.reshape(n_seq, 2, ML_HEADS, ML_DK), m_fin.reshape(n_seq, 2, ML_HEADS)


def kernel(x_prompt, x_sample, cache_na_k, cache_na_v, state_lru, state_mlstm_C, state_mlstm_n, state_mlstm_m, cache_gqa_k, cache_gqa_v, c, c_ctx, ada_w, ada_b, norm_g, mlp_up, mlp_down, na_w_qkv, na_rpb, na_w_o, lru_w_in, lru_conv_w, lru_conv_b, lru_w_gates, lru_b_gates, lru_lambda, lru_w_o, ml_w_in, ml_w_gates, ml_b_gates, ml_norm_g, ml_w_o, gqa_w_qkv, gqa_q_norm, gqa_k_norm, gqa_w_o):
    x = jnp.concatenate([x_prompt.reshape(N_CTX, D_MODEL), x_sample.reshape(N_LAT, D_MODEL)], axis=0)
    cond = jnp.concatenate([c_ctx[None], c, jnp.zeros((MOD_ROWS - 1 - DEC_BATCH, D_MODEL), F32)], axis=0)
    mods = adaln_all(cond, ada_w, ada_b)
    gains = norm_g.reshape(DEPTH, 4, 1, D_MODEL)

    outs = {}
    h = norm_mod(x, gains, mods, 0)
    for i in range(DEPTH):
        kind = i % 4
        if kind == 0:
            qkv = matmul(h, na_w_qkv[0])
            shp = (BATCH, 1, SEQ, NA_HEADS, HEAD_DIM)
            outs['na_k'] = qkv[:N_CTX, D_MODEL:2 * D_MODEL].reshape(shp)
            outs['na_v'] = qkv[:N_CTX, 2 * D_MODEL:].reshape(shp)
            yp = na_context_attn(qkv)
            yl = na_latent_attn(qkv, cache_na_k[:, 0].reshape(DEC_BATCH, PAST_LEN, D_MODEL),
                                cache_na_v[:, 0].reshape(DEC_BATCH, PAST_LEN, D_MODEL), na_rpb[0])
            y = matmul(jnp.concatenate([yp, yl], axis=0), na_w_o[0])
        elif kind == 1:
            proj = matmul(h, lru_w_in[0])
            pre, fin = rglru_mix(proj, state_lru[:, 0], lru_conv_w[0], lru_conv_b[0], lru_w_gates[0],
                                 lru_b_gates[0], lru_lambda[0])
            outs['lru'] = fin[:, None]
            y = matmul(pre, lru_w_o[0])
        elif kind == 2:
            proj = matmul(h, ml_w_in[0])
            wg = jnp.pad(ml_w_gates[0], ((0, 0), (0, LANES - ML_GATE_COLS)))
            gates = matmul(h, wg, tn=LANES)
            gates_t = gates[:, :ML_GATE_COLS].T
            yp, cp, np_, mp = mlstm_mix(proj, gates, gates_t, ml_b_gates[0], ml_norm_g[0], row0=0,
                                        n_seq=BATCH, seq_len=SEQ, with_state=True)
            yl = mlstm_mix(proj, gates, gates_t, ml_b_gates[0], ml_norm_g[0], row0=N_CTX,
                           n_seq=DEC_BATCH, seq_len=DEC_SEQ, with_state=False,
                           state=(state_mlstm_C[:, 0], state_mlstm_n[:, 0], state_mlstm_m[:, 0]))
            outs['mc'], outs['mn'], outs['mm'] = cp[:, None], np_[:, None], mp[:, None]
            y = matmul(jnp.concatenate([yp, yl], axis=0), ml_w_o[0])
        else:
            qkv = matmul(h, gqa_w_qkv[0])
            kv_cols = GQA_KV_HEADS * HEAD_DIM
            yp, kp, vp = gqa_context_attn(qkv, gqa_q_norm[0], gqa_k_norm[0])
            yl = gqa_latent_attn(qkv, cache_gqa_k[:, 0].reshape(DEC_BATCH, PAST_LEN, kv_cols),
                                 cache_gqa_v[:, 0].reshape(DEC_BATCH, PAST_LEN, kv_cols),
                                 gqa_q_norm[0], gqa_k_norm[0])
            shp = (BATCH, 1, SEQ, GQA_KV_HEADS, HEAD_DIM)
            outs['gk'], outs['gv'] = kp.reshape(shp), vp.reshape(shp)
            y = matmul(jnp.concatenate([yp, yl], axis=0), gqa_w_o[0])

        x, h2 = resid_norm_mod(x, y, gains, mods, layer=i, ga=1, gate=2, nxt=(i, 2, 3, 4))
        u = matmul(h2, mlp_up[i], out_dtype=BF16, act="relu2")
        z = matmul(u, mlp_down[i])
        nxt = (i + 1, 0, 0, 1) if i + 1 < DEPTH else None
        x, h = resid_norm_mod(x, z, gains, mods, layer=i, ga=3, gate=5, nxt=nxt)

    return (x[:N_CTX].reshape(BATCH, SEQ, D_MODEL), x[N_CTX:].reshape(DEC_BATCH, DEC_SEQ, D_MODEL),
            outs['na_k'], outs['na_v'], outs['lru'], outs['mc'], outs['mn'], outs['mm'], outs['gk'], outs['gv'])
```

```python
import functools

import jax
import jax.numpy as jnp
from jax import lax
from jax.experimental import pallas as pl
from jax.experimental.pallas import tpu as pltpu

D_MODEL = 4096
BATCH = 16
SEQ = 256
DEPTH = 4
DEC_BATCH = 2
DEC_SEQ = 1024
PAST_LEN = 512
GRID_W = 64
HEAD_DIM = 128
NA_HEADS = 32
WIN_R = 8
WIN_C = 16
GQA_HEADS = 32
GQA_KV_HEADS = 8
GQA_GROUP = GQA_HEADS // GQA_KV_HEADS
ROPE_THETA = 10000.0
D_RNN = D_MODEL
LRU_BLOCKS = 16
LRU_BW = 256
CONV_W = 4
LRU_C = 8.0
ML_HEADS = 8
ML_DK = 256
ML_DV = 512
ML_CHUNK = 128
D_FF = 4 * D_MODEL
EPS = 1e-6

N_CTX = BATCH * SEQ
N_LAT = DEC_BATCH * DEC_SEQ
N_TOK = N_CTX + N_LAT
MOD_ROWS = 8
LANES = 128
NEG_BIG = -1e30

VMEM_LIMIT = 56 * 1024 * 1024

BF16 = jnp.bfloat16
F32 = jnp.float32


def _cparams(sem):
    return pltpu.CompilerParams(dimension_semantics=sem, vmem_limit_bytes=VMEM_LIMIT)


def _dot(a, b):
    return jnp.dot(a, b, preferred_element_type=F32)


def _dot_nt(a, b):
    return lax.dot_general(a, b, (((1,), (1,)), ((), ())), preferred_element_type=F32)


def _dot_tn(a, b):
    return lax.dot_general(a, b, (((0,), (0,)), ((), ())), preferred_element_type=F32)


def _sigmoid(x):
    return 0.5 * (jnp.tanh(0.5 * x) + 1.0)


def _softplus(x):
    return jnp.maximum(x, 0.0) + jnp.log1p(jnp.exp(-jnp.abs(x)))


def _mm_kernel(*refs, act, n_first):
    w_ref, o_ref = refs[-2:]

    def tile(a_ref):
        acc = _dot(a_ref[...], w_ref[...].astype(BF16))
        if act == "relu2":
            r = jnp.maximum(acc, 0.0)
            acc = r * r
        o_ref[...] = acc.astype(o_ref.dtype)

    if len(refs) == 3:
        tile(refs[0])
    else:
        i = pl.program_id(0)
        pl.when(i < n_first)(lambda: tile(refs[0]))
        pl.when(i >= n_first)(lambda: tile(refs[1]))


def _mm_kernel_kgrid(a_ref, w_ref, o_ref, acc_ref, *, nk):
    k = pl.program_id(2)
    part = _dot(a_ref[...], w_ref[...].astype(BF16))

    @pl.when(k == 0)
    def _():
        acc_ref[...] = part

    @pl.when(k > 0)
    def _():
        acc_ref[...] += part

    @pl.when(k == nk - 1)
    def _():
        o_ref[...] = acc_ref[...].astype(o_ref.dtype)


def matmul(a, w, *, a2=None, out_dtype=F32, act=None, tm=1024, tn=512, tk=4096, name="matmul"):
    m1, k = a.shape
    m = m1 + (0 if a2 is None else a2.shape[0])
    k2, n = w.shape
    assert k == k2 and a.dtype == BF16
    tm, tn, tk = min(tm, m), min(tn, n), min(tk, k)
    assert m1 % tm == 0 and m % tm == 0 and n % tn == 0 and k % tk == 0
    nk = k // tk
    n_first = m1 // tm
    if nk == 1:
        a_specs = [pl.BlockSpec((tm, k), lambda i, j: (jnp.minimum(i, n_first - 1), 0))]
        a_args = [a]
        if a2 is not None:
            assert a2.dtype == BF16 and a2.shape[1] == k
            a_specs.append(pl.BlockSpec((tm, k), lambda i, j: (jnp.maximum(i - n_first, 0), 0)))
            a_args.append(a2)
        return pl.pallas_call(
            functools.partial(_mm_kernel, act=act, n_first=n_first),
            grid=(m // tm, n // tn),
            in_specs=a_specs + [pl.BlockSpec((k, tn), lambda i, j: (0, j))],
            out_specs=pl.BlockSpec((tm, tn), lambda i, j: (i, j)),
            out_shape=jax.ShapeDtypeStruct((m, n), out_dtype),
            compiler_params=_cparams(("parallel", "arbitrary")),
            name=name,
        )(*a_args, w)
    assert act is None and a2 is None
    return pl.pallas_call(
        functools.partial(_mm_kernel_kgrid, nk=nk),
        grid=(m // tm, n // tn, nk),
        in_specs=[pl.BlockSpec((tm, tk), lambda i, j, kk: (i, kk)),
                  pl.BlockSpec((tk, tn), lambda i, j, kk: (kk, j))],
        out_specs=pl.BlockSpec((tm, tn), lambda i, j, kk: (i, j)),
        out_shape=jax.ShapeDtypeStruct((m, n), out_dtype),
        scratch_shapes=[pltpu.VMEM((tm, tn), F32)],
        compiler_params=_cparams(("parallel", "arbitrary", "arbitrary")),
        name=name,
    )(a, w)


def _ada_kernel(c_ref, w_ref, b_ref, o_ref):
    c = c_ref[...]
    a = (c * jax.nn.sigmoid(c)).astype(BF16)
    o_ref[...] = _dot(a, w_ref[...].astype(BF16)) + b_ref[...]


def adaln_all(cond, ada_w, ada_b, *, tn=512):
    n = 6 * D_MODEL
    out = pl.pallas_call(
        _ada_kernel,
        grid=(DEPTH, n // tn),
        in_specs=[pl.BlockSpec((MOD_ROWS, D_MODEL), lambda l, j: (0, 0)),
                  pl.BlockSpec((None, D_MODEL, tn), lambda l, j: (l, 0, j)),
                  pl.BlockSpec((None, 1, tn), lambda l, j: (l, 0, j))],
        out_specs=pl.BlockSpec((None, MOD_ROWS, tn), lambda l, j: (l, 0, j)),
        out_shape=jax.ShapeDtypeStruct((DEPTH, MOD_ROWS, n), F32),
        compiler_params=_cparams(("parallel", "arbitrary")),
        name="adaln",
    )(cond, ada_w, ada_b.reshape(DEPTH, 1, n))
    return out.reshape(DEPTH, MOD_ROWS, 6, 1, D_MODEL)


ROW_TILE = 256


def _row_group(i):
    return jnp.maximum(0, (i * ROW_TILE - N_CTX) // DEC_SEQ + 1)


def _rms(x, g):
    return x * lax.rsqrt(jnp.mean(x * x, axis=-1, keepdims=True) + EPS) * g


def _mod_spec(layer, which):
    return pl.BlockSpec((None, None, None, 1, D_MODEL),
                        lambda i: (layer, _row_group(i), which, 0, 0))


def _gain_spec(layer, which):
    return pl.BlockSpec((None, None, 1, D_MODEL), lambda i: (layer, which, 0, 0))


_ROWS_SPEC = pl.BlockSpec((ROW_TILE, D_MODEL), lambda i: (i, 0))


def _norm_mod_kernel(x_ref, g_ref, sh_ref, sc_ref, h_ref):
    h = _rms(x_ref[...], g_ref[...]) * (1.0 + sc_ref[...]) + sh_ref[...]
    h_ref[...] = h.astype(h_ref.dtype)


def norm_mod(x, gains, mods, layer):
    return pl.pallas_call(
        _norm_mod_kernel,
        grid=(N_TOK // ROW_TILE,),
        in_specs=[_ROWS_SPEC, _gain_spec(layer, 0), _mod_spec(layer, 0), _mod_spec(layer, 1)],
        out_specs=_ROWS_SPEC,
        out_shape=jax.ShapeDtypeStruct((N_TOK, D_MODEL), BF16),
        compiler_params=_cparams(("parallel",)),
        name="norm_mod",
    )(x, gains, mods, mods)


def _resid_kernel(x_ref, y_ref, ga_ref, gate_ref, *rest, with_h):
    xn = x_ref[...] + gate_ref[...] * _rms(y_ref[...], ga_ref[...])
    if with_h:
        gb_ref, sh_ref, sc_ref, xo_ref, h_ref = rest
        h = _rms(xn, gb_ref[...]) * (1.0 + sc_ref[...]) + sh_ref[...]
        h_ref[...] = h.astype(h_ref.dtype)
    else:
        (xo_ref,) = rest
    xo_ref[...] = xn


def resid_norm_mod(x, y, gains, mods, *, layer, ga, gate, nxt):
    in_specs = [_ROWS_SPEC, _ROWS_SPEC, _gain_spec(layer, ga), _mod_spec(layer, gate)]
    args = [x, y, gains, mods]
    out_specs = [_ROWS_SPEC]
    out_shape = [jax.ShapeDtypeStruct((N_TOK, D_MODEL), F32)]
    if nxt is not None:
        nl, ng, nsh, nsc = nxt
        in_specs += [_gain_spec(nl, ng), _mod_spec(nl, nsh), _mod_spec(nl, nsc)]
        args += [gains, mods, mods]
        out_specs.append(_ROWS_SPEC)
        out_shape.append(jax.ShapeDtypeStruct((N_TOK, D_MODEL), BF16))
    res = pl.pallas_call(
        functools.partial(_resid_kernel, with_h=nxt is not None),
        grid=(N_TOK // ROW_TILE,),
        in_specs=in_specs,
        out_specs=out_specs,
        out_shape=out_shape,
        compiler_params=_cparams(("parallel",)),
        name="resid_norm_mod",
    )(*args)
    return (res[0], res[1]) if nxt is not None else (res[0], None)


def _head_rms(x, g):
    return x * lax.rsqrt(jnp.mean(x * x, axis=-1, keepdims=True) + EPS) * g


def _rope(x, cos, sin):
    lane = lax.broadcasted_iota(jnp.int32, x.shape, 1)
    partner = jnp.where((lane & 63) < 32, pltpu.roll(x, LANES - 32, 1), pltpu.roll(x, 32, 1))
    return x * cos + partner * sin


def _rope_tables():
    t = jnp.arange(DEC_SEQ)
    half = HEAD_DIM // 2
    inv_freq = 1.0 / (ROPE_THETA ** (jnp.arange(0, half, 2, dtype=F32) / half))
    ang_r = (t // GRID_W).astype(F32)[:, None] * inv_freq
    ang_c = (t % GRID_W).astype(F32)[:, None] * inv_freq
    cos = jnp.concatenate([jnp.cos(ang_r)] * 2 + [jnp.cos(ang_c)] * 2, axis=-1)
    sin = jnp.concatenate([-jnp.sin(ang_r), jnp.sin(ang_r), -jnp.sin(ang_c), jnp.sin(ang_c)], axis=-1)
    return cos, sin


def _softmax_pv(scores, values):
    m = functools.reduce(jnp.maximum, [jnp.max(s, axis=-1, keepdims=True) for s in scores])
    es = [jnp.exp(s - m) for s in scores]
    l = functools.reduce(jnp.add, [jnp.sum(e, axis=-1, keepdims=True) for e in es])
    o = functools.reduce(jnp.add, [_dot(e.astype(BF16), v) for e, v in zip(es, values)])
    return o / l


def _ctx_attn_kernel(q_ref, k_ref, v_ref, *rest, n_kv, group, normed):
    if normed:
        qg_ref, kg_ref, o_ref, ko_ref, vo_ref = rest
    else:
        o_ref, ko_ref, vo_ref = rest
    scale = HEAD_DIM ** -0.5
    for j in range(n_kv):
        ks = slice(j * HEAD_DIM, (j + 1) * HEAD_DIM)
        k = k_ref[:, ks]
        v = v_ref[:, ks]
        if normed:
            k = _head_rms(k, kg_ref[...])
        ko_ref[:, ks] = k
        vo_ref[:, ks] = v
        kb = k.astype(BF16)
        vb = v.astype(BF16)
        for g in range(group):
            qs = slice((j * group + g) * HEAD_DIM, (j * group + g + 1) * HEAD_DIM)
            q = q_ref[:, qs]
            if normed:
                q = _head_rms(q, qg_ref[...])
            s = _dot_nt(q.astype(BF16), kb) * scale
            o_ref[:, qs] = _softmax_pv([s], [vb]).astype(o_ref.dtype)


CTX_COLS = 1024


def na_context_attn(qkv):
    nb = D_MODEL // CTX_COLS
    blk = pl.BlockSpec((SEQ, CTX_COLS), lambda b, j: (b, j))
    kv_shape = jax.ShapeDtypeStruct((N_CTX, D_MODEL), F32)
    return pl.pallas_call(
        functools.partial(_ctx_attn_kernel, n_kv=CTX_COLS // HEAD_DIM, group=1, normed=False),
        grid=(BATCH, nb),
        in_specs=[blk,
                  pl.BlockSpec((SEQ, CTX_COLS), lambda b, j: (b, nb + j)),
                  pl.BlockSpec((SEQ, CTX_COLS), lambda b, j: (b, 2 * nb + j))],
        out_specs=[blk, blk, blk],
        out_shape=[jax.ShapeDtypeStruct((N_CTX, D_MODEL), BF16), kv_shape, kv_shape],
        compiler_params=_cparams(("parallel", "arbitrary")),
        name="na_ctx_attn",
    )(qkv, qkv, qkv)


def gqa_context_attn(qkv, q_gain, k_gain):
    n_kv = CTX_COLS // (GQA_GROUP * HEAD_DIM)
    kv_cols = n_kv * HEAD_DIM
    nb = D_MODEL // CTX_COLS
    k0 = D_MODEL // kv_cols
    v0 = k0 + GQA_KV_HEADS * HEAD_DIM // kv_cols
    kv_shape = jax.ShapeDtypeStruct((N_CTX, GQA_KV_HEADS * HEAD_DIM), F32)
    gain_spec = pl.BlockSpec((1, HEAD_DIM), lambda b, j: (0, 0))
    return pl.pallas_call(
        functools.partial(_ctx_attn_kernel, n_kv=n_kv, group=GQA_GROUP, normed=True),
        grid=(BATCH, nb),
        in_specs=[pl.BlockSpec((SEQ, CTX_COLS), lambda b, j: (b, j)),
                  pl.BlockSpec((SEQ, kv_cols), lambda b, j: (b, k0 + j)),
                  pl.BlockSpec((SEQ, kv_cols), lambda b, j: (b, v0 + j)),
                  gain_spec, gain_spec],
        out_specs=[pl.BlockSpec((SEQ, CTX_COLS), lambda b, j: (b, j)),
                   pl.BlockSpec((SEQ, kv_cols), lambda b, j: (b, j)),
                   pl.BlockSpec((SEQ, kv_cols), lambda b, j: (b, j))],
        out_shape=[jax.ShapeDtypeStruct((N_CTX, D_MODEL), BF16), kv_shape, kv_shape],
        compiler_params=_cparams(("parallel", "arbitrary")),
        name="gqa_ctx_attn",
    )(qkv, qkv, qkv, q_gain.reshape(1, HEAD_DIM), k_gain.reshape(1, HEAD_DIM))


def _gqa_lat_kernel(q_ref, k_ref, v_ref, kc_ref, vc_ref, cos_ref, sin_ref, qg_ref, kg_ref, o_ref):
    scale = HEAD_DIM ** -0.5
    cos, sin = cos_ref[...], sin_ref[...]
    kc = kc_ref[...].astype(BF16)
    vc = vc_ref[...].astype(BF16)
    k = _rope(_head_rms(k_ref[...], kg_ref[...]), cos, sin).astype(BF16)
    v = v_ref[...].astype(BF16)
    for g in range(GQA_GROUP):
        qs = slice(g * HEAD_DIM, (g + 1) * HEAD_DIM)
        q = _rope(_head_rms(q_ref[:, qs], qg_ref[...]), cos, sin).astype(BF16)
        s_ctx = _dot_nt(q, kc) * scale
        s_lat = _dot_nt(q, k) * scale
        o_ref[:, qs] = _softmax_pv([s_ctx, s_lat], [vc, v]).astype(o_ref.dtype)


def gqa_latent_attn(qkv, cache_k, cache_v, q_gain, k_gain):
    cos, sin = _rope_tables()
    qcols = GQA_GROUP * HEAD_DIM
    rb0 = N_CTX // DEC_SEQ
    k0 = D_MODEL // HEAD_DIM
    v0 = k0 + GQA_KV_HEADS
    cache_spec = pl.BlockSpec((None, PAST_LEN, HEAD_DIM), lambda b, j: (b, 0, j))
    table_spec = pl.BlockSpec((DEC_SEQ, HEAD_DIM), lambda b, j: (0, 0))
    gain_spec = pl.BlockSpec((1, HEAD_DIM), lambda b, j: (0, 0))
    return pl.pallas_call(
        _gqa_lat_kernel,
        grid=(DEC_BATCH, GQA_KV_HEADS),
        in_specs=[pl.BlockSpec((DEC_SEQ, qcols), lambda b, j: (rb0 + b, j)),
                  pl.BlockSpec((DEC_SEQ, HEAD_DIM), lambda b, j: (rb0 + b, k0 + j)),
                  pl.BlockSpec((DEC_SEQ, HEAD_DIM), lambda b, j: (rb0 + b, v0 + j)),
                  cache_spec, cache_spec, table_spec, table_spec, gain_spec, gain_spec],
        out_specs=pl.BlockSpec((DEC_SEQ, qcols), lambda b, j: (b, j)),
        out_shape=jax.ShapeDtypeStruct((N_LAT, D_MODEL), BF16),
        compiler_params=_cparams(("parallel", "arbitrary")),
        name="gqa_lat_attn",
    )(qkv, qkv, qkv, cache_k, cache_v, cos, sin, q_gain.reshape(1, HEAD_DIM), k_gain.reshape(1, HEAD_DIM))


NA_ROWS = DEC_SEQ // GRID_W
NA_WR = min(WIN_R, NA_ROWS)
NA_WKEYS = NA_WR * GRID_W


NA_REL_ROWS = 2 * WIN_R - 1
NA_ROW_PAIRS = NA_WR // 2


def _na_bias_table(rpb):
    col = jnp.arange(GRID_W)
    cs = jnp.clip(col - WIN_C // 2, 0, GRID_W - WIN_C)
    col_ok = (col[None, :] >= cs[:, None]) & (col[None, :] < cs[:, None] + WIN_C)
    dc = jnp.clip(col[None, :] - col[:, None] + WIN_C - 1, 0, 2 * WIN_C - 2)
    blocks = jnp.where(col_ok[None, None], rpb[:, :, dc], NEG_BIG).astype(F32)
    return jnp.concatenate([blocks[:, :-1], blocks[:, 1:]], axis=-1)


def _na_lat_kernel(q_ref, k_ref, v_ref, kc_ref, vc_ref, bias_ref, o_ref,
                   kb_ref, vb_ref, slat_ref, elat_ref, olat_ref):
    scale = HEAD_DIM ** -0.5
    kb_ref[...] = k_ref[...].astype(BF16)
    vb_ref[...] = v_ref[...].astype(BF16)

    def window(r):
        rs = min(max(r - NA_WR // 2, 0), NA_ROWS - NA_WR)
        return (slice(r * GRID_W, (r + 1) * GRID_W), slice(rs * GRID_W, rs * GRID_W + NA_WKEYS),
                rs - r + WIN_R - 1)

    for r in range(NA_ROWS):
        q_rows, k_rows, rel0 = window(r)
        bias = jnp.concatenate([bias_ref[rel0 + 2 * p] for p in range(NA_ROW_PAIRS)], axis=1)
        slat_ref[q_rows, :] = _dot_nt(q_ref[q_rows, :].astype(BF16), kb_ref[k_rows, :]) * scale + bias
    s_lat = slat_ref[...]
    s_ctx = _dot_nt(q_ref[...].astype(BF16), kc_ref[...].astype(BF16)) * scale
    m = jnp.maximum(jnp.max(s_lat, axis=-1, keepdims=True), jnp.max(s_ctx, axis=-1, keepdims=True))
    e_lat = jnp.exp(s_lat - m)
    e_ctx = jnp.exp(s_ctx - m)
    l = jnp.sum(e_lat, axis=-1, keepdims=True) + jnp.sum(e_ctx, axis=-1, keepdims=True)
    elat_ref[...] = e_lat.astype(BF16)
    o_ctx = _dot(e_ctx.astype(BF16), vc_ref[...].astype(BF16))
    for r in range(NA_ROWS):
        q_rows, k_rows, _ = window(r)
        olat_ref[q_rows, :] = _dot(elat_ref[q_rows, :], vb_ref[k_rows, :])
    o_ref[...] = ((olat_ref[...] + o_ctx) / l).astype(o_ref.dtype)


def na_latent_attn(qkv, cache_k, cache_v, rpb):
    bias = _na_bias_table(rpb)
    rb0 = N_CTX // DEC_SEQ
    cache_spec = pl.BlockSpec((None, PAST_LEN, HEAD_DIM), lambda h, b: (b, 0, h))
    return pl.pallas_call(
        _na_lat_kernel,
        grid=(NA_HEADS, DEC_BATCH),
        in_specs=[pl.BlockSpec((DEC_SEQ, HEAD_DIM), lambda h, b: (rb0 + b, h)),
                  pl.BlockSpec((DEC_SEQ, HEAD_DIM), lambda h, b: (rb0 + b, NA_HEADS + h)),
                  pl.BlockSpec((DEC_SEQ, HEAD_DIM), lambda h, b: (rb0 + b, 2 * NA_HEADS + h)),
                  cache_spec, cache_spec,
                  pl.BlockSpec((None, NA_REL_ROWS - 1, GRID_W, 2 * GRID_W), lambda h, b: (h, 0, 0, 0))],
        out_specs=pl.BlockSpec((DEC_SEQ, HEAD_DIM), lambda h, b: (b, h)),
        out_shape=jax.ShapeDtypeStruct((N_LAT, D_MODEL), BF16),
        scratch_shapes=[pltpu.VMEM((DEC_SEQ, HEAD_DIM), BF16), pltpu.VMEM((DEC_SEQ, HEAD_DIM), BF16),
                        pltpu.VMEM((DEC_SEQ, NA_WKEYS), F32), pltpu.VMEM((DEC_SEQ, NA_WKEYS), BF16),
                        pltpu.VMEM((DEC_SEQ, HEAD_DIM), F32)],
        compiler_params=_cparams(("parallel", "arbitrary")),
        name="na_lat_attn",
    )(qkv, qkv, qkv, cache_k, cache_v, bias)


LRU_ROWS = 2048
LRU_CTX_STEPS = N_CTX // LRU_ROWS
LRU_SEQ_PAD = 8
LRU_PITCH_PAD = 4
LRU_SCAN_ROWS = max(LRU_ROWS // SEQ * (SEQ + LRU_PITCH_PAD), LRU_ROWS // DEC_SEQ * (DEC_SEQ + LRU_PITCH_PAD))


def _lru_body(xr_ref, gb_ref, cw_ref, cb_ref, wg_ref, bg_ref, lam_ref, h0_ref, y_ref, fin_ref,
              af_ref, uf_ref, ab_ref, ub_ref, hf_ref, hb_ref, *, n_seq, seq_len):
    rows = n_seq * seq_len
    halves = [slice(c * LANES, (c + 1) * LANES) for c in range(LRU_BW // LANES)]
    pitch = seq_len + LRU_PITCH_PAD
    seqs = [(slice(s * seq_len, (s + 1) * seq_len), slice(s * pitch, s * pitch + seq_len))
            for s in range(n_seq)]
    x = xr_ref[...]
    t = lax.broadcasted_iota(jnp.int32, (rows, LRU_BW), 0) & (seq_len - 1)
    cw = cw_ref[...]
    xf = (jnp.where(t >= 2, pltpu.roll(x, 2, 0), 0.0) * cw[0:1]
          + jnp.where(t >= 1, pltpu.roll(x, 1, 0), 0.0) * cw[1:2]
          + x * cw[2:3]
          + jnp.where(t < seq_len - 1, pltpu.roll(x, rows - 1, 0), 0.0) * cw[3:4]) + cb_ref[...]
    xb = xf.astype(BF16)
    sp = _softplus(-lam_ref[...])
    for d, (a_ref, u_ref) in enumerate(((af_ref, uf_ref), (ab_ref, ub_ref))):
        bg = bg_ref[d]
        r_gate = _sigmoid(_dot(xb, wg_ref[d, 0].astype(BF16)) + bg[0:1])
        i_gate = _sigmoid(_dot(xb, wg_ref[d, 1].astype(BF16)) + bg[1:2])
        a = jnp.exp(-LRU_C * r_gate * sp[d:d + 1])
        u = jnp.sqrt(1.0 - a * a) * i_gate * xf
        for c, cols in enumerate(halves):
            for src, dst in seqs:
                a_ref[c, dst, :] = a[src, cols]
                u_ref[c, dst, :] = u[src, cols]

    def step(i, carry):
        rf = pl.ds(i, n_seq, stride=pitch)
        rb = pl.ds(seq_len - 1 - i, n_seq, stride=pitch)
        out = []
        for c in range(len(halves)):
            hf = af_ref[c, rf, :] * carry[2 * c] + uf_ref[c, rf, :]
            hf_ref[c, rf, :] = hf
            hb = ab_ref[c, rb, :] * carry[2 * c + 1] + ub_ref[c, rb, :]
            hb_ref[c, rb, :] = hb
            out += [hf, hb]
        return tuple(out)

    init = tuple(h0_ref[d, 0:n_seq, cols] for cols in halves for d in range(2))
    fin = lax.fori_loop(0, seq_len, step, init, unroll=4)
    fin_ref[...] = jnp.zeros(fin_ref.shape, F32)
    gate = jax.nn.gelu(gb_ref[...])
    for c, cols in enumerate(halves):
        fin_ref[0, 0:n_seq, cols] = fin[2 * c]
        fin_ref[1, 0:n_seq, cols] = fin[2 * c + 1]
        for src, dst in seqs:
            h = hf_ref[c, dst, :] + hb_ref[c, dst, :]
            y_ref[src, cols] = (h * gate[src, cols]).astype(y_ref.dtype)


def _lru_kernel(*refs):
    i = pl.program_id(0)

    @pl.when(i < LRU_CTX_STEPS)
    def _():
        _lru_body(*refs, n_seq=LRU_ROWS // SEQ, seq_len=SEQ)

    @pl.when(i >= LRU_CTX_STEPS)
    def _():
        _lru_body(*refs, n_seq=LRU_ROWS // DEC_SEQ, seq_len=DEC_SEQ)


def rglru_mix(proj, state, conv_w, conv_b, w_gates, b_gates, lam):
    assert LRU_ROWS // SEQ == LRU_SEQ_PAD and LRU_ROWS // DEC_SEQ == DEC_BATCH
    n_steps = N_TOK // LRU_ROWS
    h0 = jnp.zeros((2, n_steps * LRU_SEQ_PAD, D_RNN), F32)
    h0 = h0.at[:, BATCH:BATCH + DEC_BATCH].set(state.transpose(1, 0, 2))
    blk = lambda i, n: (i, n)
    chan = lambda i, n: (0, n)
    y, fin = pl.pallas_call(
        _lru_kernel,
        grid=(n_steps, LRU_BLOCKS),
        in_specs=[pl.BlockSpec((LRU_ROWS, LRU_BW), blk),
                  pl.BlockSpec((LRU_ROWS, LRU_BW), lambda i, n: (i, LRU_BLOCKS + n)),
                  pl.BlockSpec((CONV_W, LRU_BW), chan),
                  pl.BlockSpec((1, LRU_BW), chan),
                  pl.BlockSpec((2, 2, None, LRU_BW, LRU_BW), lambda i, n: (0, 0, n, 0, 0)),
                  pl.BlockSpec((2, 2, LRU_BW), lambda i, n: (0, 0, n)),
                  pl.BlockSpec((2, LRU_BW), chan),
                  pl.BlockSpec((2, LRU_SEQ_PAD, LRU_BW), lambda i, n: (0, i, n))],
        out_specs=[pl.BlockSpec((LRU_ROWS, LRU_BW), blk),
                   pl.BlockSpec((2, LRU_SEQ_PAD, LRU_BW), lambda i, n: (0, i, n))],
        out_shape=[jax.ShapeDtypeStruct((N_TOK, D_RNN), BF16),
                   jax.ShapeDtypeStruct((2, n_steps * LRU_SEQ_PAD, D_RNN), F32)],
        scratch_shapes=[pltpu.VMEM((LRU_BW // LANES, LRU_SCAN_ROWS, LANES), F32)] * 6,
        compiler_params=_cparams(("parallel", "arbitrary")),
        name="rglru_mix",
    )(proj, proj, conv_w, conv_b.reshape(1, D_RNN), w_gates, b_gates, lam, h0)
    return y, fin[:, :BATCH].transpose(1, 0, 2)


ML_GATE_COLS = 4 * ML_HEADS


def _pick_lane(x, idx):
    lane = lax.broadcasted_iota(jnp.int32, x.shape, 1)
    return jnp.sum(jnp.where(lane == idx, x, 0.0), axis=1, keepdims=True)


def _pick_row(x, idx):
    row = lax.broadcasted_iota(jnp.int32, x.shape, 0)
    return jnp.sum(jnp.where(row == idx, x, 0.0), axis=0, keepdims=True)


def _mlstm_kernel(q_ref, k_ref, v_ref, og_ref, g_ref, gt_ref, bg_ref, bgt_ref, ng_ref, *rest,
                  seq_len, zero_init, with_state):
    rest = list(rest)
    if not zero_init:
        c0_ref, n0_ref, m0_ref = rest[:3]
        rest = rest[3:]
    y_ref = rest.pop(0)
    if with_state:
        co_ref, no_ref, mo_ref = rest[:3]
        rest = rest[3:]
    h_ref, c_ref, n_ref = rest
    head = pl.program_id(1)
    n_chunks = seq_len // ML_CHUNK
    sub = lax.broadcasted_iota(jnp.int32, (ML_CHUNK, ML_CHUNK), 0)
    lane = lax.broadcasted_iota(jnp.int32, (ML_CHUNK, ML_CHUNK), 1)

    if zero_init:
        c_ref[...] = jnp.zeros(c_ref.shape, F32)
        n_ref[...] = jnp.zeros(n_ref.shape, F32)
        m_init = (jnp.zeros((1, 1), F32),) * 2
    else:
        c_ref[...] = c0_ref[...]
        n_ref[...] = n0_ref[...]
        m_init = (m0_ref[0], m0_ref[1])

    def one_chunk(d, c_idx, m_prev):
        valid = (lane <= sub) if d == 0 else (lane >= sub)
        valid_t = (sub <= lane) if d == 0 else (sub >= lane)
        i_col = d * 2 * ML_HEADS + head
        f_col = i_col + ML_HEADS
        rows = pl.ds(pl.multiple_of(c_idx * ML_CHUNK, ML_CHUNK), ML_CHUNK)
        g = g_ref[rows, :] + bg_ref[...]
        gt = gt_ref[:, rows] + bgt_ref[...]
        li_col = _pick_lane(g, i_col)
        lf_col = -_softplus(-_pick_lane(g, f_col))
        li_row = _pick_row(gt, i_col)
        lf_row = -_softplus(-_pick_row(gt, f_col))
        bcum_col = jnp.sum(jnp.where(valid, lf_row, 0.0), axis=1, keepdims=True)
        bcum_row = jnp.sum(jnp.where(valid_t, lf_col, 0.0), axis=0, keepdims=True)
        b_last = jnp.sum(lf_row, axis=1, keepdims=True)
        dmat = jnp.where(valid, bcum_col - bcum_row + li_row, NEG_BIG)
        m_inter = bcum_col + m_prev
        m_t = jnp.maximum(m_inter, jnp.max(dmat, axis=1, keepdims=True))
        qf = q_ref[rows, :]
        qb = qf.astype(BF16)
        kf = k_ref[rows, :] * (ML_DK ** -0.5)
        vb = v_ref[rows, :].astype(BF16)
        s = _dot_nt(qb, kf.astype(BF16)) * jnp.exp(dmat - m_t)
        inter = jnp.exp(m_inter - m_t)
        num = _dot(s.astype(BF16), vb) + inter * _dot(qb, c_ref[d].astype(BF16))
        den = (jnp.sum(s, axis=1, keepdims=True)
               + inter * jnp.sum(qf * n_ref[d], axis=1, keepdims=True))
        h_ref[d, rows, :] = num / jnp.maximum(jnp.abs(den), jnp.exp(-m_t))
        dec = b_last - bcum_col + li_col
        m_new = jnp.maximum(b_last + m_prev, jnp.max(dec, axis=0, keepdims=True))
        kw = kf * jnp.exp(dec - m_new)
        carry_scale = jnp.exp(b_last + m_prev - m_new)
        c_ref[d] = carry_scale * c_ref[d] + _dot_tn(kw.astype(BF16), vb)
        n_ref[d] = carry_scale * n_ref[d] + jnp.sum(kw, axis=0, keepdims=True)
        return m_new

    def both(ci, m_prev):
        return one_chunk(0, ci, m_prev[0]), one_chunk(1, n_chunks - 1 - ci, m_prev[1])

    m_fin = lax.fori_loop(0, n_chunks, both, m_init)
    if with_state:
        co_ref[...] = c_ref[...]
        no_ref[...] = n_ref[...]
        mo_ref[0] = m_fin[0]
        mo_ref[1] = m_fin[1]

    hs = h_ref[0] + h_ref[1]
    hn = hs * lax.rsqrt(jnp.mean(hs * hs, axis=-1, keepdims=True) + EPS) * ng_ref[...]
    y_ref[...] = (jax.nn.sigmoid(og_ref[...]) * hn).astype(y_ref.dtype)


def mlstm_mix(proj, gates, gates_t, b_gates, norm_g, *, row0, n_seq, seq_len, state=None, with_state):
    rb0 = row0 // seq_len
    kq = ML_HEADS * ML_DK // ML_DK
    v0 = 2 * ML_HEADS * ML_DK // ML_DV
    o0 = v0 + ML_HEADS
    zero_init = state is None
    bias = jnp.pad(b_gates.reshape(1, ML_GATE_COLS), ((0, 0), (0, LANES - ML_GATE_COLS)))
    in_specs = [pl.BlockSpec((seq_len, ML_DK), lambda b, h: (rb0 + b, h)),
                pl.BlockSpec((seq_len, ML_DK), lambda b, h: (rb0 + b, kq + h)),
                pl.BlockSpec((seq_len, ML_DV), lambda b, h: (rb0 + b, v0 + h)),
                pl.BlockSpec((seq_len, ML_DV), lambda b, h: (rb0 + b, o0 + h)),
                pl.BlockSpec((seq_len, LANES), lambda b, h: (rb0 + b, 0)),
                pl.BlockSpec((ML_GATE_COLS, seq_len), lambda b, h: (0, rb0 + b)),
                pl.BlockSpec((1, LANES), lambda b, h: (0, 0)),
                pl.BlockSpec((ML_GATE_COLS, 1), lambda b, h: (0, 0)),
                pl.BlockSpec((1, ML_DV), lambda b, h: (0, h))]
    args = [proj, proj, proj, proj, gates, gates_t, bias, b_gates.reshape(ML_GATE_COLS, 1),
            norm_g.reshape(1, ML_HEADS * ML_DV)]
    c_spec = pl.BlockSpec((None, 2, None, ML_DK, ML_DV), lambda b, h: (b, 0, h, 0, 0))
    n_spec = pl.BlockSpec((None, 2, None, 1, ML_DK), lambda b, h: (b, 0, h, 0, 0))
    m_spec = pl.BlockSpec((None, 2, None, 1, 1), lambda b, h: (b, 0, h, 0, 0))
    if not zero_init:
        c0, n0, m0 = state
        in_specs += [c_spec, n_spec, m_spec]
        args += [c0, n0.reshape(n_seq, 2, ML_HEADS, 1, ML_DK), m0.reshape(n_seq, 2, ML_HEADS, 1, 1)]
    out_specs = [pl.BlockSpec((seq_len, ML_DV), lambda b, h: (b, h))]
    out_shape = [jax.ShapeDtypeStruct((n_seq * seq_len, ML_HEADS * ML_DV), BF16)]
    if with_state:
        out_specs += [c_spec, n_spec, m_spec]
        out_shape += [jax.ShapeDtypeStruct((n_seq, 2, ML_HEADS, ML_DK, ML_DV), F32),
                      jax.ShapeDtypeStruct((n_seq, 2, ML_HEADS, 1, ML_DK), F32),
                      jax.ShapeDtypeStruct((n_seq, 2, ML_HEADS, 1, 1), F32)]
    res = pl.pallas_call(
        functools.partial(_mlstm_kernel, seq_len=seq_len, zero_init=zero_init, with_state=with_state),
        grid=(n_seq, ML_HEADS),
        in_specs=in_specs,
        out_specs=out_specs,
        out_shape=out_shape,
        scratch_shapes=[pltpu.VMEM((2, seq_len, ML_DV), F32), pltpu.VMEM((2, ML_DK, ML_DV), F32),
                        pltpu.VMEM((2, 1, ML_DK), F32)],
        compiler_params=_cparams(("parallel", "arbitrary")),
        name="mlstm_mix",
    )(*args)
    if not with_state:
        return res[0]
    y, c_fin, n_fin, m_fin = res
    return y, c_fin, n_fin.reshape(n_seq, 2, ML_HEADS, ML_DK), m_fin.reshape(n_seq, 2, ML_HEADS)


def kernel(x_prompt, x_sample, cache_na_k, cache_na_v, state_lru, state_mlstm_C, state_mlstm_n, state_mlstm_m, cache_gqa_k, cache_gqa_v, c, c_ctx, ada_w, ada_b, norm_g, mlp_up, mlp_down, na_w_qkv, na_rpb, na_w_o, lru_w_in, lru_conv_w, lru_conv_b, lru_w_gates, lru_b_gates, lru_lambda, lru_w_o, ml_w_in, ml_w_gates, ml_b_gates, ml_norm_g, ml_w_o, gqa_w_qkv, gqa_q_norm, gqa_k_norm, gqa_w_o):
    x = jnp.concatenate([x_prompt.reshape(N_CTX, D_MODEL), x_sample.reshape(N_LAT, D_MODEL)], axis=0)
    cond = jnp.concatenate([c_ctx[None], c, jnp.zeros((MOD_ROWS - 1 - DEC_BATCH, D_MODEL), F32)], axis=0)
    mods = adaln_all(cond, ada_w, ada_b)
    gains = norm_g.reshape(DEPTH, 4, 1, D_MODEL)

    outs = {}
    h = norm_mod(x, gains, mods, 0)
    for i in range(DEPTH):
        kind = i % 4
        if kind == 0:
            qkv = matmul(h, na_w_qkv[0], name="na_qkv")
            shp = (BATCH, 1, SEQ, NA_HEADS, HEAD_DIM)
            yp, kp, vp = na_context_attn(qkv)
            outs['na_k'], outs['na_v'] = kp.reshape(shp), vp.reshape(shp)
            yl = na_latent_attn(qkv, cache_na_k[:, 0].reshape(DEC_BATCH, PAST_LEN, D_MODEL),
                                cache_na_v[:, 0].reshape(DEC_BATCH, PAST_LEN, D_MODEL), na_rpb[0])
            y = matmul(yp, na_w_o[0], a2=yl, name="na_out")
        elif kind == 1:
            proj = matmul(h, lru_w_in[0], name="lru_in")
            pre, fin = rglru_mix(proj, state_lru[:, 0], lru_conv_w[0], lru_conv_b[0], lru_w_gates[0],
                                 lru_b_gates[0], lru_lambda[0])
            outs['lru'] = fin[:, None]
            y = matmul(pre, lru_w_o[0], name="lru_out")
        elif kind == 2:
            proj = matmul(h, ml_w_in[0], name="ml_in")
            wg = jnp.pad(ml_w_gates[0], ((0, 0), (0, LANES - ML_GATE_COLS)))
            gates = matmul(h, wg, tn=LANES, name="ml_gates")
            gates_t = gates[:, :ML_GATE_COLS].T
            yp, cp, np_, mp = mlstm_mix(proj, gates, gates_t, ml_b_gates[0], ml_norm_g[0], row0=0,
                                        n_seq=BATCH, seq_len=SEQ, with_state=True)
            yl = mlstm_mix(proj, gates, gates_t, ml_b_gates[0], ml_norm_g[0], row0=N_CTX,
                           n_seq=DEC_BATCH, seq_len=DEC_SEQ, with_state=False,
                           state=(state_mlstm_C[:, 0], state_mlstm_n[:, 0], state_mlstm_m[:, 0]))
            outs['mc'], outs['mn'], outs['mm'] = cp[:, None], np_[:, None], mp[:, None]
            y = matmul(yp, ml_w_o[0], a2=yl, name="ml_out")
        else:
            qkv = matmul(h, gqa_w_qkv[0], name="gqa_qkv")
            kv_cols = GQA_KV_HEADS * HEAD_DIM
            yp, kp, vp = gqa_context_attn(qkv, gqa_q_norm[0], gqa_k_norm[0])
            yl = gqa_latent_attn(qkv, cache_gqa_k[:, 0].reshape(DEC_BATCH, PAST_LEN, kv_cols),
                                 cache_gqa_v[:, 0].reshape(DEC_BATCH, PAST_LEN, kv_cols),
                                 gqa_q_norm[0], gqa_k_norm[0])
            shp = (BATCH, 1, SEQ, GQA_KV_HEADS, HEAD_DIM)
            outs['gk'], outs['gv'] = kp.reshape(shp), vp.reshape(shp)
            y = matmul(yp, gqa_w_o[0], a2=yl, name="gqa_out")

        x, h2 = resid_norm_mod(x, y, gains, mods, layer=i, ga=1, gate=2, nxt=(i, 2, 3, 4))
        u = matmul(h2, mlp_up[i], out_dtype=BF16, act="relu2", name="mlp_up")
        z = matmul(u, mlp_down[i], name="mlp_down")
        nxt = (i + 1, 0, 0, 1) if i + 1 < DEPTH else None
        x, h = resid_norm_mod(x, z, gains, mods, layer=i, ga=3, gate=5, nxt=nxt)

    return (x[:N_CTX].reshape(BATCH, SEQ, D_MODEL), x[N_CTX:].reshape(DEC_BATCH, DEC_SEQ, D_MODEL),
            outs['na_k'], outs['na_v'], outs['lru'], outs['mc'], outs['mn'], outs['mm'], outs['gk'], outs['gv'])
```

```python
import functools

import jax
import jax.numpy as jnp
from jax import lax
from jax.experimental import pallas as pl
from jax.experimental.pallas import tpu as pltpu

D_MODEL = 4096
BATCH = 16
SEQ = 256
DEPTH = 4
DEC_BATCH = 2
DEC_SEQ = 1024
PAST_LEN = 512
GRID_W = 64
HEAD_DIM = 128
NA_HEADS = 32
WIN_R = 8
WIN_C = 16
GQA_HEADS = 32
GQA_KV_HEADS = 8
GQA_GROUP = GQA_HEADS // GQA_KV_HEADS
ROPE_THETA = 10000.0
D_RNN = D_MODEL
LRU_BLOCKS = 16
LRU_BW = 256
CONV_W = 4
LRU_C = 8.0
ML_HEADS = 8
ML_DK = 256
ML_DV = 512
ML_CHUNK = 128
D_FF = 4 * D_MODEL
EPS = 1e-6

N_CTX = BATCH * SEQ
N_LAT = DEC_BATCH * DEC_SEQ
N_TOK = N_CTX + N_LAT
MOD_ROWS = 8
LANES = 128
NEG_BIG = -1e30

VMEM_LIMIT = 56 * 1024 * 1024

BF16 = jnp.bfloat16
F32 = jnp.float32


def _cparams(sem):
    return pltpu.CompilerParams(dimension_semantics=sem, vmem_limit_bytes=VMEM_LIMIT)


def _dot(a, b):
    return jnp.dot(a, b, preferred_element_type=F32)


def _dot_nt(a, b):
    return lax.dot_general(a, b, (((1,), (1,)), ((), ())), preferred_element_type=F32)


def _dot_tn(a, b):
    return lax.dot_general(a, b, (((0,), (0,)), ((), ())), preferred_element_type=F32)


def _sigmoid(x):
    return 0.5 * (jnp.tanh(0.5 * x) + 1.0)


def _softplus(x):
    return jnp.maximum(x, 0.0) + jnp.log1p(jnp.exp(-jnp.abs(x)))


def _mm_kernel(*refs, act, n_first):
    w_ref, o_ref = refs[-2:]

    def tile(a_ref):
        acc = _dot(a_ref[...], w_ref[...].astype(BF16))
        if act == "relu2":
            r = jnp.maximum(acc, 0.0)
            acc = r * r
        o_ref[...] = acc.astype(o_ref.dtype)

    if len(refs) == 3:
        tile(refs[0])
    else:
        i = pl.program_id(0)
        pl.when(i < n_first)(lambda: tile(refs[0]))
        pl.when(i >= n_first)(lambda: tile(refs[1]))


def _mm_kernel_kgrid(a_ref, w_ref, o_ref, acc_ref, *, nk):
    k = pl.program_id(2)
    part = _dot(a_ref[...], w_ref[...].astype(BF16))

    @pl.when(k == 0)
    def _():
        acc_ref[...] = part

    @pl.when(k > 0)
    def _():
        acc_ref[...] += part

    @pl.when(k == nk - 1)
    def _():
        o_ref[...] = acc_ref[...].astype(o_ref.dtype)


def matmul(a, w, *, layer=0, a2=None, out_dtype=F32, act=None, tm=1024, tn=512, tk=4096, name="matmul"):
    m1, k = a.shape
    m = m1 + (0 if a2 is None else a2.shape[0])
    k2, n = w.shape[-2:]
    assert k == k2 and a.dtype == BF16
    lead = () if w.ndim == 2 else (None,)
    at = () if w.ndim == 2 else (layer,)
    tm, tn, tk = min(tm, m), min(tn, n), min(tk, k)
    assert m1 % tm == 0 and m % tm == 0 and n % tn == 0 and k % tk == 0
    nk = k // tk
    n_first = m1 // tm
    if nk == 1:
        a_specs = [pl.BlockSpec((tm, k), lambda i, j: (jnp.minimum(i, n_first - 1), 0))]
        a_args = [a]
        if a2 is not None:
            assert a2.dtype == BF16 and a2.shape[1] == k
            a_specs.append(pl.BlockSpec((tm, k), lambda i, j: (jnp.maximum(i - n_first, 0), 0)))
            a_args.append(a2)
        return pl.pallas_call(
            functools.partial(_mm_kernel, act=act, n_first=n_first),
            grid=(m // tm, n // tn),
            in_specs=a_specs + [pl.BlockSpec(lead + (k, tn), lambda i, j: at + (0, j))],
            out_specs=pl.BlockSpec((tm, tn), lambda i, j: (i, j)),
            out_shape=jax.ShapeDtypeStruct((m, n), out_dtype),
            compiler_params=_cparams(("parallel", "arbitrary")),
            name=name,
        )(*a_args, w)
    assert act is None and a2 is None
    return pl.pallas_call(
        functools.partial(_mm_kernel_kgrid, nk=nk),
        grid=(m // tm, n // tn, nk),
        in_specs=[pl.BlockSpec((tm, tk), lambda i, j, kk: (i, kk)),
                  pl.BlockSpec(lead + (tk, tn), lambda i, j, kk: at + (kk, j))],
        out_specs=pl.BlockSpec((tm, tn), lambda i, j, kk: (i, j)),
        out_shape=jax.ShapeDtypeStruct((m, n), out_dtype),
        scratch_shapes=[pltpu.VMEM((tm, tn), F32)],
        compiler_params=_cparams(("parallel", "arbitrary", "arbitrary")),
        name=name,
    )(a, w)


def _ada_kernel(c_ref, w_ref, b_ref, o_ref):
    c = c_ref[...]
    a = (c * jax.nn.sigmoid(c)).astype(BF16)
    o_ref[...] = _dot(a, w_ref[...].astype(BF16)) + b_ref[...]


def adaln_all(cond, ada_w, ada_b, *, tn=512):
    n = 6 * D_MODEL
    out = pl.pallas_call(
        _ada_kernel,
        grid=(DEPTH, n // tn),
        in_specs=[pl.BlockSpec((MOD_ROWS, D_MODEL), lambda l, j: (0, 0)),
                  pl.BlockSpec((None, D_MODEL, tn), lambda l, j: (l, 0, j)),
                  pl.BlockSpec((None, 1, tn), lambda l, j: (l, 0, j))],
        out_specs=pl.BlockSpec((None, MOD_ROWS, tn), lambda l, j: (l, 0, j)),
        out_shape=jax.ShapeDtypeStruct((DEPTH, MOD_ROWS, n), F32),
        compiler_params=_cparams(("parallel", "arbitrary")),
        name="adaln",
    )(cond, ada_w, ada_b.reshape(DEPTH, 1, n))
    return out.reshape(DEPTH, MOD_ROWS, 6, 1, D_MODEL)


ROW_TILE = 256


def _row_group(i):
    return jnp.maximum(0, (i * ROW_TILE - N_CTX) // DEC_SEQ + 1)


def _rms(x, g):
    return x * lax.rsqrt(jnp.mean(x * x, axis=-1, keepdims=True) + EPS) * g


def _mod_spec(layer, which):
    return pl.BlockSpec((None, None, None, 1, D_MODEL),
                        lambda i: (layer, _row_group(i), which, 0, 0))


def _gain_spec(layer, which):
    return pl.BlockSpec((None, None, 1, D_MODEL), lambda i: (layer, which, 0, 0))


_ROWS_SPEC = pl.BlockSpec((ROW_TILE, D_MODEL), lambda i: (i, 0))


CTX_TILES = N_CTX // ROW_TILE
_CTX_ROWS_SPEC = pl.BlockSpec((ROW_TILE, D_MODEL), lambda i: (jnp.minimum(i, CTX_TILES - 1), 0))
_LAT_ROWS_SPEC = pl.BlockSpec((ROW_TILE, D_MODEL), lambda i: (jnp.maximum(i - CTX_TILES, 0), 0))


def _x_specs(x):
    return [_CTX_ROWS_SPEC, _LAT_ROWS_SPEC] if isinstance(x, tuple) else [_ROWS_SPEC]


def _x_args(x):
    return list(x) if isinstance(x, tuple) else [x]


def _load_rows(refs):
    if len(refs) == 1:
        return refs[0][...]
    return jnp.where(pl.program_id(0) < CTX_TILES, refs[0][...], refs[1][...])


def _norm_mod_kernel(*refs):
    g_ref, sh_ref, sc_ref, h_ref = refs[-4:]
    h = _rms(_load_rows(refs[:-4]), g_ref[...]) * (1.0 + sc_ref[...]) + sh_ref[...]
    h_ref[...] = h.astype(h_ref.dtype)


def norm_mod(x, gains, mods, layer):
    return pl.pallas_call(
        _norm_mod_kernel,
        grid=(N_TOK // ROW_TILE,),
        in_specs=_x_specs(x) + [_gain_spec(layer, 0), _mod_spec(layer, 0), _mod_spec(layer, 1)],
        out_specs=_ROWS_SPEC,
        out_shape=jax.ShapeDtypeStruct((N_TOK, D_MODEL), BF16),
        compiler_params=_cparams(("parallel",)),
        name="norm_mod",
    )(*_x_args(x), gains, mods, mods)


def _resid_kernel(*refs, n_x, with_h, split_out):
    x_refs, (y_ref, ga_ref, gate_ref), rest = refs[:n_x], refs[n_x:n_x + 3], refs[n_x + 3:]
    xn = _load_rows(x_refs) + gate_ref[...] * _rms(y_ref[...], ga_ref[...])
    if with_h:
        gb_ref, sh_ref, sc_ref = rest[:3]
        rest = rest[3:]
        h_ref = rest[-1]
        h = _rms(xn, gb_ref[...]) * (1.0 + sc_ref[...]) + sh_ref[...]
        h_ref[...] = h.astype(h_ref.dtype)
    if split_out:
        i = pl.program_id(0)

        @pl.when(i < CTX_TILES)
        def _():
            rest[0][...] = xn

        @pl.when(i >= CTX_TILES)
        def _():
            rest[1][...] = xn
    else:
        rest[0][...] = xn


def resid_norm_mod(x, y, gains, mods, *, layer, ga, gate, nxt, split_out=False):
    in_specs = _x_specs(x) + [_ROWS_SPEC, _gain_spec(layer, ga), _mod_spec(layer, gate)]
    args = _x_args(x) + [y, gains, mods]
    if split_out:
        out_specs = [_CTX_ROWS_SPEC, _LAT_ROWS_SPEC]
        out_shape = [jax.ShapeDtypeStruct((N_CTX, D_MODEL), F32), jax.ShapeDtypeStruct((N_LAT, D_MODEL), F32)]
    else:
        out_specs = [_ROWS_SPEC]
        out_shape = [jax.ShapeDtypeStruct((N_TOK, D_MODEL), F32)]
    if nxt is not None:
        nl, ng, nsh, nsc = nxt
        in_specs += [_gain_spec(nl, ng), _mod_spec(nl, nsh), _mod_spec(nl, nsc)]
        args += [gains, mods, mods]
        out_specs.append(_ROWS_SPEC)
        out_shape.append(jax.ShapeDtypeStruct((N_TOK, D_MODEL), BF16))
    res = pl.pallas_call(
        functools.partial(_resid_kernel, n_x=len(_x_args(x)), with_h=nxt is not None, split_out=split_out),
        grid=(N_TOK // ROW_TILE,),
        in_specs=in_specs,
        out_specs=out_specs,
        out_shape=out_shape,
        compiler_params=_cparams(("arbitrary",) if split_out else ("parallel",)),
        name="resid_norm_mod",
    )(*args)
    x_new = (res[0], res[1]) if split_out else res[0]
    return x_new, (res[-1] if nxt is not None else None)


def _head_rms(x, g):
    return x * lax.rsqrt(jnp.mean(x * x, axis=-1, keepdims=True) + EPS) * g


def _rope(x, cos, sin):
    lane = lax.broadcasted_iota(jnp.int32, x.shape, 1)
    partner = jnp.where((lane & 63) < 32, pltpu.roll(x, LANES - 32, 1), pltpu.roll(x, 32, 1))
    return x * cos + partner * sin


def _rope_tables():
    t = jnp.arange(DEC_SEQ)
    half = HEAD_DIM // 2
    inv_freq = 1.0 / (ROPE_THETA ** (jnp.arange(0, half, 2, dtype=F32) / half))
    ang_r = (t // GRID_W).astype(F32)[:, None] * inv_freq
    ang_c = (t % GRID_W).astype(F32)[:, None] * inv_freq
    cos = jnp.concatenate([jnp.cos(ang_r)] * 2 + [jnp.cos(ang_c)] * 2, axis=-1)
    sin = jnp.concatenate([-jnp.sin(ang_r), jnp.sin(ang_r), -jnp.sin(ang_c), jnp.sin(ang_c)], axis=-1)
    return cos, sin


def _softmax_pv(scores, values):
    m = functools.reduce(jnp.maximum, [jnp.max(s, axis=-1, keepdims=True) for s in scores])
    es = [jnp.exp(s - m) for s in scores]
    l = functools.reduce(jnp.add, [jnp.sum(e, axis=-1, keepdims=True) for e in es])
    o = functools.reduce(jnp.add, [_dot(e.astype(BF16), v) for e, v in zip(es, values)])
    return o / l


def _ctx_attn_kernel(q_ref, k_ref, v_ref, *rest, n_kv, group, normed):
    if normed:
        qg_ref, kg_ref, o_ref, ko_ref, vo_ref = rest
    else:
        o_ref, ko_ref, vo_ref = rest
    scale = HEAD_DIM ** -0.5
    for j in range(n_kv):
        ks = slice(j * HEAD_DIM, (j + 1) * HEAD_DIM)
        k = k_ref[:, ks]
        v = v_ref[:, ks]
        if normed:
            k = _head_rms(k, kg_ref[...])
        ko_ref[:, ks] = k
        vo_ref[:, ks] = v
        kb = k.astype(BF16)
        vb = v.astype(BF16)
        for g in range(group):
            qs = slice((j * group + g) * HEAD_DIM, (j * group + g + 1) * HEAD_DIM)
            q = q_ref[:, qs]
            if normed:
                q = _head_rms(q, qg_ref[...])
            s = _dot_nt(q.astype(BF16), kb) * scale
            o_ref[:, qs] = _softmax_pv([s], [vb]).astype(o_ref.dtype)


CTX_COLS = 1024


def na_context_attn(qkv):
    nb = D_MODEL // CTX_COLS
    blk = pl.BlockSpec((SEQ, CTX_COLS), lambda b, j: (b, j))
    kv_shape = jax.ShapeDtypeStruct((N_CTX, D_MODEL), F32)
    return pl.pallas_call(
        functools.partial(_ctx_attn_kernel, n_kv=CTX_COLS // HEAD_DIM, group=1, normed=False),
        grid=(BATCH, nb),
        in_specs=[blk,
                  pl.BlockSpec((SEQ, CTX_COLS), lambda b, j: (b, nb + j)),
                  pl.BlockSpec((SEQ, CTX_COLS), lambda b, j: (b, 2 * nb + j))],
        out_specs=[blk, blk, blk],
        out_shape=[jax.ShapeDtypeStruct((N_CTX, D_MODEL), BF16), kv_shape, kv_shape],
        compiler_params=_cparams(("parallel", "arbitrary")),
        name="na_ctx_attn",
    )(qkv, qkv, qkv)


def gqa_context_attn(qkv, q_gain, k_gain):
    n_kv = CTX_COLS // (GQA_GROUP * HEAD_DIM)
    kv_cols = n_kv * HEAD_DIM
    nb = D_MODEL // CTX_COLS
    k0 = D_MODEL // kv_cols
    v0 = k0 + GQA_KV_HEADS * HEAD_DIM // kv_cols
    kv_shape = jax.ShapeDtypeStruct((N_CTX, GQA_KV_HEADS * HEAD_DIM), F32)
    gain_spec = pl.BlockSpec((1, HEAD_DIM), lambda b, j: (0, 0))
    return pl.pallas_call(
        functools.partial(_ctx_attn_kernel, n_kv=n_kv, group=GQA_GROUP, normed=True),
        grid=(BATCH, nb),
        in_specs=[pl.BlockSpec((SEQ, CTX_COLS), lambda b, j: (b, j)),
                  pl.BlockSpec((SEQ, kv_cols), lambda b, j: (b, k0 + j)),
                  pl.BlockSpec((SEQ, kv_cols), lambda b, j: (b, v0 + j)),
                  gain_spec, gain_spec],
        out_specs=[pl.BlockSpec((SEQ, CTX_COLS), lambda b, j: (b, j)),
                   pl.BlockSpec((SEQ, kv_cols), lambda b, j: (b, j)),
                   pl.BlockSpec((SEQ, kv_cols), lambda b, j: (b, j))],
        out_shape=[jax.ShapeDtypeStruct((N_CTX, D_MODEL), BF16), kv_shape, kv_shape],
        compiler_params=_cparams(("parallel", "arbitrary")),
        name="gqa_ctx_attn",
    )(qkv, qkv, qkv, q_gain.reshape(1, HEAD_DIM), k_gain.reshape(1, HEAD_DIM))


def _gqa_lat_kernel(q_ref, k_ref, v_ref, kc_ref, vc_ref, cos_ref, sin_ref, qg_ref, kg_ref, o_ref):
    scale = HEAD_DIM ** -0.5
    cos, sin = cos_ref[...], sin_ref[...]
    kc = kc_ref[...].astype(BF16)
    vc = vc_ref[...].astype(BF16)
    k = _rope(_head_rms(k_ref[...], kg_ref[...]), cos, sin).astype(BF16)
    v = v_ref[...].astype(BF16)
    for g in range(GQA_GROUP):
        qs = slice(g * HEAD_DIM, (g + 1) * HEAD_DIM)
        q = _rope(_head_rms(q_ref[:, qs], qg_ref[...]), cos, sin).astype(BF16)
        s_ctx = _dot_nt(q, kc) * scale
        s_lat = _dot_nt(q, k) * scale
        o_ref[:, qs] = _softmax_pv([s_ctx, s_lat], [vc, v]).astype(o_ref.dtype)


def gqa_latent_attn(qkv, cache_k, cache_v, q_gain, k_gain):
    cos, sin = _rope_tables()
    qcols = GQA_GROUP * HEAD_DIM
    rb0 = N_CTX // DEC_SEQ
    k0 = D_MODEL // HEAD_DIM
    v0 = k0 + GQA_KV_HEADS
    cache_spec = pl.BlockSpec((None, PAST_LEN, HEAD_DIM), lambda b, j: (b, 0, j))
    table_spec = pl.BlockSpec((DEC_SEQ, HEAD_DIM), lambda b, j: (0, 0))
    gain_spec = pl.BlockSpec((1, HEAD_DIM), lambda b, j: (0, 0))
    return pl.pallas_call(
        _gqa_lat_kernel,
        grid=(DEC_BATCH, GQA_KV_HEADS),
        in_specs=[pl.BlockSpec((DEC_SEQ, qcols), lambda b, j: (rb0 + b, j)),
                  pl.BlockSpec((DEC_SEQ, HEAD_DIM), lambda b, j: (rb0 + b, k0 + j)),
                  pl.BlockSpec((DEC_SEQ, HEAD_DIM), lambda b, j: (rb0 + b, v0 + j)),
                  cache_spec, cache_spec, table_spec, table_spec, gain_spec, gain_spec],
        out_specs=pl.BlockSpec((DEC_SEQ, qcols), lambda b, j: (b, j)),
        out_shape=jax.ShapeDtypeStruct((N_LAT, D_MODEL), BF16),
        compiler_params=_cparams(("parallel", "arbitrary")),
        name="gqa_lat_attn",
    )(qkv, qkv, qkv, cache_k, cache_v, cos, sin, q_gain.reshape(1, HEAD_DIM), k_gain.reshape(1, HEAD_DIM))


NA_ROWS = DEC_SEQ // GRID_W
NA_WR = min(WIN_R, NA_ROWS)
NA_WKEYS = NA_WR * GRID_W


NA_REL_ROWS = 2 * WIN_R - 1
NA_ROW_PAIRS = NA_WR // 2


def _na_bias_table(rpb):
    col = jnp.arange(GRID_W)
    cs = jnp.clip(col - WIN_C // 2, 0, GRID_W - WIN_C)
    col_ok = (col[None, :] >= cs[:, None]) & (col[None, :] < cs[:, None] + WIN_C)
    dc = jnp.clip(col[None, :] - col[:, None] + WIN_C - 1, 0, 2 * WIN_C - 2)
    blocks = jnp.where(col_ok[None, None], rpb[:, :, dc], NEG_BIG).astype(F32)
    return jnp.concatenate([blocks[:, :-1], blocks[:, 1:]], axis=-1)


def _na_lat_kernel(q_ref, k_ref, v_ref, kc_ref, vc_ref, bias_ref, o_ref,
                   kb_ref, vb_ref, slat_ref, elat_ref, olat_ref):
    scale = HEAD_DIM ** -0.5
    kb_ref[...] = k_ref[...].astype(BF16)
    vb_ref[...] = v_ref[...].astype(BF16)

    def window(r):
        rs = min(max(r - NA_WR // 2, 0), NA_ROWS - NA_WR)
        return (slice(r * GRID_W, (r + 1) * GRID_W), slice(rs * GRID_W, rs * GRID_W + NA_WKEYS),
                rs - r + WIN_R - 1)

    for r in range(NA_ROWS):
        q_rows, k_rows, rel0 = window(r)
        bias = jnp.concatenate([bias_ref[rel0 + 2 * p] for p in range(NA_ROW_PAIRS)], axis=1)
        slat_ref[q_rows, :] = _dot_nt(q_ref[q_rows, :].astype(BF16), kb_ref[k_rows, :]) * scale + bias
    s_lat = slat_ref[...]
    s_ctx = _dot_nt(q_ref[...].astype(BF16), kc_ref[...].astype(BF16)) * scale
    m = jnp.maximum(jnp.max(s_lat, axis=-1, keepdims=True), jnp.max(s_ctx, axis=-1, keepdims=True))
    e_lat = jnp.exp(s_lat - m)
    e_ctx = jnp.exp(s_ctx - m)
    l = jnp.sum(e_lat, axis=-1, keepdims=True) + jnp.sum(e_ctx, axis=-1, keepdims=True)
    elat_ref[...] = e_lat.astype(BF16)
    o_ctx = _dot(e_ctx.astype(BF16), vc_ref[...].astype(BF16))
    for r in range(NA_ROWS):
        q_rows, k_rows, _ = window(r)
        olat_ref[q_rows, :] = _dot(elat_ref[q_rows, :], vb_ref[k_rows, :])
    o_ref[...] = ((olat_ref[...] + o_ctx) / l).astype(o_ref.dtype)


def na_latent_attn(qkv, cache_k, cache_v, rpb):
    bias = _na_bias_table(rpb)
    rb0 = N_CTX // DEC_SEQ
    cache_spec = pl.BlockSpec((None, PAST_LEN, HEAD_DIM), lambda h, b: (b, 0, h))
    return pl.pallas_call(
        _na_lat_kernel,
        grid=(NA_HEADS, DEC_BATCH),
        in_specs=[pl.BlockSpec((DEC_SEQ, HEAD_DIM), lambda h, b: (rb0 + b, h)),
                  pl.BlockSpec((DEC_SEQ, HEAD_DIM), lambda h, b: (rb0 + b, NA_HEADS + h)),
                  pl.BlockSpec((DEC_SEQ, HEAD_DIM), lambda h, b: (rb0 + b, 2 * NA_HEADS + h)),
                  cache_spec, cache_spec,
                  pl.BlockSpec((None, NA_REL_ROWS - 1, GRID_W, 2 * GRID_W), lambda h, b: (h, 0, 0, 0))],
        out_specs=pl.BlockSpec((DEC_SEQ, HEAD_DIM), lambda h, b: (b, h)),
        out_shape=jax.ShapeDtypeStruct((N_LAT, D_MODEL), BF16),
        scratch_shapes=[pltpu.VMEM((DEC_SEQ, HEAD_DIM), BF16), pltpu.VMEM((DEC_SEQ, HEAD_DIM), BF16),
                        pltpu.VMEM((DEC_SEQ, NA_WKEYS), F32), pltpu.VMEM((DEC_SEQ, NA_WKEYS), BF16),
                        pltpu.VMEM((DEC_SEQ, HEAD_DIM), F32)],
        compiler_params=_cparams(("parallel", "arbitrary")),
        name="na_lat_attn",
    )(qkv, qkv, qkv, cache_k, cache_v, bias)


LRU_ROWS = 2048
LRU_CTX_STEPS = N_CTX // LRU_ROWS
LRU_SEQ_PAD = 8
LRU_PITCH_PAD = 4
LRU_SCAN_ROWS = max(LRU_ROWS // SEQ * (SEQ + LRU_PITCH_PAD), LRU_ROWS // DEC_SEQ * (DEC_SEQ + LRU_PITCH_PAD))


def _lru_body(xr_ref, gb_ref, cw_ref, cb_ref, wg_ref, bg_ref, lam_ref, h0_ref, y_ref, fin_ref,
              af_ref, uf_ref, ab_ref, ub_ref, hf_ref, hb_ref, *, n_seq, seq_len):
    rows = n_seq * seq_len
    halves = [slice(c * LANES, (c + 1) * LANES) for c in range(LRU_BW // LANES)]
    pitch = seq_len + LRU_PITCH_PAD
    seqs = [(slice(s * seq_len, (s + 1) * seq_len), slice(s * pitch, s * pitch + seq_len))
            for s in range(n_seq)]
    x = xr_ref[...]
    t = lax.broadcasted_iota(jnp.int32, (rows, LRU_BW), 0) & (seq_len - 1)
    cw = cw_ref[...]
    xf = (jnp.where(t >= 2, pltpu.roll(x, 2, 0), 0.0) * cw[0:1]
          + jnp.where(t >= 1, pltpu.roll(x, 1, 0), 0.0) * cw[1:2]
          + x * cw[2:3]
          + jnp.where(t < seq_len - 1, pltpu.roll(x, rows - 1, 0), 0.0) * cw[3:4]) + cb_ref[...]
    xb = xf.astype(BF16)
    sp = _softplus(-lam_ref[...])
    for d, (a_ref, u_ref) in enumerate(((af_ref, uf_ref), (ab_ref, ub_ref))):
        bg = bg_ref[d]
        r_gate = _sigmoid(_dot(xb, wg_ref[d, 0].astype(BF16)) + bg[0:1])
        i_gate = _sigmoid(_dot(xb, wg_ref[d, 1].astype(BF16)) + bg[1:2])
        a = jnp.exp(-LRU_C * r_gate * sp[d:d + 1])
        u = jnp.sqrt(1.0 - a * a) * i_gate * xf
        for c, cols in enumerate(halves):
            for src, dst in seqs:
                a_ref[c, dst, :] = a[src, cols]
                u_ref[c, dst, :] = u[src, cols]

    def step(i, carry):
        rf = pl.ds(i, n_seq, stride=pitch)
        rb = pl.ds(seq_len - 1 - i, n_seq, stride=pitch)
        out = []
        for c in range(len(halves)):
            hf = af_ref[c, rf, :] * carry[2 * c] + uf_ref[c, rf, :]
            hf_ref[c, rf, :] = hf
            hb = ab_ref[c, rb, :] * carry[2 * c + 1] + ub_ref[c, rb, :]
            hb_ref[c, rb, :] = hb
            out += [hf, hb]
        return tuple(out)

    init = tuple(h0_ref[d, 0:n_seq, cols] for cols in halves for d in range(2))
    fin = lax.fori_loop(0, seq_len, step, init, unroll=4)
    fin_ref[...] = jnp.zeros(fin_ref.shape, F32)
    gate = jax.nn.gelu(gb_ref[...])
    for c, cols in enumerate(halves):
        fin_ref[0, 0:n_seq, cols] = fin[2 * c]
        fin_ref[1, 0:n_seq, cols] = fin[2 * c + 1]
        for src, dst in seqs:
            h = hf_ref[c, dst, :] + hb_ref[c, dst, :]
            y_ref[src, cols] = (h * gate[src, cols]).astype(y_ref.dtype)


def _lru_kernel(*refs):
    i = pl.program_id(0)

    @pl.when(i < LRU_CTX_STEPS)
    def _():
        _lru_body(*refs, n_seq=LRU_ROWS // SEQ, seq_len=SEQ)

    @pl.when(i >= LRU_CTX_STEPS)
    def _():
        _lru_body(*refs, n_seq=LRU_ROWS // DEC_SEQ, seq_len=DEC_SEQ)


def rglru_mix(proj, state, conv_w, conv_b, w_gates, b_gates, lam):
    assert LRU_ROWS // SEQ == LRU_SEQ_PAD and LRU_ROWS // DEC_SEQ == DEC_BATCH
    n_steps = N_TOK // LRU_ROWS
    h0 = jnp.zeros((2, n_steps * LRU_SEQ_PAD, D_RNN), F32)
    h0 = h0.at[:, BATCH:BATCH + DEC_BATCH].set(state.transpose(1, 0, 2))
    blk = lambda i, n: (i, n)
    chan = lambda i, n: (0, n)
    y, fin = pl.pallas_call(
        _lru_kernel,
        grid=(n_steps, LRU_BLOCKS),
        in_specs=[pl.BlockSpec((LRU_ROWS, LRU_BW), blk),
                  pl.BlockSpec((LRU_ROWS, LRU_BW), lambda i, n: (i, LRU_BLOCKS + n)),
                  pl.BlockSpec((CONV_W, LRU_BW), chan),
                  pl.BlockSpec((1, LRU_BW), chan),
                  pl.BlockSpec((2, 2, None, LRU_BW, LRU_BW), lambda i, n: (0, 0, n, 0, 0)),
                  pl.BlockSpec((2, 2, LRU_BW), lambda i, n: (0, 0, n)),
                  pl.BlockSpec((2, LRU_BW), chan),
                  pl.BlockSpec((2, LRU_SEQ_PAD, LRU_BW), lambda i, n: (0, i, n))],
        out_specs=[pl.BlockSpec((LRU_ROWS, LRU_BW), blk),
                   pl.BlockSpec((2, LRU_SEQ_PAD, LRU_BW), lambda i, n: (0, i, n))],
        out_shape=[jax.ShapeDtypeStruct((N_TOK, D_RNN), BF16),
                   jax.ShapeDtypeStruct((2, n_steps * LRU_SEQ_PAD, D_RNN), F32)],
        scratch_shapes=[pltpu.VMEM((LRU_BW // LANES, LRU_SCAN_ROWS, LANES), F32)] * 6,
        compiler_params=_cparams(("parallel", "arbitrary")),
        name="rglru_mix",
    )(proj, proj, conv_w, conv_b.reshape(1, D_RNN), w_gates, b_gates, lam, h0)
    return y, fin[:, :BATCH].transpose(1, 0, 2)


ML_GATE_COLS = 4 * ML_HEADS


def _pick_lane(x, idx):
    lane = lax.broadcasted_iota(jnp.int32, x.shape, 1)
    return jnp.sum(jnp.where(lane == idx, x, 0.0), axis=1, keepdims=True)


def _pick_row(x, idx):
    row = lax.broadcasted_iota(jnp.int32, x.shape, 0)
    return jnp.sum(jnp.where(row == idx, x, 0.0), axis=0, keepdims=True)


def _mlstm_kernel(q_ref, k_ref, v_ref, og_ref, g_ref, gt_ref, bg_ref, bgt_ref, ng_ref, *rest,
                  seq_len, zero_init, with_state):
    rest = list(rest)
    if not zero_init:
        c0_ref, n0_ref, m0_ref = rest[:3]
        rest = rest[3:]
    y_ref = rest.pop(0)
    if with_state:
        co_ref, no_ref, mo_ref = rest[:3]
        rest = rest[3:]
    h_ref, c_ref, n_ref = rest
    head = pl.program_id(1)
    n_chunks = seq_len // ML_CHUNK
    sub = lax.broadcasted_iota(jnp.int32, (ML_CHUNK, ML_CHUNK), 0)
    lane = lax.broadcasted_iota(jnp.int32, (ML_CHUNK, ML_CHUNK), 1)

    if zero_init:
        c_ref[...] = jnp.zeros(c_ref.shape, F32)
        n_ref[...] = jnp.zeros(n_ref.shape, F32)
        m_init = (jnp.zeros((1, 1), F32),) * 2
    else:
        c_ref[...] = c0_ref[...]
        n_ref[...] = n0_ref[...]
        m_init = (m0_ref[0], m0_ref[1])

    def one_chunk(d, c_idx, m_prev):
        valid = (lane <= sub) if d == 0 else (lane >= sub)
        valid_t = (sub <= lane) if d == 0 else (sub >= lane)
        i_col = d * 2 * ML_HEADS + head
        f_col = i_col + ML_HEADS
        rows = pl.ds(pl.multiple_of(c_idx * ML_CHUNK, ML_CHUNK), ML_CHUNK)
        g = g_ref[rows, :] + bg_ref[...]
        gt = gt_ref[:, rows] + bgt_ref[...]
        li_col = _pick_lane(g, i_col)
        lf_col = -_softplus(-_pick_lane(g, f_col))
        li_row = _pick_row(gt, i_col)
        lf_row = -_softplus(-_pick_row(gt, f_col))
        bcum_col = jnp.sum(jnp.where(valid, lf_row, 0.0), axis=1, keepdims=True)
        bcum_row = jnp.sum(jnp.where(valid_t, lf_col, 0.0), axis=0, keepdims=True)
        b_last = jnp.sum(lf_row, axis=1, keepdims=True)
        dmat = jnp.where(valid, bcum_col - bcum_row + li_row, NEG_BIG)
        m_inter = bcum_col + m_prev
        m_t = jnp.maximum(m_inter, jnp.max(dmat, axis=1, keepdims=True))
        qf = q_ref[rows, :]
        qb = qf.astype(BF16)
        kf = k_ref[rows, :] * (ML_DK ** -0.5)
        vb = v_ref[rows, :].astype(BF16)
        s = _dot_nt(qb, kf.astype(BF16)) * jnp.exp(dmat - m_t)
        inter = jnp.exp(m_inter - m_t)
        num = _dot(s.astype(BF16), vb) + inter * _dot(qb, c_ref[d].astype(BF16))
        den = (jnp.sum(s, axis=1, keepdims=True)
               + inter * jnp.sum(qf * n_ref[d], axis=1, keepdims=True))
        h_ref[d, rows, :] = num / jnp.maximum(jnp.abs(den), jnp.exp(-m_t))
        dec = b_last - bcum_col + li_col
        m_new = jnp.maximum(b_last + m_prev, jnp.max(dec, axis=0, keepdims=True))
        kw = kf * jnp.exp(dec - m_new)
        carry_scale = jnp.exp(b_last + m_prev - m_new)
        c_ref[d] = carry_scale * c_ref[d] + _dot_tn(kw.astype(BF16), vb)
        n_ref[d] = carry_scale * n_ref[d] + jnp.sum(kw, axis=0, keepdims=True)
        return m_new

    def both(ci, m_prev):
        return one_chunk(0, ci, m_prev[0]), one_chunk(1, n_chunks - 1 - ci, m_prev[1])

    m_fin = lax.fori_loop(0, n_chunks, both, m_init)
    if with_state:
        co_ref[...] = c_ref[...]
        no_ref[...] = n_ref[...]
        mo_ref[0] = m_fin[0]
        mo_ref[1] = m_fin[1]

    hs = h_ref[0] + h_ref[1]
    hn = hs * lax.rsqrt(jnp.mean(hs * hs, axis=-1, keepdims=True) + EPS) * ng_ref[...]
    y_ref[...] = (jax.nn.sigmoid(og_ref[...]) * hn).astype(y_ref.dtype)


def mlstm_mix(proj, gates, gates_t, b_gates, norm_g, *, row0, n_seq, seq_len, state=None, with_state):
    rb0 = row0 // seq_len
    kq = ML_HEADS * ML_DK // ML_DK
    v0 = 2 * ML_HEADS * ML_DK // ML_DV
    o0 = v0 + ML_HEADS
    zero_init = state is None
    bias = jnp.pad(b_gates.reshape(1, ML_GATE_COLS), ((0, 0), (0, LANES - ML_GATE_COLS)))
    in_specs = [pl.BlockSpec((seq_len, ML_DK), lambda b, h: (rb0 + b, h)),
                pl.BlockSpec((seq_len, ML_DK), lambda b, h: (rb0 + b, kq + h)),
                pl.BlockSpec((seq_len, ML_DV), lambda b, h: (rb0 + b, v0 + h)),
                pl.BlockSpec((seq_len, ML_DV), lambda b, h: (rb0 + b, o0 + h)),
                pl.BlockSpec((seq_len, LANES), lambda b, h: (rb0 + b, 0)),
                pl.BlockSpec((ML_GATE_COLS, seq_len), lambda b, h: (0, rb0 + b)),
                pl.BlockSpec((1, LANES), lambda b, h: (0, 0)),
                pl.BlockSpec((ML_GATE_COLS, 1), lambda b, h: (0, 0)),
                pl.BlockSpec((1, ML_DV), lambda b, h: (0, h))]
    args = [proj, proj, proj, proj, gates, gates_t, bias, b_gates.reshape(ML_GATE_COLS, 1),
            norm_g.reshape(1, ML_HEADS * ML_DV)]
    c_spec = pl.BlockSpec((None, 2, None, ML_DK, ML_DV), lambda b, h: (b, 0, h, 0, 0))
    n_spec = pl.BlockSpec((None, 2, None, 1, ML_DK), lambda b, h: (b, 0, h, 0, 0))
    m_spec = pl.BlockSpec((None, 2, None, 1, 1), lambda b, h: (b, 0, h, 0, 0))
    if not zero_init:
        c0, n0, m0 = state
        in_specs += [c_spec, n_spec, m_spec]
        args += [c0, n0.reshape(n_seq, 2, ML_HEADS, 1, ML_DK), m0.reshape(n_seq, 2, ML_HEADS, 1, 1)]
    out_specs = [pl.BlockSpec((seq_len, ML_DV), lambda b, h: (b, h))]
    out_shape = [jax.ShapeDtypeStruct((n_seq * seq_len, ML_HEADS * ML_DV), BF16)]
    if with_state:
        out_specs += [c_spec, n_spec, m_spec]
        out_shape += [jax.ShapeDtypeStruct((n_seq, 2, ML_HEADS, ML_DK, ML_DV), F32),
                      jax.ShapeDtypeStruct((n_seq, 2, ML_HEADS, 1, ML_DK), F32),
                      jax.ShapeDtypeStruct((n_seq, 2, ML_HEADS, 1, 1), F32)]
    res = pl.pallas_call(
        functools.partial(_mlstm_kernel, seq_len=seq_len, zero_init=zero_init, with_state=with_state),
        grid=(n_seq, ML_HEADS),
        in_specs=in_specs,
        out_specs=out_specs,
        out_shape=out_shape,
        scratch_shapes=[pltpu.VMEM((2, seq_len, ML_DV), F32), pltpu.VMEM((2, ML_DK, ML_DV), F32),
                        pltpu.VMEM((2, 1, ML_DK), F32)],
        compiler_params=_cparams(("parallel", "arbitrary")),
        name="mlstm_mix",
    )(*args)
    if not with_state:
        return res[0]
    y, c_fin, n_fin, m_fin = res
    return y, c_fin, n_fin.reshape(n_seq, 2, ML_HEADS, ML_DK), m_fin.reshape(n_seq, 2, ML_HEADS)


def kernel(x_prompt, x_sample, cache_na_k, cache_na_v, state_lru, state_mlstm_C, state_mlstm_n, state_mlstm_m, cache_gqa_k, cache_gqa_v, c, c_ctx, ada_w, ada_b, norm_g, mlp_up, mlp_down, na_w_qkv, na_rpb, na_w_o, lru_w_in, lru_conv_w, lru_conv_b, lru_w_gates, lru_b_gates, lru_lambda, lru_w_o, ml_w_in, ml_w_gates, ml_b_gates, ml_norm_g, ml_w_o, gqa_w_qkv, gqa_q_norm, gqa_k_norm, gqa_w_o):
    x = (x_prompt.reshape(N_CTX, D_MODEL), x_sample.reshape(N_LAT, D_MODEL))
    cond = jnp.concatenate([c_ctx[None], c, jnp.zeros((MOD_ROWS - 1 - DEC_BATCH, D_MODEL), F32)], axis=0)
    mods = adaln_all(cond, ada_w, ada_b)
    gains = norm_g.reshape(DEPTH, 4, 1, D_MODEL)

    outs = {}
    h = norm_mod(x, gains, mods, 0)
    for i in range(DEPTH):
        kind = i % 4
        if kind == 0:
            qkv = matmul(h, na_w_qkv[0], name="na_qkv")
            shp = (BATCH, 1, SEQ, NA_HEADS, HEAD_DIM)
            yp, kp, vp = na_context_attn(qkv)
            outs['na_k'], outs['na_v'] = kp.reshape(shp), vp.reshape(shp)
            yl = na_latent_attn(qkv, cache_na_k[:, 0].reshape(DEC_BATCH, PAST_LEN, D_MODEL),
                                cache_na_v[:, 0].reshape(DEC_BATCH, PAST_LEN, D_MODEL), na_rpb[0])
            y = matmul(yp, na_w_o[0], a2=yl, name="na_out")
        elif kind == 1:
            proj = matmul(h, lru_w_in[0], name="lru_in")
            pre, fin = rglru_mix(proj, state_lru[:, 0], lru_conv_w[0], lru_conv_b[0], lru_w_gates[0],
                                 lru_b_gates[0], lru_lambda[0])
            outs['lru'] = fin[:, None]
            y = matmul(pre, lru_w_o[0], name="lru_out")
        elif kind == 2:
            proj = matmul(h, ml_w_in[0], name="ml_in")
            wg = jnp.pad(ml_w_gates[0], ((0, 0), (0, LANES - ML_GATE_COLS)))
            gates = matmul(h, wg, tn=LANES, name="ml_gates")
            gates_t = gates[:, :ML_GATE_COLS].T
            yp, cp, np_, mp = mlstm_mix(proj, gates, gates_t, ml_b_gates[0], ml_norm_g[0], row0=0,
                                        n_seq=BATCH, seq_len=SEQ, with_state=True)
            yl = mlstm_mix(proj, gates, gates_t, ml_b_gates[0], ml_norm_g[0], row0=N_CTX,
                           n_seq=DEC_BATCH, seq_len=DEC_SEQ, with_state=False,
                           state=(state_mlstm_C[:, 0], state_mlstm_n[:, 0], state_mlstm_m[:, 0]))
            outs['mc'], outs['mn'], outs['mm'] = cp[:, None], np_[:, None], mp[:, None]
            y = matmul(yp, ml_w_o[0], a2=yl, name="ml_out")
        else:
            qkv = matmul(h, gqa_w_qkv[0], name="gqa_qkv")
            kv_cols = GQA_KV_HEADS * HEAD_DIM
            yp, kp, vp = gqa_context_attn(qkv, gqa_q_norm[0], gqa_k_norm[0])
            yl = gqa_latent_attn(qkv, cache_gqa_k[:, 0].reshape(DEC_BATCH, PAST_LEN, kv_cols),
                                 cache_gqa_v[:, 0].reshape(DEC_BATCH, PAST_LEN, kv_cols),
                                 gqa_q_norm[0], gqa_k_norm[0])
            shp = (BATCH, 1, SEQ, GQA_KV_HEADS, HEAD_DIM)
            outs['gk'], outs['gv'] = kp.reshape(shp), vp.reshape(shp)
            y = matmul(yp, gqa_w_o[0], a2=yl, name="gqa_out")

        x, h2 = resid_norm_mod(x, y, gains, mods, layer=i, ga=1, gate=2, nxt=(i, 2, 3, 4))
        u = matmul(h2, mlp_up, layer=i, out_dtype=BF16, act="relu2", name="mlp_up")
        z = matmul(u, mlp_down, layer=i, tn=1024, tk=2048, name="mlp_down")
        last = i + 1 == DEPTH
        nxt = None if last else (i + 1, 0, 0, 1)
        x, h = resid_norm_mod(x, z, gains, mods, layer=i, ga=3, gate=5, nxt=nxt, split_out=last)

    return (x[0].reshape(BATCH, SEQ, D_MODEL), x[1].reshape(DEC_BATCH, DEC_SEQ, D_MODEL),
            outs['na_k'], outs['na_v'], outs['lru'], outs['mc'], outs['mn'], outs['mm'], outs['gk'], outs['gv'])
```

```python
import functools

import jax
import jax.numpy as jnp
from jax import lax
from jax.experimental import pallas as pl
from jax.experimental.pallas import tpu as pltpu

D_MODEL = 4096
BATCH = 16
SEQ = 256
DEPTH = 4
DEC_BATCH = 2
DEC_SEQ = 1024
PAST_LEN = 512
GRID_W = 64
HEAD_DIM = 128
NA_HEADS = 32
WIN_R = 8
WIN_C = 16
GQA_HEADS = 32
GQA_KV_HEADS = 8
GQA_GROUP = GQA_HEADS // GQA_KV_HEADS
ROPE_THETA = 10000.0
D_RNN = D_MODEL
LRU_BLOCKS = 16
LRU_BW = 256
CONV_W = 4
LRU_C = 8.0
ML_HEADS = 8
ML_DK = 256
ML_DV = 512
ML_CHUNK = 128
D_FF = 4 * D_MODEL
EPS = 1e-6

N_CTX = BATCH * SEQ
N_LAT = DEC_BATCH * DEC_SEQ
N_TOK = N_CTX + N_LAT
MOD_ROWS = 8
LANES = 128
NEG_BIG = -1e30

VMEM_LIMIT = 56 * 1024 * 1024

BF16 = jnp.bfloat16
F32 = jnp.float32


def _cparams(sem):
    return pltpu.CompilerParams(dimension_semantics=sem, vmem_limit_bytes=VMEM_LIMIT)


def _dot(a, b):
    return jnp.dot(a, b, preferred_element_type=F32)


def _dot_nt(a, b):
    return lax.dot_general(a, b, (((1,), (1,)), ((), ())), preferred_element_type=F32)


def _dot_tn(a, b):
    return lax.dot_general(a, b, (((0,), (0,)), ((), ())), preferred_element_type=F32)


def _sigmoid(x):
    return 0.5 * (jnp.tanh(0.5 * x) + 1.0)


def _softplus(x):
    return jnp.maximum(x, 0.0) + jnp.log1p(jnp.exp(-jnp.abs(x)))


def _mm_kernel(*refs, act, n_first):
    w_ref, o_ref = refs[-2:]

    def tile(a_ref):
        acc = _dot(a_ref[...], w_ref[...].astype(BF16))
        if act == "relu2":
            r = jnp.maximum(acc, 0.0)
            acc = r * r
        o_ref[...] = acc.astype(o_ref.dtype)

    if len(refs) == 3:
        tile(refs[0])
    else:
        i = pl.program_id(0)
        pl.when(i < n_first)(lambda: tile(refs[0]))
        pl.when(i >= n_first)(lambda: tile(refs[1]))


def _mm_kernel_kgrid(a_ref, w_ref, o_ref):
    @pl.when(pl.program_id(2) == 0)
    def _():
        o_ref[...] = jnp.zeros(o_ref.shape, o_ref.dtype)

    o_ref[...] += _dot(a_ref[...], w_ref[...].astype(BF16))


def matmul(a, w, *, layer=0, a2=None, out_dtype=F32, act=None, tm=1024, tn=512, tk=4096, name="matmul"):
    m1, k = a.shape
    m = m1 + (0 if a2 is None else a2.shape[0])
    k2, n = w.shape[-2:]
    assert k == k2 and a.dtype == BF16
    lead = () if w.ndim == 2 else (None,)
    at = () if w.ndim == 2 else (layer,)
    tm, tn, tk = min(tm, m), min(tn, n), min(tk, k)
    assert m1 % tm == 0 and m % tm == 0 and n % tn == 0 and k % tk == 0
    nk = k // tk
    n_first = m1 // tm
    if nk == 1:
        a_specs = [pl.BlockSpec((tm, k), lambda i, j: (jnp.minimum(i, n_first - 1), 0))]
        a_args = [a]
        if a2 is not None:
            assert a2.dtype == BF16 and a2.shape[1] == k
            a_specs.append(pl.BlockSpec((tm, k), lambda i, j: (jnp.maximum(i - n_first, 0), 0)))
            a_args.append(a2)
        return pl.pallas_call(
            functools.partial(_mm_kernel, act=act, n_first=n_first),
            grid=(m // tm, n // tn),
            in_specs=a_specs + [pl.BlockSpec(lead + (k, tn), lambda i, j: at + (0, j))],
            out_specs=pl.BlockSpec((tm, tn), lambda i, j: (i, j)),
            out_shape=jax.ShapeDtypeStruct((m, n), out_dtype),
            compiler_params=_cparams(("parallel", "arbitrary")),
            name=name,
        )(*a_args, w)
    assert act is None and a2 is None and out_dtype == F32
    return pl.pallas_call(
        _mm_kernel_kgrid,
        grid=(m // tm, n // tn, nk),
        in_specs=[pl.BlockSpec((tm, tk), lambda i, j, kk: (i, kk)),
                  pl.BlockSpec(lead + (tk, tn), lambda i, j, kk: at + (kk, j))],
        out_specs=pl.BlockSpec((tm, tn), lambda i, j, kk: (i, j)),
        out_shape=jax.ShapeDtypeStruct((m, n), out_dtype),
        compiler_params=_cparams(("parallel", "arbitrary", "arbitrary")),
        name=name,
    )(a, w)


def _ada_kernel(c_ref, w_ref, b_ref, o_ref):
    c = c_ref[...]
    a = (c * jax.nn.sigmoid(c)).astype(BF16)
    o_ref[...] = _dot(a, w_ref[...].astype(BF16)) + b_ref[...]


def adaln_all(cond, ada_w, ada_b, *, tn=512):
    n = 6 * D_MODEL
    out = pl.pallas_call(
        _ada_kernel,
        grid=(DEPTH, n // tn),
        in_specs=[pl.BlockSpec((MOD_ROWS, D_MODEL), lambda l, j: (0, 0)),
                  pl.BlockSpec((None, D_MODEL, tn), lambda l, j: (l, 0, j)),
                  pl.BlockSpec((None, 1, tn), lambda l, j: (l, 0, j))],
        out_specs=pl.BlockSpec((None, MOD_ROWS, tn), lambda l, j: (l, 0, j)),
        out_shape=jax.ShapeDtypeStruct((DEPTH, MOD_ROWS, n), F32),
        compiler_params=_cparams(("parallel", "arbitrary")),
        name="adaln",
    )(cond, ada_w, ada_b.reshape(DEPTH, 1, n))
    return out.reshape(DEPTH, MOD_ROWS, 6, 1, D_MODEL)


ROW_TILE = 256


def _row_group(i):
    return jnp.maximum(0, (i * ROW_TILE - N_CTX) // DEC_SEQ + 1)


def _rms(x, g):
    return x * lax.rsqrt(jnp.mean(x * x, axis=-1, keepdims=True) + EPS) * g


def _mod_spec(layer, which):
    return pl.BlockSpec((None, None, None, 1, D_MODEL),
                        lambda i: (layer, _row_group(i), which, 0, 0))


def _gain_spec(layer, which):
    return pl.BlockSpec((None, None, 1, D_MODEL), lambda i: (layer, which, 0, 0))


_ROWS_SPEC = pl.BlockSpec((ROW_TILE, D_MODEL), lambda i: (i, 0))


CTX_TILES = N_CTX // ROW_TILE
_CTX_ROWS_SPEC = pl.BlockSpec((ROW_TILE, D_MODEL), lambda i: (jnp.minimum(i, CTX_TILES - 1), 0))
_LAT_ROWS_SPEC = pl.BlockSpec((ROW_TILE, D_MODEL), lambda i: (jnp.maximum(i - CTX_TILES, 0), 0))


def _x_specs(x):
    return [_CTX_ROWS_SPEC, _LAT_ROWS_SPEC] if isinstance(x, tuple) else [_ROWS_SPEC]


def _x_args(x):
    return list(x) if isinstance(x, tuple) else [x]


def _load_rows(refs):
    if len(refs) == 1:
        return refs[0][...]
    return jnp.where(pl.program_id(0) < CTX_TILES, refs[0][...], refs[1][...])


def _norm_mod_kernel(*refs):
    g_ref, sh_ref, sc_ref, h_ref = refs[-4:]
    h = _rms(_load_rows(refs[:-4]), g_ref[...]) * (1.0 + sc_ref[...]) + sh_ref[...]
    h_ref[...] = h.astype(h_ref.dtype)


def norm_mod(x, gains, mods, layer):
    return pl.pallas_call(
        _norm_mod_kernel,
        grid=(N_TOK // ROW_TILE,),
        in_specs=_x_specs(x) + [_gain_spec(layer, 0), _mod_spec(layer, 0), _mod_spec(layer, 1)],
        out_specs=_ROWS_SPEC,
        out_shape=jax.ShapeDtypeStruct((N_TOK, D_MODEL), BF16),
        compiler_params=_cparams(("parallel",)),
        name="norm_mod",
    )(*_x_args(x), gains, mods, mods)


def _resid_kernel(*refs, n_x, with_h, split_out):
    x_refs, (y_ref, ga_ref, gate_ref), rest = refs[:n_x], refs[n_x:n_x + 3], refs[n_x + 3:]
    xn = _load_rows(x_refs) + gate_ref[...] * _rms(y_ref[...], ga_ref[...])
    if with_h:
        gb_ref, sh_ref, sc_ref = rest[:3]
        rest = rest[3:]
        h_ref = rest[-1]
        h = _rms(xn, gb_ref[...]) * (1.0 + sc_ref[...]) + sh_ref[...]
        h_ref[...] = h.astype(h_ref.dtype)
    if split_out:
        i = pl.program_id(0)

        @pl.when(i < CTX_TILES)
        def _():
            rest[0][...] = xn

        @pl.when(i >= CTX_TILES)
        def _():
            rest[1][...] = xn
    else:
        rest[0][...] = xn


def resid_norm_mod(x, y, gains, mods, *, layer, ga, gate, nxt, split_out=False):
    in_specs = _x_specs(x) + [_ROWS_SPEC, _gain_spec(layer, ga), _mod_spec(layer, gate)]
    args = _x_args(x) + [y, gains, mods]
    if split_out:
        out_specs = [_CTX_ROWS_SPEC, _LAT_ROWS_SPEC]
        out_shape = [jax.ShapeDtypeStruct((N_CTX, D_MODEL), F32), jax.ShapeDtypeStruct((N_LAT, D_MODEL), F32)]
    else:
        out_specs = [_ROWS_SPEC]
        out_shape = [jax.ShapeDtypeStruct((N_TOK, D_MODEL), F32)]
    if nxt is not None:
        nl, ng, nsh, nsc = nxt
        in_specs += [_gain_spec(nl, ng), _mod_spec(nl, nsh), _mod_spec(nl, nsc)]
        args += [gains, mods, mods]
        out_specs.append(_ROWS_SPEC)
        out_shape.append(jax.ShapeDtypeStruct((N_TOK, D_MODEL), BF16))
    res = pl.pallas_call(
        functools.partial(_resid_kernel, n_x=len(_x_args(x)), with_h=nxt is not None, split_out=split_out),
        grid=(N_TOK // ROW_TILE,),
        in_specs=in_specs,
        out_specs=out_specs,
        out_shape=out_shape,
        compiler_params=_cparams(("arbitrary",) if split_out else ("parallel",)),
        name="resid_norm_mod",
    )(*args)
    x_new = (res[0], res[1]) if split_out else res[0]
    return x_new, (res[-1] if nxt is not None else None)


def _head_rms(x, g):
    return x * lax.rsqrt(jnp.mean(x * x, axis=-1, keepdims=True) + EPS) * g


def _rope(x, cos, sin):
    lane = lax.broadcasted_iota(jnp.int32, x.shape, 1)
    partner = jnp.where((lane & 63) < 32, pltpu.roll(x, LANES - 32, 1), pltpu.roll(x, 32, 1))
    return x * cos + partner * sin


def _rope_tables():
    t = jnp.arange(DEC_SEQ)
    half = HEAD_DIM // 2
    inv_freq = 1.0 / (ROPE_THETA ** (jnp.arange(0, half, 2, dtype=F32) / half))
    ang_r = (t // GRID_W).astype(F32)[:, None] * inv_freq
    ang_c = (t % GRID_W).astype(F32)[:, None] * inv_freq
    cos = jnp.concatenate([jnp.cos(ang_r)] * 2 + [jnp.cos(ang_c)] * 2, axis=-1)
    sin = jnp.concatenate([-jnp.sin(ang_r), jnp.sin(ang_r), -jnp.sin(ang_c), jnp.sin(ang_c)], axis=-1)
    return cos, sin


def _softmax_pv(scores, values):
    m = functools.reduce(jnp.maximum, [jnp.max(s, axis=-1, keepdims=True) for s in scores])
    es = [jnp.exp(s - m) for s in scores]
    l = functools.reduce(jnp.add, [jnp.sum(e, axis=-1, keepdims=True) for e in es])
    o = functools.reduce(jnp.add, [_dot(e.astype(BF16), v) for e, v in zip(es, values)])
    return o / l


def _ctx_attn_kernel(q_ref, k_ref, v_ref, *rest, n_kv, group, normed):
    if normed:
        qg_ref, kg_ref, o_ref, ko_ref, vo_ref = rest
    else:
        o_ref, ko_ref, vo_ref = rest
    scale = HEAD_DIM ** -0.5
    for j in range(n_kv):
        ks = slice(j * HEAD_DIM, (j + 1) * HEAD_DIM)
        k = k_ref[:, ks]
        v = v_ref[:, ks]
        if normed:
            k = _head_rms(k, kg_ref[...])
        ko_ref[:, ks] = k
        vo_ref[:, ks] = v
        kb = k.astype(BF16)
        vb = v.astype(BF16)
        for g in range(group):
            qs = slice((j * group + g) * HEAD_DIM, (j * group + g + 1) * HEAD_DIM)
            q = q_ref[:, qs]
            if normed:
                q = _head_rms(q, qg_ref[...])
            s = _dot_nt(q.astype(BF16), kb) * scale
            o_ref[:, qs] = _softmax_pv([s], [vb]).astype(o_ref.dtype)


CTX_COLS = 1024


def na_context_attn(qkv):
    nb = D_MODEL // CTX_COLS
    blk = pl.BlockSpec((SEQ, CTX_COLS), lambda b, j: (b, j))
    kv_shape = jax.ShapeDtypeStruct((N_CTX, D_MODEL), F32)
    return pl.pallas_call(
        functools.partial(_ctx_attn_kernel, n_kv=CTX_COLS // HEAD_DIM, group=1, normed=False),
        grid=(BATCH, nb),
        in_specs=[blk,
                  pl.BlockSpec((SEQ, CTX_COLS), lambda b, j: (b, nb + j)),
                  pl.BlockSpec((SEQ, CTX_COLS), lambda b, j: (b, 2 * nb + j))],
        out_specs=[blk, blk, blk],
        out_shape=[jax.ShapeDtypeStruct((N_CTX, D_MODEL), BF16), kv_shape, kv_shape],
        compiler_params=_cparams(("parallel", "arbitrary")),
        name="na_ctx_attn",
    )(qkv, qkv, qkv)


def gqa_context_attn(qkv, q_gain, k_gain):
    n_kv = CTX_COLS // (GQA_GROUP * HEAD_DIM)
    kv_cols = n_kv * HEAD_DIM
    nb = D_MODEL // CTX_COLS
    k0 = D_MODEL // kv_cols
    v0 = k0 + GQA_KV_HEADS * HEAD_DIM // kv_cols
    kv_shape = jax.ShapeDtypeStruct((N_CTX, GQA_KV_HEADS * HEAD_DIM), F32)
    gain_spec = pl.BlockSpec((1, HEAD_DIM), lambda b, j: (0, 0))
    return pl.pallas_call(
        functools.partial(_ctx_attn_kernel, n_kv=n_kv, group=GQA_GROUP, normed=True),
        grid=(BATCH, nb),
        in_specs=[pl.BlockSpec((SEQ, CTX_COLS), lambda b, j: (b, j)),
                  pl.BlockSpec((SEQ, kv_cols), lambda b, j: (b, k0 + j)),
                  pl.BlockSpec((SEQ, kv_cols), lambda b, j: (b, v0 + j)),
                  gain_spec, gain_spec],
        out_specs=[pl.BlockSpec((SEQ, CTX_COLS), lambda b, j: (b, j)),
                   pl.BlockSpec((SEQ, kv_cols), lambda b, j: (b, j)),
                   pl.BlockSpec((SEQ, kv_cols), lambda b, j: (b, j))],
        out_shape=[jax.ShapeDtypeStruct((N_CTX, D_MODEL), BF16), kv_shape, kv_shape],
        compiler_params=_cparams(("parallel", "arbitrary")),
        name="gqa_ctx_attn",
    )(qkv, qkv, qkv, q_gain.reshape(1, HEAD_DIM), k_gain.reshape(1, HEAD_DIM))


def _gqa_lat_kernel(q_ref, k_ref, v_ref, kc_ref, vc_ref, cos_ref, sin_ref, qg_ref, kg_ref, o_ref):
    scale = HEAD_DIM ** -0.5
    cos, sin = cos_ref[...], sin_ref[...]
    kc = kc_ref[...].astype(BF16)
    vc = vc_ref[...].astype(BF16)
    k = _rope(_head_rms(k_ref[...], kg_ref[...]), cos, sin).astype(BF16)
    v = v_ref[...].astype(BF16)
    for g in range(GQA_GROUP):
        qs = slice(g * HEAD_DIM, (g + 1) * HEAD_DIM)
        q = _rope(_head_rms(q_ref[:, qs], qg_ref[...]), cos, sin).astype(BF16)
        s_ctx = _dot_nt(q, kc) * scale
        s_lat = _dot_nt(q, k) * scale
        o_ref[:, qs] = _softmax_pv([s_ctx, s_lat], [vc, v]).astype(o_ref.dtype)


def gqa_latent_attn(qkv, cache_k, cache_v, q_gain, k_gain):
    cos, sin = _rope_tables()
    qcols = GQA_GROUP * HEAD_DIM
    rb0 = N_CTX // DEC_SEQ
    k0 = D_MODEL // HEAD_DIM
    v0 = k0 + GQA_KV_HEADS
    cache_spec = pl.BlockSpec((None, PAST_LEN, HEAD_DIM), lambda b, j: (b, 0, j))
    table_spec = pl.BlockSpec((DEC_SEQ, HEAD_DIM), lambda b, j: (0, 0))
    gain_spec = pl.BlockSpec((1, HEAD_DIM), lambda b, j: (0, 0))
    return pl.pallas_call(
        _gqa_lat_kernel,
        grid=(DEC_BATCH, GQA_KV_HEADS),
        in_specs=[pl.BlockSpec((DEC_SEQ, qcols), lambda b, j: (rb0 + b, j)),
                  pl.BlockSpec((DEC_SEQ, HEAD_DIM), lambda b, j: (rb0 + b, k0 + j)),
                  pl.BlockSpec((DEC_SEQ, HEAD_DIM), lambda b, j: (rb0 + b, v0 + j)),
                  cache_spec, cache_spec, table_spec, table_spec, gain_spec, gain_spec],
        out_specs=pl.BlockSpec((DEC_SEQ, qcols), lambda b, j: (b, j)),
        out_shape=jax.ShapeDtypeStruct((N_LAT, D_MODEL), BF16),
        compiler_params=_cparams(("parallel", "arbitrary")),
        name="gqa_lat_attn",
    )(qkv, qkv, qkv, cache_k, cache_v, cos, sin, q_gain.reshape(1, HEAD_DIM), k_gain.reshape(1, HEAD_DIM))


NA_ROWS = DEC_SEQ // GRID_W
NA_WR = min(WIN_R, NA_ROWS)
NA_WKEYS = NA_WR * GRID_W


NA_REL_ROWS = 2 * WIN_R - 1
NA_ROW_PAIRS = NA_WR // 2


def _na_bias_table(rpb):
    col = jnp.arange(GRID_W)
    cs = jnp.clip(col - WIN_C // 2, 0, GRID_W - WIN_C)
    col_ok = (col[None, :] >= cs[:, None]) & (col[None, :] < cs[:, None] + WIN_C)
    dc = jnp.clip(col[None, :] - col[:, None] + WIN_C - 1, 0, 2 * WIN_C - 2)
    blocks = jnp.where(col_ok[None, None], rpb[:, :, dc], NEG_BIG).astype(F32)
    return jnp.concatenate([blocks[:, :-1], blocks[:, 1:]], axis=-1)


def _na_lat_kernel(q_ref, k_ref, v_ref, kc_ref, vc_ref, bias_ref, o_ref,
                   kb_ref, vb_ref, slat_ref, elat_ref, olat_ref):
    scale = HEAD_DIM ** -0.5
    kb_ref[...] = k_ref[...].astype(BF16)
    vb_ref[...] = v_ref[...].astype(BF16)

    def window(r):
        rs = min(max(r - NA_WR // 2, 0), NA_ROWS - NA_WR)
        return (slice(r * GRID_W, (r + 1) * GRID_W), slice(rs * GRID_W, rs * GRID_W + NA_WKEYS),
                rs - r + WIN_R - 1)

    for r in range(NA_ROWS):
        q_rows, k_rows, rel0 = window(r)
        bias = jnp.concatenate([bias_ref[rel0 + 2 * p] for p in range(NA_ROW_PAIRS)], axis=1)
        slat_ref[q_rows, :] = _dot_nt(q_ref[q_rows, :].astype(BF16), kb_ref[k_rows, :]) * scale + bias
    s_lat = slat_ref[...]
    s_ctx = _dot_nt(q_ref[...].astype(BF16), kc_ref[...].astype(BF16)) * scale
    m = jnp.maximum(jnp.max(s_lat, axis=-1, keepdims=True), jnp.max(s_ctx, axis=-1, keepdims=True))
    e_lat = jnp.exp(s_lat - m)
    e_ctx = jnp.exp(s_ctx - m)
    l = jnp.sum(e_lat, axis=-1, keepdims=True) + jnp.sum(e_ctx, axis=-1, keepdims=True)
    elat_ref[...] = e_lat.astype(BF16)
    o_ctx = _dot(e_ctx.astype(BF16), vc_ref[...].astype(BF16))
    for r in range(NA_ROWS):
        q_rows, k_rows, _ = window(r)
        olat_ref[q_rows, :] = _dot(elat_ref[q_rows, :], vb_ref[k_rows, :])
    o_ref[...] = ((olat_ref[...] + o_ctx) / l).astype(o_ref.dtype)


def na_latent_attn(qkv, cache_k, cache_v, rpb):
    bias = _na_bias_table(rpb)
    rb0 = N_CTX // DEC_SEQ
    cache_spec = pl.BlockSpec((None, PAST_LEN, HEAD_DIM), lambda h, b: (b, 0, h))
    return pl.pallas_call(
        _na_lat_kernel,
        grid=(NA_HEADS, DEC_BATCH),
        in_specs=[pl.BlockSpec((DEC_SEQ, HEAD_DIM), lambda h, b: (rb0 + b, h)),
                  pl.BlockSpec((DEC_SEQ, HEAD_DIM), lambda h, b: (rb0 + b, NA_HEADS + h)),
                  pl.BlockSpec((DEC_SEQ, HEAD_DIM), lambda h, b: (rb0 + b, 2 * NA_HEADS + h)),
                  cache_spec, cache_spec,
                  pl.BlockSpec((None, NA_REL_ROWS - 1, GRID_W, 2 * GRID_W), lambda h, b: (h, 0, 0, 0))],
        out_specs=pl.BlockSpec((DEC_SEQ, HEAD_DIM), lambda h, b: (b, h)),
        out_shape=jax.ShapeDtypeStruct((N_LAT, D_MODEL), BF16),
        scratch_shapes=[pltpu.VMEM((DEC_SEQ, HEAD_DIM), BF16), pltpu.VMEM((DEC_SEQ, HEAD_DIM), BF16),
                        pltpu.VMEM((DEC_SEQ, NA_WKEYS), F32), pltpu.VMEM((DEC_SEQ, NA_WKEYS), BF16),
                        pltpu.VMEM((DEC_SEQ, HEAD_DIM), F32)],
        compiler_params=_cparams(("parallel", "arbitrary")),
        name="na_lat_attn",
    )(qkv, qkv, qkv, cache_k, cache_v, bias)


LRU_ROWS = 2048
LRU_CTX_STEPS = N_CTX // LRU_ROWS
LRU_SEQ_PAD = 8
LRU_PITCH_PAD = 4
LRU_SCAN_ROWS = max(LRU_ROWS // SEQ * (SEQ + LRU_PITCH_PAD), LRU_ROWS // DEC_SEQ * (DEC_SEQ + LRU_PITCH_PAD))


def _lru_body(xr_ref, gb_ref, cw_ref, cb_ref, wg_ref, bg_ref, lam_ref, h0_ref, y_ref, fin_ref,
              af_ref, uf_ref, ab_ref, ub_ref, hf_ref, hb_ref, *, n_seq, seq_len):
    rows = n_seq * seq_len
    halves = [slice(c * LANES, (c + 1) * LANES) for c in range(LRU_BW // LANES)]
    pitch = seq_len + LRU_PITCH_PAD
    seqs = [(slice(s * seq_len, (s + 1) * seq_len), slice(s * pitch, s * pitch + seq_len))
            for s in range(n_seq)]
    x = xr_ref[...]
    t = lax.broadcasted_iota(jnp.int32, (rows, LRU_BW), 0) & (seq_len - 1)
    cw = cw_ref[...]
    xf = (jnp.where(t >= 2, pltpu.roll(x, 2, 0), 0.0) * cw[0:1]
          + jnp.where(t >= 1, pltpu.roll(x, 1, 0), 0.0) * cw[1:2]
          + x * cw[2:3]
          + jnp.where(t < seq_len - 1, pltpu.roll(x, rows - 1, 0), 0.0) * cw[3:4]) + cb_ref[...]
    xb = xf.astype(BF16)
    sp = _softplus(-lam_ref[...])
    for d, (a_ref, u_ref) in enumerate(((af_ref, uf_ref), (ab_ref, ub_ref))):
        bg = bg_ref[d]
        r_gate = _sigmoid(_dot(xb, wg_ref[d, 0].astype(BF16)) + bg[0:1])
        i_gate = _sigmoid(_dot(xb, wg_ref[d, 1].astype(BF16)) + bg[1:2])
        a = jnp.exp(-LRU_C * r_gate * sp[d:d + 1])
        u = jnp.sqrt(1.0 - a * a) * i_gate * xf
        for c, cols in enumerate(halves):
            for src, dst in seqs:
                a_ref[c, dst, :] = a[src, cols]
                u_ref[c, dst, :] = u[src, cols]

    def step(i, carry):
        rf = pl.ds(i, n_seq, stride=pitch)
        rb = pl.ds(seq_len - 1 - i, n_seq, stride=pitch)
        out = []
        for c in range(len(halves)):
            hf = af_ref[c, rf, :] * carry[2 * c] + uf_ref[c, rf, :]
            hf_ref[c, rf, :] = hf
            hb = ab_ref[c, rb, :] * carry[2 * c + 1] + ub_ref[c, rb, :]
            hb_ref[c, rb, :] = hb
            out += [hf, hb]
        return tuple(out)

    init = tuple(h0_ref[d, 0:n_seq, cols] for cols in halves for d in range(2))
    fin = lax.fori_loop(0, seq_len, step, init, unroll=4)
    fin_ref[...] = jnp.zeros(fin_ref.shape, F32)
    gate = jax.nn.gelu(gb_ref[...])
    for c, cols in enumerate(halves):
        fin_ref[0, 0:n_seq, cols] = fin[2 * c]
        fin_ref[1, 0:n_seq, cols] = fin[2 * c + 1]
        for src, dst in seqs:
            h = hf_ref[c, dst, :] + hb_ref[c, dst, :]
            y_ref[src, cols] = (h * gate[src, cols]).astype(y_ref.dtype)


def _lru_kernel(*refs):
    i = pl.program_id(0)

    @pl.when(i < LRU_CTX_STEPS)
    def _():
        _lru_body(*refs, n_seq=LRU_ROWS // SEQ, seq_len=SEQ)

    @pl.when(i >= LRU_CTX_STEPS)
    def _():
        _lru_body(*refs, n_seq=LRU_ROWS // DEC_SEQ, seq_len=DEC_SEQ)


def rglru_mix(proj, state, conv_w, conv_b, w_gates, b_gates, lam):
    assert LRU_ROWS // SEQ == LRU_SEQ_PAD and LRU_ROWS // DEC_SEQ == DEC_BATCH
    n_steps = N_TOK // LRU_ROWS
    h0 = jnp.zeros((2, n_steps * LRU_SEQ_PAD, D_RNN), F32)
    h0 = h0.at[:, BATCH:BATCH + DEC_BATCH].set(state.transpose(1, 0, 2))
    blk = lambda i, n: (i, n)
    chan = lambda i, n: (0, n)
    y, fin = pl.pallas_call(
        _lru_kernel,
        grid=(n_steps, LRU_BLOCKS),
        in_specs=[pl.BlockSpec((LRU_ROWS, LRU_BW), blk),
                  pl.BlockSpec((LRU_ROWS, LRU_BW), lambda i, n: (i, LRU_BLOCKS + n)),
                  pl.BlockSpec((CONV_W, LRU_BW), chan),
                  pl.BlockSpec((1, LRU_BW), chan),
                  pl.BlockSpec((2, 2, None, LRU_BW, LRU_BW), lambda i, n: (0, 0, n, 0, 0)),
                  pl.BlockSpec((2, 2, LRU_BW), lambda i, n: (0, 0, n)),
                  pl.BlockSpec((2, LRU_BW), chan),
                  pl.BlockSpec((2, LRU_SEQ_PAD, LRU_BW), lambda i, n: (0, i, n))],
        out_specs=[pl.BlockSpec((LRU_ROWS, LRU_BW), blk),
                   pl.BlockSpec((2, LRU_SEQ_PAD, LRU_BW), lambda i, n: (0, i, n))],
        out_shape=[jax.ShapeDtypeStruct((N_TOK, D_RNN), BF16),
                   jax.ShapeDtypeStruct((2, n_steps * LRU_SEQ_PAD, D_RNN), F32)],
        scratch_shapes=[pltpu.VMEM((LRU_BW // LANES, LRU_SCAN_ROWS, LANES), F32)] * 6,
        compiler_params=_cparams(("parallel", "arbitrary")),
        name="rglru_mix",
    )(proj, proj, conv_w, conv_b.reshape(1, D_RNN), w_gates, b_gates, lam, h0)
    return y, fin[:, :BATCH].transpose(1, 0, 2)


ML_GATE_COLS = 4 * ML_HEADS


def _pick_lane(x, idx):
    lane = lax.broadcasted_iota(jnp.int32, x.shape, 1)
    return jnp.sum(jnp.where(lane == idx, x, 0.0), axis=1, keepdims=True)


def _pick_row(x, idx):
    row = lax.broadcasted_iota(jnp.int32, x.shape, 0)
    return jnp.sum(jnp.where(row == idx, x, 0.0), axis=0, keepdims=True)


def _mlstm_kernel(q_ref, k_ref, v_ref, og_ref, g_ref, gt_ref, bg_ref, bgt_ref, ng_ref, *rest,
                  seq_len, zero_init, with_state):
    rest = list(rest)
    if not zero_init:
        c0_ref, n0_ref, m0_ref = rest[:3]
        rest = rest[3:]
    y_ref = rest.pop(0)
    if with_state:
        co_ref, no_ref, mo_ref = rest[:3]
        rest = rest[3:]
    (h_ref, c_ref, n_ref, dmat_ref, qk_ref, bcum_ref, rmax_ref, dec_ref, blast_ref, dmax_ref) = rest
    head = pl.program_id(1)
    n_chunks = seq_len // ML_CHUNK
    k_scale = ML_DK ** -0.5
    sub = lax.broadcasted_iota(jnp.int32, (ML_CHUNK, ML_CHUNK), 0)
    lane = lax.broadcasted_iota(jnp.int32, (ML_CHUNK, ML_CHUNK), 1)

    if zero_init:
        c_ref[...] = jnp.zeros(c_ref.shape, F32)
        n_ref[...] = jnp.zeros(n_ref.shape, F32)
        m_init = (jnp.zeros((1, 1), F32),) * 2
    else:
        c_ref[...] = c0_ref[...]
        n_ref[...] = n0_ref[...]
        m_init = (m0_ref[0], m0_ref[1])

    for c in range(n_chunks):
        rows = slice(c * ML_CHUNK, (c + 1) * ML_CHUNK)
        g = g_ref[rows, :] + bg_ref[...]
        gt = gt_ref[:, rows] + bgt_ref[...]
        qk_ref[c] = _dot_nt(q_ref[rows, :].astype(BF16), (k_ref[rows, :] * k_scale).astype(BF16))
        for d in range(2):
            valid = (lane <= sub) if d == 0 else (lane >= sub)
            valid_t = (sub <= lane) if d == 0 else (sub >= lane)
            i_col = d * 2 * ML_HEADS + head
            f_col = i_col + ML_HEADS
            li_col = _pick_lane(g, i_col)
            lf_col = -_softplus(-_pick_lane(g, f_col))
            li_row = _pick_row(gt, i_col)
            lf_row = -_softplus(-_pick_row(gt, f_col))
            bcum_col = jnp.sum(jnp.where(valid, lf_row, 0.0), axis=1, keepdims=True)
            bcum_row = jnp.sum(jnp.where(valid_t, lf_col, 0.0), axis=0, keepdims=True)
            b_last = jnp.sum(lf_row, axis=1, keepdims=True)
            dmat = jnp.where(valid, bcum_col - bcum_row + li_row, NEG_BIG)
            dec = b_last - bcum_col + li_col
            idx = d * n_chunks + c
            dmat_ref[idx] = dmat
            rmax_ref[idx] = jnp.max(dmat, axis=1, keepdims=True)
            bcum_ref[idx] = bcum_col
            dec_ref[idx] = dec
            blast_ref[idx] = b_last
            dmax_ref[idx] = jnp.max(dec, axis=0, keepdims=True)

    def one_chunk(d, c_idx, m_prev):
        idx = d * n_chunks + c_idx
        rows = pl.ds(pl.multiple_of(c_idx * ML_CHUNK, ML_CHUNK), ML_CHUNK)
        m_inter = bcum_ref[idx] + m_prev
        m_t = jnp.maximum(m_inter, rmax_ref[idx])
        qf = q_ref[rows, :]
        qb = qf.astype(BF16)
        kf = k_ref[rows, :] * k_scale
        vb = v_ref[rows, :].astype(BF16)
        s = qk_ref[c_idx] * jnp.exp(dmat_ref[idx] - m_t)
        inter = jnp.exp(m_inter - m_t)
        num = _dot(s.astype(BF16), vb) + inter * _dot(qb, c_ref[d].astype(BF16))
        den = (jnp.sum(s, axis=1, keepdims=True)
               + inter * jnp.sum(qf * n_ref[d], axis=1, keepdims=True))
        h_ref[d, rows, :] = num / jnp.maximum(jnp.abs(den), jnp.exp(-m_t))
        b_last = blast_ref[idx]
        m_new = jnp.maximum(b_last + m_prev, dmax_ref[idx])
        kw = kf * jnp.exp(dec_ref[idx] - m_new)
        carry_scale = jnp.exp(b_last + m_prev - m_new)
        c_ref[d] = carry_scale * c_ref[d] + _dot_tn(kw.astype(BF16), vb)
        n_ref[d] = carry_scale * n_ref[d] + jnp.sum(kw, axis=0, keepdims=True)
        return m_new

    def both(ci, m_prev):
        return one_chunk(0, ci, m_prev[0]), one_chunk(1, n_chunks - 1 - ci, m_prev[1])

    m_fin = lax.fori_loop(0, n_chunks, both, m_init)
    if with_state:
        co_ref[...] = c_ref[...]
        no_ref[...] = n_ref[...]
        mo_ref[0] = m_fin[0]
        mo_ref[1] = m_fin[1]

    hs = h_ref[0] + h_ref[1]
    hn = hs * lax.rsqrt(jnp.mean(hs * hs, axis=-1, keepdims=True) + EPS) * ng_ref[...]
    y_ref[...] = (jax.nn.sigmoid(og_ref[...]) * hn).astype(y_ref.dtype)


def mlstm_mix(proj, gates, gates_t, b_gates, norm_g, *, row0, n_seq, seq_len, state=None, with_state):
    rb0 = row0 // seq_len
    kq = ML_HEADS * ML_DK // ML_DK
    v0 = 2 * ML_HEADS * ML_DK // ML_DV
    o0 = v0 + ML_HEADS
    zero_init = state is None
    n_chunks = seq_len // ML_CHUNK
    bias = jnp.pad(b_gates.reshape(1, ML_GATE_COLS), ((0, 0), (0, LANES - ML_GATE_COLS)))
    in_specs = [pl.BlockSpec((seq_len, ML_DK), lambda b, h: (rb0 + b, h)),
                pl.BlockSpec((seq_len, ML_DK), lambda b, h: (rb0 + b, kq + h)),
                pl.BlockSpec((seq_len, ML_DV), lambda b, h: (rb0 + b, v0 + h)),
                pl.BlockSpec((seq_len, ML_DV), lambda b, h: (rb0 + b, o0 + h)),
                pl.BlockSpec((seq_len, LANES), lambda b, h: (rb0 + b, 0)),
                pl.BlockSpec((ML_GATE_COLS, seq_len), lambda b, h: (0, rb0 + b)),
                pl.BlockSpec((1, LANES), lambda b, h: (0, 0)),
                pl.BlockSpec((ML_GATE_COLS, 1), lambda b, h: (0, 0)),
                pl.BlockSpec((1, ML_DV), lambda b, h: (0, h))]
    args = [proj, proj, proj, proj, gates, gates_t, bias, b_gates.reshape(ML_GATE_COLS, 1),
            norm_g.reshape(1, ML_HEADS * ML_DV)]
    c_spec = pl.BlockSpec((None, 2, None, ML_DK, ML_DV), lambda b, h: (b, 0, h, 0, 0))
    n_spec = pl.BlockSpec((None, 2, None, 1, ML_DK), lambda b, h: (b, 0, h, 0, 0))
    m_spec = pl.BlockSpec((None, 2, None, 1, 1), lambda b, h: (b, 0, h, 0, 0))
    if not zero_init:
        c0, n0, m0 = state
        in_specs += [c_spec, n_spec, m_spec]
        args += [c0, n0.reshape(n_seq, 2, ML_HEADS, 1, ML_DK), m0.reshape(n_seq, 2, ML_HEADS, 1, 1)]
    out_specs = [pl.BlockSpec((seq_len, ML_DV), lambda b, h: (b, h))]
    out_shape = [jax.ShapeDtypeStruct((n_seq * seq_len, ML_HEADS * ML_DV), BF16)]
    if with_state:
        out_specs += [c_spec, n_spec, m_spec]
        out_shape += [jax.ShapeDtypeStruct((n_seq, 2, ML_HEADS, ML_DK, ML_DV), F32),
                      jax.ShapeDtypeStruct((n_seq, 2, ML_HEADS, 1, ML_DK), F32),
                      jax.ShapeDtypeStruct((n_seq, 2, ML_HEADS, 1, 1), F32)]
    res = pl.pallas_call(
        functools.partial(_mlstm_kernel, seq_len=seq_len, zero_init=zero_init, with_state=with_state),
        grid=(n_seq, ML_HEADS),
        in_specs=in_specs,
        out_specs=out_specs,
        out_shape=out_shape,
        scratch_shapes=[pltpu.VMEM((2, seq_len, ML_DV), F32), pltpu.VMEM((2, ML_DK, ML_DV), F32),
                        pltpu.VMEM((2, 1, ML_DK), F32),
                        pltpu.VMEM((2 * n_chunks, ML_CHUNK, ML_CHUNK), F32),
                        pltpu.VMEM((n_chunks, ML_CHUNK, ML_CHUNK), F32)]
                       + [pltpu.VMEM((2 * n_chunks, ML_CHUNK, 1), F32)] * 3
                       + [pltpu.VMEM((2 * n_chunks, 1, 1), F32)] * 2,
        compiler_params=_cparams(("parallel", "arbitrary")),
        name="mlstm_mix",
    )(*args)
    if not with_state:
        return res[0]
    y, c_fin, n_fin, m_fin = res
    return y, c_fin, n_fin.reshape(n_seq, 2, ML_HEADS, ML_DK), m_fin.reshape(n_seq, 2, ML_HEADS)


def kernel(x_prompt, x_sample, cache_na_k, cache_na_v, state_lru, state_mlstm_C, state_mlstm_n, state_mlstm_m, cache_gqa_k, cache_gqa_v, c, c_ctx, ada_w, ada_b, norm_g, mlp_up, mlp_down, na_w_qkv, na_rpb, na_w_o, lru_w_in, lru_conv_w, lru_conv_b, lru_w_gates, lru_b_gates, lru_lambda, lru_w_o, ml_w_in, ml_w_gates, ml_b_gates, ml_norm_g, ml_w_o, gqa_w_qkv, gqa_q_norm, gqa_k_norm, gqa_w_o):
    x = (x_prompt.reshape(N_CTX, D_MODEL), x_sample.reshape(N_LAT, D_MODEL))
    cond = jnp.concatenate([c_ctx[None], c, jnp.zeros((MOD_ROWS - 1 - DEC_BATCH, D_MODEL), F32)], axis=0)
    mods = adaln_all(cond, ada_w, ada_b)
    gains = norm_g.reshape(DEPTH, 4, 1, D_MODEL)

    outs = {}
    h = norm_mod(x, gains, mods, 0)
    for i in range(DEPTH):
        kind = i % 4
        if kind == 0:
            qkv = matmul(h, na_w_qkv[0], name="na_qkv")
            shp = (BATCH, 1, SEQ, NA_HEADS, HEAD_DIM)
            yp, kp, vp = na_context_attn(qkv)
            outs['na_k'], outs['na_v'] = kp.reshape(shp), vp.reshape(shp)
            yl = na_latent_attn(qkv, cache_na_k[:, 0].reshape(DEC_BATCH, PAST_LEN, D_MODEL),
                                cache_na_v[:, 0].reshape(DEC_BATCH, PAST_LEN, D_MODEL), na_rpb[0])
            y = matmul(yp, na_w_o[0], a2=yl, name="na_out")
        elif kind == 1:
            proj = matmul(h, lru_w_in[0], name="lru_in")
            pre, fin = rglru_mix(proj, state_lru[:, 0], lru_conv_w[0], lru_conv_b[0], lru_w_gates[0],
                                 lru_b_gates[0], lru_lambda[0])
            outs['lru'] = fin[:, None]
            y = matmul(pre, lru_w_o[0], name="lru_out")
        elif kind == 2:
            proj = matmul(h, ml_w_in[0], name="ml_in")
            wg = jnp.pad(ml_w_gates[0], ((0, 0), (0, LANES - ML_GATE_COLS)))
            gates = matmul(h, wg, tn=LANES, name="ml_gates")
            gates_t = gates[:, :ML_GATE_COLS].T
            yp, cp, np_, mp = mlstm_mix(proj, gates, gates_t, ml_b_gates[0], ml_norm_g[0], row0=0,
                                        n_seq=BATCH, seq_len=SEQ, with_state=True)
            yl = mlstm_mix(proj, gates, gates_t, ml_b_gates[0], ml_norm_g[0], row0=N_CTX,
                           n_seq=DEC_BATCH, seq_len=DEC_SEQ, with_state=False,
                           state=(state_mlstm_C[:, 0], state_mlstm_n[:, 0], state_mlstm_m[:, 0]))
            outs['mc'], outs['mn'], outs['mm'] = cp[:, None], np_[:, None], mp[:, None]
            y = matmul(yp, ml_w_o[0], a2=yl, name="ml_out")
        else:
            qkv = matmul(h, gqa_w_qkv[0], name="gqa_qkv")
            kv_cols = GQA_KV_HEADS * HEAD_DIM
            yp, kp, vp = gqa_context_attn(qkv, gqa_q_norm[0], gqa_k_norm[0])
            yl = gqa_latent_attn(qkv, cache_gqa_k[:, 0].reshape(DEC_BATCH, PAST_LEN, kv_cols),
                                 cache_gqa_v[:, 0].reshape(DEC_BATCH, PAST_LEN, kv_cols),
                                 gqa_q_norm[0], gqa_k_norm[0])
            shp = (BATCH, 1, SEQ, GQA_KV_HEADS, HEAD_DIM)
            outs['gk'], outs['gv'] = kp.reshape(shp), vp.reshape(shp)
            y = matmul(yp, gqa_w_o[0], a2=yl, name="gqa_out")

        x, h2 = resid_norm_mod(x, y, gains, mods, layer=i, ga=1, gate=2, nxt=(i, 2, 3, 4))
        u = matmul(h2, mlp_up, layer=i, out_dtype=BF16, act="relu2", name="mlp_up")
        z = matmul(u, mlp_down, layer=i, tn=1024, tk=2048, name="mlp_down")
        last = i + 1 == DEPTH
        nxt = None if last else (i + 1, 0, 0, 1)
        x, h = resid_norm_mod(x, z, gains, mods, layer=i, ga=3, gate=5, nxt=nxt, split_out=last)

    return (x[0].reshape(BATCH, SEQ, D_MODEL), x[1].reshape(DEC_BATCH, DEC_SEQ, D_MODEL),
            outs['na_k'], outs['na_v'], outs['lru'], outs['mc'], outs['mn'], outs['mm'], outs['gk'], outs['gv'])
```

```python
import functools

import jax
import jax.numpy as jnp
from jax import lax
from jax.experimental import pallas as pl
from jax.experimental.pallas import tpu as pltpu

D_MODEL = 4096
BATCH = 16
SEQ = 256
DEPTH = 4
DEC_BATCH = 2
DEC_SEQ = 1024
PAST_LEN = 512
GRID_W = 64
HEAD_DIM = 128
NA_HEADS = 32
WIN_R = 8
WIN_C = 16
GQA_HEADS = 32
GQA_KV_HEADS = 8
GQA_GROUP = GQA_HEADS // GQA_KV_HEADS
ROPE_THETA = 10000.0
D_RNN = D_MODEL
LRU_BLOCKS = 16
LRU_BW = 256
CONV_W = 4
LRU_C = 8.0
ML_HEADS = 8
ML_DK = 256
ML_DV = 512
ML_CHUNK = 128
D_FF = 4 * D_MODEL
EPS = 1e-6

N_CTX = BATCH * SEQ
N_LAT = DEC_BATCH * DEC_SEQ
N_TOK = N_CTX + N_LAT
MOD_ROWS = 8
LANES = 128
NEG_BIG = -1e30

VMEM_LIMIT = 56 * 1024 * 1024

BF16 = jnp.bfloat16
F32 = jnp.float32


def _cparams(sem):
    return pltpu.CompilerParams(dimension_semantics=sem, vmem_limit_bytes=VMEM_LIMIT)


def _dot(a, b):
    return jnp.dot(a, b, preferred_element_type=F32)


def _dot_nt(a, b):
    return lax.dot_general(a, b, (((1,), (1,)), ((), ())), preferred_element_type=F32)


def _dot_tn(a, b):
    return lax.dot_general(a, b, (((0,), (0,)), ((), ())), preferred_element_type=F32)


def _sigmoid(x):
    return 0.5 * (jnp.tanh(0.5 * x) + 1.0)


def _softplus(x):
    return jnp.maximum(x, 0.0) + jnp.log1p(jnp.exp(-jnp.abs(x)))


def _mm_kernel(*refs, act, n_first, n_split):
    w_ref, o_ref = refs[-2:]
    slab = w_ref.shape[-1] // n_split

    def tile(a_ref):
        a = a_ref[...]
        for c in range(n_split):
            cols = slice(c * slab, (c + 1) * slab)
            acc = _dot(a, w_ref[:, cols].astype(BF16))
            if act == "relu2":
                r = jnp.maximum(acc, 0.0)
                acc = r * r
            o_ref[:, cols] = acc.astype(o_ref.dtype)

    if len(refs) == 3:
        tile(refs[0])
    else:
        i = pl.program_id(0)
        pl.when(i < n_first)(lambda: tile(refs[0]))
        pl.when(i >= n_first)(lambda: tile(refs[1]))


def _mm_kernel_kgrid(a_ref, w_ref, o_ref):
    @pl.when(pl.program_id(2) == 0)
    def _():
        o_ref[...] = jnp.zeros(o_ref.shape, o_ref.dtype)

    o_ref[...] += _dot(a_ref[...], w_ref[...].astype(BF16))


def matmul(a, w, *, layer=0, a2=None, out_dtype=F32, act=None, tm=1024, tn=512, tk=4096, n_split=1,
           a_buffers=2, name="matmul"):
    m1, k = a.shape
    m = m1 + (0 if a2 is None else a2.shape[0])
    k2, n = w.shape[-2:]
    assert k == k2 and a.dtype == BF16
    lead = () if w.ndim == 2 else (None,)
    at = () if w.ndim == 2 else (layer,)
    tm, tn, tk = min(tm, m), min(tn, n), min(tk, k)
    assert m1 % tm == 0 and m % tm == 0 and n % tn == 0 and k % tk == 0
    nk = k // tk
    n_first = m1 // tm
    if nk == 1:
        a_mode = {} if a_buffers == 2 else {"pipeline_mode": pl.Buffered(a_buffers)}
        a_specs = [pl.BlockSpec((tm, k), lambda i, j: (jnp.minimum(i, n_first - 1), 0), **a_mode)]
        a_args = [a]
        if a2 is not None:
            assert a2.dtype == BF16 and a2.shape[1] == k
            a_specs.append(pl.BlockSpec((tm, k), lambda i, j: (jnp.maximum(i - n_first, 0), 0), **a_mode))
            a_args.append(a2)
        return pl.pallas_call(
            functools.partial(_mm_kernel, act=act, n_first=n_first, n_split=n_split),
            grid=(m // tm, n // tn),
            in_specs=a_specs + [pl.BlockSpec(lead + (k, tn), lambda i, j: at + (0, j))],
            out_specs=pl.BlockSpec((tm, tn), lambda i, j: (i, j)),
            out_shape=jax.ShapeDtypeStruct((m, n), out_dtype),
            compiler_params=_cparams(("parallel", "arbitrary")),
            name=name,
        )(*a_args, w)
    assert act is None and a2 is None and out_dtype == F32
    return pl.pallas_call(
        _mm_kernel_kgrid,
        grid=(m // tm, n // tn, nk),
        in_specs=[pl.BlockSpec((tm, tk), lambda i, j, kk: (i, kk)),
                  pl.BlockSpec(lead + (tk, tn), lambda i, j, kk: at + (kk, j))],
        out_specs=pl.BlockSpec((tm, tn), lambda i, j, kk: (i, j)),
        out_shape=jax.ShapeDtypeStruct((m, n), out_dtype),
        compiler_params=_cparams(("parallel", "arbitrary", "arbitrary")),
        name=name,
    )(a, w)


WIDE = dict(tn=1024, n_split=2, a_buffers=1)


def _ada_kernel(c_ref, w_ref, b_ref, o_ref):
    c = c_ref[...]
    a = (c * jax.nn.sigmoid(c)).astype(BF16)
    o_ref[...] = _dot(a, w_ref[...].astype(BF16)) + b_ref[...]


def adaln_all(cond, ada_w, ada_b, *, tn=512):
    n = 6 * D_MODEL
    out = pl.pallas_call(
        _ada_kernel,
        grid=(DEPTH, n // tn),
        in_specs=[pl.BlockSpec((MOD_ROWS, D_MODEL), lambda l, j: (0, 0)),
                  pl.BlockSpec((None, D_MODEL, tn), lambda l, j: (l, 0, j)),
                  pl.BlockSpec((None, 1, tn), lambda l, j: (l, 0, j))],
        out_specs=pl.BlockSpec((None, MOD_ROWS, tn), lambda l, j: (l, 0, j)),
        out_shape=jax.ShapeDtypeStruct((DEPTH, MOD_ROWS, n), F32),
        compiler_params=_cparams(("parallel", "arbitrary")),
        name="adaln",
    )(cond, ada_w, ada_b.reshape(DEPTH, 1, n))
    return out.reshape(DEPTH, MOD_ROWS, 6, 1, D_MODEL)


ROW_TILE = 256


def _row_group(i):
    return jnp.maximum(0, (i * ROW_TILE - N_CTX) // DEC_SEQ + 1)


def _rms(x, g):
    return x * lax.rsqrt(jnp.mean(x * x, axis=-1, keepdims=True) + EPS) * g


def _mod_spec(layer, which):
    return pl.BlockSpec((None, None, None, 1, D_MODEL),
                        lambda i: (layer, _row_group(i), which, 0, 0))


def _gain_spec(layer, which):
    return pl.BlockSpec((None, None, 1, D_MODEL), lambda i: (layer, which, 0, 0))


_ROWS_SPEC = pl.BlockSpec((ROW_TILE, D_MODEL), lambda i: (i, 0))


CTX_TILES = N_CTX // ROW_TILE
_CTX_ROWS_SPEC = pl.BlockSpec((ROW_TILE, D_MODEL), lambda i: (jnp.minimum(i, CTX_TILES - 1), 0))
_LAT_ROWS_SPEC = pl.BlockSpec((ROW_TILE, D_MODEL), lambda i: (jnp.maximum(i - CTX_TILES, 0), 0))


def _x_specs(x):
    return [_CTX_ROWS_SPEC, _LAT_ROWS_SPEC] if isinstance(x, tuple) else [_ROWS_SPEC]


def _x_args(x):
    return list(x) if isinstance(x, tuple) else [x]


def _load_rows(refs):
    if len(refs) == 1:
        return refs[0][...]
    return jnp.where(pl.program_id(0) < CTX_TILES, refs[0][...], refs[1][...])


def _norm_mod_kernel(*refs):
    g_ref, sh_ref, sc_ref, h_ref = refs[-4:]
    h = _rms(_load_rows(refs[:-4]), g_ref[...]) * (1.0 + sc_ref[...]) + sh_ref[...]
    h_ref[...] = h.astype(h_ref.dtype)


def norm_mod(x, gains, mods, layer):
    return pl.pallas_call(
        _norm_mod_kernel,
        grid=(N_TOK // ROW_TILE,),
        in_specs=_x_specs(x) + [_gain_spec(layer, 0), _mod_spec(layer, 0), _mod_spec(layer, 1)],
        out_specs=_ROWS_SPEC,
        out_shape=jax.ShapeDtypeStruct((N_TOK, D_MODEL), BF16),
        compiler_params=_cparams(("parallel",)),
        name="norm_mod",
    )(*_x_args(x), gains, mods, mods)


def _resid_kernel(*refs, n_x, with_h, split_out):
    x_refs, (y_ref, ga_ref, gate_ref), rest = refs[:n_x], refs[n_x:n_x + 3], refs[n_x + 3:]
    xn = _load_rows(x_refs) + gate_ref[...] * _rms(y_ref[...], ga_ref[...])
    if with_h:
        gb_ref, sh_ref, sc_ref = rest[:3]
        rest = rest[3:]
        h_ref = rest[-1]
        h = _rms(xn, gb_ref[...]) * (1.0 + sc_ref[...]) + sh_ref[...]
        h_ref[...] = h.astype(h_ref.dtype)
    if split_out:
        i = pl.program_id(0)

        @pl.when(i < CTX_TILES)
        def _():
            rest[0][...] = xn

        @pl.when(i >= CTX_TILES)
        def _():
            rest[1][...] = xn
    else:
        rest[0][...] = xn


def resid_norm_mod(x, y, gains, mods, *, layer, ga, gate, nxt, split_out=False):
    in_specs = _x_specs(x) + [_ROWS_SPEC, _gain_spec(layer, ga), _mod_spec(layer, gate)]
    args = _x_args(x) + [y, gains, mods]
    if split_out:
        out_specs = [_CTX_ROWS_SPEC, _LAT_ROWS_SPEC]
        out_shape = [jax.ShapeDtypeStruct((N_CTX, D_MODEL), F32), jax.ShapeDtypeStruct((N_LAT, D_MODEL), F32)]
    else:
        out_specs = [_ROWS_SPEC]
        out_shape = [jax.ShapeDtypeStruct((N_TOK, D_MODEL), F32)]
    if nxt is not None:
        nl, ng, nsh, nsc = nxt
        in_specs += [_gain_spec(nl, ng), _mod_spec(nl, nsh), _mod_spec(nl, nsc)]
        args += [gains, mods, mods]
        out_specs.append(_ROWS_SPEC)
        out_shape.append(jax.ShapeDtypeStruct((N_TOK, D_MODEL), BF16))
    res = pl.pallas_call(
        functools.partial(_resid_kernel, n_x=len(_x_args(x)), with_h=nxt is not None, split_out=split_out),
        grid=(N_TOK // ROW_TILE,),
        in_specs=in_specs,
        out_specs=out_specs,
        out_shape=out_shape,
        compiler_params=_cparams(("arbitrary",) if split_out else ("parallel",)),
        name="resid_norm_mod",
    )(*args)
    x_new = (res[0], res[1]) if split_out else res[0]
    return x_new, (res[-1] if nxt is not None else None)


def _head_rms(x, g):
    return x * lax.rsqrt(jnp.mean(x * x, axis=-1, keepdims=True) + EPS) * g


def _rope(x, cos, sin):
    lane = lax.broadcasted_iota(jnp.int32, x.shape, 1)
    partner = jnp.where((lane & 63) < 32, pltpu.roll(x, LANES - 32, 1), pltpu.roll(x, 32, 1))
    return x * cos + partner * sin


def _rope_tables():
    t = jnp.arange(DEC_SEQ)
    half = HEAD_DIM // 2
    inv_freq = 1.0 / (ROPE_THETA ** (jnp.arange(0, half, 2, dtype=F32) / half))
    ang_r = (t // GRID_W).astype(F32)[:, None] * inv_freq
    ang_c = (t % GRID_W).astype(F32)[:, None] * inv_freq
    cos = jnp.concatenate([jnp.cos(ang_r)] * 2 + [jnp.cos(ang_c)] * 2, axis=-1)
    sin = jnp.concatenate([-jnp.sin(ang_r), jnp.sin(ang_r), -jnp.sin(ang_c), jnp.sin(ang_c)], axis=-1)
    return cos, sin


def _softmax_pv(scores, values):
    m = functools.reduce(jnp.maximum, [jnp.max(s, axis=-1, keepdims=True) for s in scores])
    es = [jnp.exp(s - m) for s in scores]
    l = functools.reduce(jnp.add, [jnp.sum(e, axis=-1, keepdims=True) for e in es])
    o = functools.reduce(jnp.add, [_dot(e.astype(BF16), v) for e, v in zip(es, values)])
    return o / l


def _ctx_attn_kernel(q_ref, k_ref, v_ref, *rest, n_kv, group, normed):
    if normed:
        qg_ref, kg_ref, o_ref, ko_ref, vo_ref = rest
    else:
        o_ref, ko_ref, vo_ref = rest
    scale = HEAD_DIM ** -0.5
    for j in range(n_kv):
        ks = slice(j * HEAD_DIM, (j + 1) * HEAD_DIM)
        k = k_ref[:, ks]
        v = v_ref[:, ks]
        if normed:
            k = _head_rms(k, kg_ref[...])
        ko_ref[:, ks] = k
        vo_ref[:, ks] = v
        kb = k.astype(BF16)
        vb = v.astype(BF16)
        for g in range(group):
            qs = slice((j * group + g) * HEAD_DIM, (j * group + g + 1) * HEAD_DIM)
            q = q_ref[:, qs]
            if normed:
                q = _head_rms(q, qg_ref[...])
            s = _dot_nt(q.astype(BF16), kb) * scale
            o_ref[:, qs] = _softmax_pv([s], [vb]).astype(o_ref.dtype)


CTX_COLS = 1024


def na_context_attn(qkv):
    nb = D_MODEL // CTX_COLS
    blk = pl.BlockSpec((SEQ, CTX_COLS), lambda b, j: (b, j))
    kv_shape = jax.ShapeDtypeStruct((N_CTX, D_MODEL), F32)
    return pl.pallas_call(
        functools.partial(_ctx_attn_kernel, n_kv=CTX_COLS // HEAD_DIM, group=1, normed=False),
        grid=(BATCH, nb),
        in_specs=[blk,
                  pl.BlockSpec((SEQ, CTX_COLS), lambda b, j: (b, nb + j)),
                  pl.BlockSpec((SEQ, CTX_COLS), lambda b, j: (b, 2 * nb + j))],
        out_specs=[blk, blk, blk],
        out_shape=[jax.ShapeDtypeStruct((N_CTX, D_MODEL), BF16), kv_shape, kv_shape],
        compiler_params=_cparams(("parallel", "arbitrary")),
        name="na_ctx_attn",
    )(qkv, qkv, qkv)


def gqa_context_attn(qkv, q_gain, k_gain):
    n_kv = CTX_COLS // (GQA_GROUP * HEAD_DIM)
    kv_cols = n_kv * HEAD_DIM
    nb = D_MODEL // CTX_COLS
    k0 = D_MODEL // kv_cols
    v0 = k0 + GQA_KV_HEADS * HEAD_DIM // kv_cols
    kv_shape = jax.ShapeDtypeStruct((N_CTX, GQA_KV_HEADS * HEAD_DIM), F32)
    gain_spec = pl.BlockSpec((1, HEAD_DIM), lambda b, j: (0, 0))
    return pl.pallas_call(
        functools.partial(_ctx_attn_kernel, n_kv=n_kv, group=GQA_GROUP, normed=True),
        grid=(BATCH, nb),
        in_specs=[pl.BlockSpec((SEQ, CTX_COLS), lambda b, j: (b, j)),
                  pl.BlockSpec((SEQ, kv_cols), lambda b, j: (b, k0 + j)),
                  pl.BlockSpec((SEQ, kv_cols), lambda b, j: (b, v0 + j)),
                  gain_spec, gain_spec],
        out_specs=[pl.BlockSpec((SEQ, CTX_COLS), lambda b, j: (b, j)),
                   pl.BlockSpec((SEQ, kv_cols), lambda b, j: (b, j)),
                   pl.BlockSpec((SEQ, kv_cols), lambda b, j: (b, j))],
        out_shape=[jax.ShapeDtypeStruct((N_CTX, D_MODEL), BF16), kv_shape, kv_shape],
        compiler_params=_cparams(("parallel", "arbitrary")),
        name="gqa_ctx_attn",
    )(qkv, qkv, qkv, q_gain.reshape(1, HEAD_DIM), k_gain.reshape(1, HEAD_DIM))


def _gqa_lat_kernel(q_ref, k_ref, v_ref, kc_ref, vc_ref, cos_ref, sin_ref, qg_ref, kg_ref, o_ref):
    scale = HEAD_DIM ** -0.5
    cos, sin = cos_ref[...], sin_ref[...]
    kc = kc_ref[...].astype(BF16)
    vc = vc_ref[...].astype(BF16)
    k = _rope(_head_rms(k_ref[...], kg_ref[...]), cos, sin).astype(BF16)
    v = v_ref[...].astype(BF16)
    for g in range(GQA_GROUP):
        qs = slice(g * HEAD_DIM, (g + 1) * HEAD_DIM)
        q = _rope(_head_rms(q_ref[:, qs], qg_ref[...]), cos, sin).astype(BF16)
        s_ctx = _dot_nt(q, kc) * scale
        s_lat = _dot_nt(q, k) * scale
        o_ref[:, qs] = _softmax_pv([s_ctx, s_lat], [vc, v]).astype(o_ref.dtype)


def gqa_latent_attn(qkv, cache_k, cache_v, q_gain, k_gain):
    cos, sin = _rope_tables()
    qcols = GQA_GROUP * HEAD_DIM
    rb0 = N_CTX // DEC_SEQ
    k0 = D_MODEL // HEAD_DIM
    v0 = k0 + GQA_KV_HEADS
    cache_spec = pl.BlockSpec((None, PAST_LEN, HEAD_DIM), lambda b, j: (b, 0, j))
    table_spec = pl.BlockSpec((DEC_SEQ, HEAD_DIM), lambda b, j: (0, 0))
    gain_spec = pl.BlockSpec((1, HEAD_DIM), lambda b, j: (0, 0))
    return pl.pallas_call(
        _gqa_lat_kernel,
        grid=(DEC_BATCH, GQA_KV_HEADS),
        in_specs=[pl.BlockSpec((DEC_SEQ, qcols), lambda b, j: (rb0 + b, j)),
                  pl.BlockSpec((DEC_SEQ, HEAD_DIM), lambda b, j: (rb0 + b, k0 + j)),
                  pl.BlockSpec((DEC_SEQ, HEAD_DIM), lambda b, j: (rb0 + b, v0 + j)),
                  cache_spec, cache_spec, table_spec, table_spec, gain_spec, gain_spec],
        out_specs=pl.BlockSpec((DEC_SEQ, qcols), lambda b, j: (b, j)),
        out_shape=jax.ShapeDtypeStruct((N_LAT, D_MODEL), BF16),
        compiler_params=_cparams(("parallel", "arbitrary")),
        name="gqa_lat_attn",
    )(qkv, qkv, qkv, cache_k, cache_v, cos, sin, q_gain.reshape(1, HEAD_DIM), k_gain.reshape(1, HEAD_DIM))


NA_ROWS = DEC_SEQ // GRID_W
NA_WR = min(WIN_R, NA_ROWS)
NA_WKEYS = NA_WR * GRID_W


NA_REL_ROWS = 2 * WIN_R - 1
NA_ROW_PAIRS = NA_WR // 2


def _na_bias_table(rpb):
    col = jnp.arange(GRID_W)
    cs = jnp.clip(col - WIN_C // 2, 0, GRID_W - WIN_C)
    col_ok = (col[None, :] >= cs[:, None]) & (col[None, :] < cs[:, None] + WIN_C)
    dc = jnp.clip(col[None, :] - col[:, None] + WIN_C - 1, 0, 2 * WIN_C - 2)
    blocks = jnp.where(col_ok[None, None], rpb[:, :, dc], NEG_BIG).astype(F32)
    return jnp.concatenate([blocks[:, :-1], blocks[:, 1:]], axis=-1)


def _na_lat_kernel(q_ref, k_ref, v_ref, kc_ref, vc_ref, bias_ref, o_ref,
                   kb_ref, vb_ref, slat_ref, elat_ref, olat_ref):
    scale = HEAD_DIM ** -0.5
    kb_ref[...] = k_ref[...].astype(BF16)
    vb_ref[...] = v_ref[...].astype(BF16)

    def window(r):
        rs = min(max(r - NA_WR // 2, 0), NA_ROWS - NA_WR)
        return (slice(r * GRID_W, (r + 1) * GRID_W), slice(rs * GRID_W, rs * GRID_W + NA_WKEYS),
                rs - r + WIN_R - 1)

    for r in range(NA_ROWS):
        q_rows, k_rows, rel0 = window(r)
        bias = jnp.concatenate([bias_ref[rel0 + 2 * p] for p in range(NA_ROW_PAIRS)], axis=1)
        slat_ref[q_rows, :] = _dot_nt(q_ref[q_rows, :].astype(BF16), kb_ref[k_rows, :]) * scale + bias
    s_lat = slat_ref[...]
    s_ctx = _dot_nt(q_ref[...].astype(BF16), kc_ref[...].astype(BF16)) * scale
    m = jnp.maximum(jnp.max(s_lat, axis=-1, keepdims=True), jnp.max(s_ctx, axis=-1, keepdims=True))
    e_lat = jnp.exp(s_lat - m)
    e_ctx = jnp.exp(s_ctx - m)
    l = jnp.sum(e_lat, axis=-1, keepdims=True) + jnp.sum(e_ctx, axis=-1, keepdims=True)
    elat_ref[...] = e_lat.astype(BF16)
    o_ctx = _dot(e_ctx.astype(BF16), vc_ref[...].astype(BF16))
    for r in range(NA_ROWS):
        q_rows, k_rows, _ = window(r)
        olat_ref[q_rows, :] = _dot(elat_ref[q_rows, :], vb_ref[k_rows, :])
    o_ref[...] = ((olat_ref[...] + o_ctx) / l).astype(o_ref.dtype)


def na_latent_attn(qkv, cache_k, cache_v, rpb):
    bias = _na_bias_table(rpb)
    rb0 = N_CTX // DEC_SEQ
    cache_spec = pl.BlockSpec((None, PAST_LEN, HEAD_DIM), lambda h, b: (b, 0, h))
    return pl.pallas_call(
        _na_lat_kernel,
        grid=(NA_HEADS, DEC_BATCH),
        in_specs=[pl.BlockSpec((DEC_SEQ, HEAD_DIM), lambda h, b: (rb0 + b, h)),
                  pl.BlockSpec((DEC_SEQ, HEAD_DIM), lambda h, b: (rb0 + b, NA_HEADS + h)),
                  pl.BlockSpec((DEC_SEQ, HEAD_DIM), lambda h, b: (rb0 + b, 2 * NA_HEADS + h)),
                  cache_spec, cache_spec,
                  pl.BlockSpec((None, NA_REL_ROWS - 1, GRID_W, 2 * GRID_W), lambda h, b: (h, 0, 0, 0))],
        out_specs=pl.BlockSpec((DEC_SEQ, HEAD_DIM), lambda h, b: (b, h)),
        out_shape=jax.ShapeDtypeStruct((N_LAT, D_MODEL), BF16),
        scratch_shapes=[pltpu.VMEM((DEC_SEQ, HEAD_DIM), BF16), pltpu.VMEM((DEC_SEQ, HEAD_DIM), BF16),
                        pltpu.VMEM((DEC_SEQ, NA_WKEYS), F32), pltpu.VMEM((DEC_SEQ, NA_WKEYS), BF16),
                        pltpu.VMEM((DEC_SEQ, HEAD_DIM), F32)],
        compiler_params=_cparams(("parallel", "arbitrary")),
        name="na_lat_attn",
    )(qkv, qkv, qkv, cache_k, cache_v, bias)


LRU_ROWS = 2048
LRU_CTX_STEPS = N_CTX // LRU_ROWS
LRU_SEQ_PAD = 8
LRU_PITCH_PAD = 4
LRU_SCAN_ROWS = max(LRU_ROWS // SEQ * (SEQ + LRU_PITCH_PAD), LRU_ROWS // DEC_SEQ * (DEC_SEQ + LRU_PITCH_PAD))


def _lru_body(xr_ref, gb_ref, cw_ref, cb_ref, wg_ref, bg_ref, lam_ref, h0_ref, y_ref, fin_ref,
              af_ref, uf_ref, ab_ref, ub_ref, hf_ref, hb_ref, *, n_seq, seq_len):
    rows = n_seq * seq_len
    halves = [slice(c * LANES, (c + 1) * LANES) for c in range(LRU_BW // LANES)]
    pitch = seq_len + LRU_PITCH_PAD
    seqs = [(slice(s * seq_len, (s + 1) * seq_len), slice(s * pitch, s * pitch + seq_len))
            for s in range(n_seq)]
    x = xr_ref[...]
    t = lax.broadcasted_iota(jnp.int32, (rows, LRU_BW), 0) & (seq_len - 1)
    cw = cw_ref[...]
    xf = (jnp.where(t >= 2, pltpu.roll(x, 2, 0), 0.0) * cw[0:1]
          + jnp.where(t >= 1, pltpu.roll(x, 1, 0), 0.0) * cw[1:2]
          + x * cw[2:3]
          + jnp.where(t < seq_len - 1, pltpu.roll(x, rows - 1, 0), 0.0) * cw[3:4]) + cb_ref[...]
    xb = xf.astype(BF16)
    sp = _softplus(-lam_ref[...])
    for d, (a_ref, u_ref) in enumerate(((af_ref, uf_ref), (ab_ref, ub_ref))):
        bg = bg_ref[d]
        r_gate = _sigmoid(_dot(xb, wg_ref[d, 0].astype(BF16)) + bg[0:1])
        i_gate = _sigmoid(_dot(xb, wg_ref[d, 1].astype(BF16)) + bg[1:2])
        a = jnp.exp(-LRU_C * r_gate * sp[d:d + 1])
        u = jnp.sqrt(1.0 - a * a) * i_gate * xf
        for c, cols in enumerate(halves):
            for src, dst in seqs:
                a_ref[c, dst, :] = a[src, cols]
                u_ref[c, dst, :] = u[src, cols]

    def step(i, carry):
        rf = pl.ds(i, n_seq, stride=pitch)
        rb = pl.ds(seq_len - 1 - i, n_seq, stride=pitch)
        out = []
        for c in range(len(halves)):
            hf = af_ref[c, rf, :] * carry[2 * c] + uf_ref[c, rf, :]
            hf_ref[c, rf, :] = hf
            hb = ab_ref[c, rb, :] * carry[2 * c + 1] + ub_ref[c, rb, :]
            hb_ref[c, rb, :] = hb
            out += [hf, hb]
        return tuple(out)

    init = tuple(h0_ref[d, 0:n_seq, cols] for cols in halves for d in range(2))
    fin = lax.fori_loop(0, seq_len, step, init, unroll=4)
    fin_ref[...] = jnp.zeros(fin_ref.shape, F32)
    gate = jax.nn.gelu(gb_ref[...])
    for c, cols in enumerate(halves):
        fin_ref[0, 0:n_seq, cols] = fin[2 * c]
        fin_ref[1, 0:n_seq, cols] = fin[2 * c + 1]
        for src, dst in seqs:
            h = hf_ref[c, dst, :] + hb_ref[c, dst, :]
            y_ref[src, cols] = (h * gate[src, cols]).astype(y_ref.dtype)


def _lru_kernel(*refs):
    i = pl.program_id(0)

    @pl.when(i < LRU_CTX_STEPS)
    def _():
        _lru_body(*refs, n_seq=LRU_ROWS // SEQ, seq_len=SEQ)

    @pl.when(i >= LRU_CTX_STEPS)
    def _():
        _lru_body(*refs, n_seq=LRU_ROWS // DEC_SEQ, seq_len=DEC_SEQ)


def rglru_mix(proj, state, conv_w, conv_b, w_gates, b_gates, lam):
    assert LRU_ROWS // SEQ == LRU_SEQ_PAD and LRU_ROWS // DEC_SEQ == DEC_BATCH
    n_steps = N_TOK // LRU_ROWS
    h0 = jnp.zeros((2, n_steps * LRU_SEQ_PAD, D_RNN), F32)
    h0 = h0.at[:, BATCH:BATCH + DEC_BATCH].set(state.transpose(1, 0, 2))
    blk = lambda i, n: (i, n)
    chan = lambda i, n: (0, n)
    y, fin = pl.pallas_call(
        _lru_kernel,
        grid=(n_steps, LRU_BLOCKS),
        in_specs=[pl.BlockSpec((LRU_ROWS, LRU_BW), blk),
                  pl.BlockSpec((LRU_ROWS, LRU_BW), lambda i, n: (i, LRU_BLOCKS + n)),
                  pl.BlockSpec((CONV_W, LRU_BW), chan),
                  pl.BlockSpec((1, LRU_BW), chan),
                  pl.BlockSpec((2, 2, None, LRU_BW, LRU_BW), lambda i, n: (0, 0, n, 0, 0)),
                  pl.BlockSpec((2, 2, LRU_BW), lambda i, n: (0, 0, n)),
                  pl.BlockSpec((2, LRU_BW), chan),
                  pl.BlockSpec((2, LRU_SEQ_PAD, LRU_BW), lambda i, n: (0, i, n))],
        out_specs=[pl.BlockSpec((LRU_ROWS, LRU_BW), blk),
                   pl.BlockSpec((2, LRU_SEQ_PAD, LRU_BW), lambda i, n: (0, i, n))],
        out_shape=[jax.ShapeDtypeStruct((N_TOK, D_RNN), BF16),
                   jax.ShapeDtypeStruct((2, n_steps * LRU_SEQ_PAD, D_RNN), F32)],
        scratch_shapes=[pltpu.VMEM((LRU_BW // LANES, LRU_SCAN_ROWS, LANES), F32)] * 6,
        compiler_params=_cparams(("parallel", "arbitrary")),
        name="rglru_mix",
    )(proj, proj, conv_w, conv_b.reshape(1, D_RNN), w_gates, b_gates, lam, h0)
    return y, fin[:, :BATCH].transpose(1, 0, 2)


ML_GATE_COLS = 4 * ML_HEADS


def _pick_lane(x, idx):
    lane = lax.broadcasted_iota(jnp.int32, x.shape, 1)
    return jnp.sum(jnp.where(lane == idx, x, 0.0), axis=1, keepdims=True)


def _pick_row(x, idx):
    row = lax.broadcasted_iota(jnp.int32, x.shape, 0)
    return jnp.sum(jnp.where(row == idx, x, 0.0), axis=0, keepdims=True)


def _mlstm_kernel(q_ref, k_ref, v_ref, og_ref, g_ref, gt_ref, bg_ref, bgt_ref, ng_ref, *rest,
                  seq_len, zero_init, with_state):
    rest = list(rest)
    if not zero_init:
        c0_ref, n0_ref, m0_ref = rest[:3]
        rest = rest[3:]
    y_ref = rest.pop(0)
    if with_state:
        co_ref, no_ref, mo_ref = rest[:3]
        rest = rest[3:]
    (h_ref, c_ref, n_ref, dmat_ref, qk_ref, bcum_ref, rmax_ref, dec_ref, blast_ref, dmax_ref) = rest
    head = pl.program_id(1)
    n_chunks = seq_len // ML_CHUNK
    k_scale = ML_DK ** -0.5
    sub = lax.broadcasted_iota(jnp.int32, (ML_CHUNK, ML_CHUNK), 0)
    lane = lax.broadcasted_iota(jnp.int32, (ML_CHUNK, ML_CHUNK), 1)

    if zero_init:
        c_ref[...] = jnp.zeros(c_ref.shape, F32)
        n_ref[...] = jnp.zeros(n_ref.shape, F32)
        m_init = (jnp.zeros((1, 1), F32),) * 2
    else:
        c_ref[...] = c0_ref[...]
        n_ref[...] = n0_ref[...]
        m_init = (m0_ref[0], m0_ref[1])

    for c in range(n_chunks):
        rows = slice(c * ML_CHUNK, (c + 1) * ML_CHUNK)
        g = g_ref[rows, :] + bg_ref[...]
        gt = gt_ref[:, rows] + bgt_ref[...]
        qk_ref[c] = _dot_nt(q_ref[rows, :].astype(BF16), (k_ref[rows, :] * k_scale).astype(BF16))
        for d in range(2):
            valid = (lane <= sub) if d == 0 else (lane >= sub)
            valid_t = (sub <= lane) if d == 0 else (sub >= lane)
            i_col = d * 2 * ML_HEADS + head
            f_col = i_col + ML_HEADS
            li_col = _pick_lane(g, i_col)
            lf_col = -_softplus(-_pick_lane(g, f_col))
            li_row = _pick_row(gt, i_col)
            lf_row = -_softplus(-_pick_row(gt, f_col))
            bcum_col = jnp.sum(jnp.where(valid, lf_row, 0.0), axis=1, keepdims=True)
            bcum_row = jnp.sum(jnp.where(valid_t, lf_col, 0.0), axis=0, keepdims=True)
            b_last = jnp.sum(lf_row, axis=1, keepdims=True)
            dmat = jnp.where(valid, bcum_col - bcum_row + li_row, NEG_BIG)
            dec = b_last - bcum_col + li_col
            idx = d * n_chunks + c
            dmat_ref[idx] = dmat
            rmax_ref[idx] = jnp.max(dmat, axis=1, keepdims=True)
            bcum_ref[idx] = bcum_col
            dec_ref[idx] = dec
            blast_ref[idx] = b_last
            dmax_ref[idx] = jnp.max(dec, axis=0, keepdims=True)

    def one_chunk(d, c_idx, m_prev):
        idx = d * n_chunks + c_idx
        rows = pl.ds(pl.multiple_of(c_idx * ML_CHUNK, ML_CHUNK), ML_CHUNK)
        m_inter = bcum_ref[idx] + m_prev
        m_t = jnp.maximum(m_inter, rmax_ref[idx])
        qf = q_ref[rows, :]
        qb = qf.astype(BF16)
        kf = k_ref[rows, :] * k_scale
        vb = v_ref[rows, :].astype(BF16)
        s = qk_ref[c_idx] * jnp.exp(dmat_ref[idx] - m_t)
        inter = jnp.exp(m_inter - m_t)
        num = _dot(s.astype(BF16), vb) + inter * _dot(qb, c_ref[d].astype(BF16))
        den = (jnp.sum(s, axis=1, keepdims=True)
               + inter * jnp.sum(qf * n_ref[d], axis=1, keepdims=True))
        h_ref[d, rows, :] = num / jnp.maximum(jnp.abs(den), jnp.exp(-m_t))
        b_last = blast_ref[idx]
        m_new = jnp.maximum(b_last + m_prev, dmax_ref[idx])
        kw = kf * jnp.exp(dec_ref[idx] - m_new)
        carry_scale = jnp.exp(b_last + m_prev - m_new)
        c_ref[d] = carry_scale * c_ref[d] + _dot_tn(kw.astype(BF16), vb)
        n_ref[d] = carry_scale * n_ref[d] + jnp.sum(kw, axis=0, keepdims=True)
        return m_new

    def both(ci, m_prev):
        return one_chunk(0, ci, m_prev[0]), one_chunk(1, n_chunks - 1 - ci, m_prev[1])

    m_fin = lax.fori_loop(0, n_chunks, both, m_init)
    if with_state:
        co_ref[...] = c_ref[...]
        no_ref[...] = n_ref[...]
        mo_ref[0] = m_fin[0]
        mo_ref[1] = m_fin[1]

    hs = h_ref[0] + h_ref[1]
    hn = hs * lax.rsqrt(jnp.mean(hs * hs, axis=-1, keepdims=True) + EPS) * ng_ref[...]
    y_ref[...] = (jax.nn.sigmoid(og_ref[...]) * hn).astype(y_ref.dtype)


def mlstm_mix(proj, gates, gates_t, b_gates, norm_g, *, row0, n_seq, seq_len, state=None, with_state):
    rb0 = row0 // seq_len
    kq = ML_HEADS * ML_DK // ML_DK
    v0 = 2 * ML_HEADS * ML_DK // ML_DV
    o0 = v0 + ML_HEADS
    zero_init = state is None
    n_chunks = seq_len // ML_CHUNK
    bias = jnp.pad(b_gates.reshape(1, ML_GATE_COLS), ((0, 0), (0, LANES - ML_GATE_COLS)))
    in_specs = [pl.BlockSpec((seq_len, ML_DK), lambda b, h: (rb0 + b, h)),
                pl.BlockSpec((seq_len, ML_DK), lambda b, h: (rb0 + b, kq + h)),
                pl.BlockSpec((seq_len, ML_DV), lambda b, h: (rb0 + b, v0 + h)),
                pl.BlockSpec((seq_len, ML_DV), lambda b, h: (rb0 + b, o0 + h)),
                pl.BlockSpec((seq_len, LANES), lambda b, h: (rb0 + b, 0)),
                pl.BlockSpec((ML_GATE_COLS, seq_len), lambda b, h: (0, rb0 + b)),
                pl.BlockSpec((1, LANES), lambda b, h: (0, 0)),
                pl.BlockSpec((ML_GATE_COLS, 1), lambda b, h: (0, 0)),
                pl.BlockSpec((1, ML_DV), lambda b, h: (0, h))]
    args = [proj, proj, proj, proj, gates, gates_t, bias, b_gates.reshape(ML_GATE_COLS, 1),
            norm_g.reshape(1, ML_HEADS * ML_DV)]
    c_spec = pl.BlockSpec((None, 2, None, ML_DK, ML_DV), lambda b, h: (b, 0, h, 0, 0))
    n_spec = pl.BlockSpec((None, 2, None, 1, ML_DK), lambda b, h: (b, 0, h, 0, 0))
    m_spec = pl.BlockSpec((None, 2, None, 1, 1), lambda b, h: (b, 0, h, 0, 0))
    if not zero_init:
        c0, n0, m0 = state
        in_specs += [c_spec, n_spec, m_spec]
        args += [c0, n0.reshape(n_seq, 2, ML_HEADS, 1, ML_DK), m0.reshape(n_seq, 2, ML_HEADS, 1, 1)]
    out_specs = [pl.BlockSpec((seq_len, ML_DV), lambda b, h: (b, h))]
    out_shape = [jax.ShapeDtypeStruct((n_seq * seq_len, ML_HEADS * ML_DV), BF16)]
    if with_state:
        out_specs += [c_spec, n_spec, m_spec]
        out_shape += [jax.ShapeDtypeStruct((n_seq, 2, ML_HEADS, ML_DK, ML_DV), F32),
                      jax.ShapeDtypeStruct((n_seq, 2, ML_HEADS, 1, ML_DK), F32),
                      jax.ShapeDtypeStruct((n_seq, 2, ML_HEADS, 1, 1), F32)]
    res = pl.pallas_call(
        functools.partial(_mlstm_kernel, seq_len=seq_len, zero_init=zero_init, with_state=with_state),
        grid=(n_seq, ML_HEADS),
        in_specs=in_specs,
        out_specs=out_specs,
        out_shape=out_shape,
        scratch_shapes=[pltpu.VMEM((2, seq_len, ML_DV), F32), pltpu.VMEM((2, ML_DK, ML_DV), F32),
                        pltpu.VMEM((2, 1, ML_DK), F32),
                        pltpu.VMEM((2 * n_chunks, ML_CHUNK, ML_CHUNK), F32),
                        pltpu.VMEM((n_chunks, ML_CHUNK, ML_CHUNK), F32)]
                       + [pltpu.VMEM((2 * n_chunks, ML_CHUNK, 1), F32)] * 3
                       + [pltpu.VMEM((2 * n_chunks, 1, 1), F32)] * 2,
        compiler_params=_cparams(("parallel", "arbitrary")),
        name="mlstm_mix",
    )(*args)
    if not with_state:
        return res[0]
    y, c_fin, n_fin, m_fin = res
    return y, c_fin, n_fin.reshape(n_seq, 2, ML_HEADS, ML_DK), m_fin.reshape(n_seq, 2, ML_HEADS)


def kernel(x_prompt, x_sample, cache_na_k, cache_na_v, state_lru, state_mlstm_C, state_mlstm_n, state_mlstm_m, cache_gqa_k, cache_gqa_v, c, c_ctx, ada_w, ada_b, norm_g, mlp_up, mlp_down, na_w_qkv, na_rpb, na_w_o, lru_w_in, lru_conv_w, lru_conv_b, lru_w_gates, lru_b_gates, lru_lambda, lru_w_o, ml_w_in, ml_w_gates, ml_b_gates, ml_norm_g, ml_w_o, gqa_w_qkv, gqa_q_norm, gqa_k_norm, gqa_w_o):
    x = (x_prompt.reshape(N_CTX, D_MODEL), x_sample.reshape(N_LAT, D_MODEL))
    cond = jnp.concatenate([c_ctx[None], c, jnp.zeros((MOD_ROWS - 1 - DEC_BATCH, D_MODEL), F32)], axis=0)
    mods = adaln_all(cond, ada_w, ada_b)
    gains = norm_g.reshape(DEPTH, 4, 1, D_MODEL)

    outs = {}
    h = norm_mod(x, gains, mods, 0)
    for i in range(DEPTH):
        kind = i % 4
        if kind == 0:
            qkv = matmul(h, na_w_qkv[0], name="na_qkv", **WIDE)
            shp = (BATCH, 1, SEQ, NA_HEADS, HEAD_DIM)
            yp, kp, vp = na_context_attn(qkv)
            outs['na_k'], outs['na_v'] = kp.reshape(shp), vp.reshape(shp)
            yl = na_latent_attn(qkv, cache_na_k[:, 0].reshape(DEC_BATCH, PAST_LEN, D_MODEL),
                                cache_na_v[:, 0].reshape(DEC_BATCH, PAST_LEN, D_MODEL), na_rpb[0])
            y = matmul(yp, na_w_o[0], a2=yl, name="na_out")
        elif kind == 1:
            proj = matmul(h, lru_w_in[0], name="lru_in", **WIDE)
            pre, fin = rglru_mix(proj, state_lru[:, 0], lru_conv_w[0], lru_conv_b[0], lru_w_gates[0],
                                 lru_b_gates[0], lru_lambda[0])
            outs['lru'] = fin[:, None]
            y = matmul(pre, lru_w_o[0], name="lru_out")
        elif kind == 2:
            proj = matmul(h, ml_w_in[0], name="ml_in", **WIDE)
            wg = jnp.pad(ml_w_gates[0], ((0, 0), (0, LANES - ML_GATE_COLS)))
            gates = matmul(h, wg, tn=LANES, name="ml_gates")
            gates_t = gates[:, :ML_GATE_COLS].T
            yp, cp, np_, mp = mlstm_mix(proj, gates, gates_t, ml_b_gates[0], ml_norm_g[0], row0=0,
                                        n_seq=BATCH, seq_len=SEQ, with_state=True)
            yl = mlstm_mix(proj, gates, gates_t, ml_b_gates[0], ml_norm_g[0], row0=N_CTX,
                           n_seq=DEC_BATCH, seq_len=DEC_SEQ, with_state=False,
                           state=(state_mlstm_C[:, 0], state_mlstm_n[:, 0], state_mlstm_m[:, 0]))
            outs['mc'], outs['mn'], outs['mm'] = cp[:, None], np_[:, None], mp[:, None]
            y = matmul(yp, ml_w_o[0], a2=yl, name="ml_out")
        else:
            qkv = matmul(h, gqa_w_qkv[0], name="gqa_qkv", **WIDE)
            kv_cols = GQA_KV_HEADS * HEAD_DIM
            yp, kp, vp = gqa_context_attn(qkv, gqa_q_norm[0], gqa_k_norm[0])
            yl = gqa_latent_attn(qkv, cache_gqa_k[:, 0].reshape(DEC_BATCH, PAST_LEN, kv_cols),
                                 cache_gqa_v[:, 0].reshape(DEC_BATCH, PAST_LEN, kv_cols),
                                 gqa_q_norm[0], gqa_k_norm[0])
            shp = (BATCH, 1, SEQ, GQA_KV_HEADS, HEAD_DIM)
            outs['gk'], outs['gv'] = kp.reshape(shp), vp.reshape(shp)
            y = matmul(yp, gqa_w_o[0], a2=yl, name="gqa_out")

        x, h2 = resid_norm_mod(x, y, gains, mods, layer=i, ga=1, gate=2, nxt=(i, 2, 3, 4))
        u = matmul(h2, mlp_up, layer=i, out_dtype=BF16, act="relu2", name="mlp_up", **WIDE)
        z = matmul(u, mlp_down, layer=i, tn=1024, tk=2048, name="mlp_down")
        last = i + 1 == DEPTH
        nxt = None if last else (i + 1, 0, 0, 1)
        x, h = resid_norm_mod(x, z, gains, mods, layer=i, ga=3, gate=5, nxt=nxt, split_out=last)

    return (x[0].reshape(BATCH, SEQ, D_MODEL), x[1].reshape(DEC_BATCH, DEC_SEQ, D_MODEL),
            outs['na_k'], outs['na_v'], outs['lru'], outs['mc'], outs['mn'], outs['mm'], outs['gk'], outs['gv'])
```

```python
import functools

import jax
import jax.numpy as jnp
from jax import lax
from jax.experimental import pallas as pl
from jax.experimental.pallas import tpu as pltpu

D_MODEL = 4096
BATCH = 16
SEQ = 256
DEPTH = 4
DEC_BATCH = 2
DEC_SEQ = 1024
PAST_LEN = 512
GRID_W = 64
HEAD_DIM = 128
NA_HEADS = 32
WIN_R = 8
WIN_C = 16
GQA_HEADS = 32
GQA_KV_HEADS = 8
GQA_GROUP = GQA_HEADS // GQA_KV_HEADS
ROPE_THETA = 10000.0
D_RNN = D_MODEL
LRU_BLOCKS = 16
LRU_BW = 256
CONV_W = 4
LRU_C = 8.0
ML_HEADS = 8
ML_DK = 256
ML_DV = 512
ML_CHUNK = 128
D_FF = 4 * D_MODEL
EPS = 1e-6

N_CTX = BATCH * SEQ
N_LAT = DEC_BATCH * DEC_SEQ
N_TOK = N_CTX + N_LAT
MOD_ROWS = 8
LANES = 128
NEG_BIG = -1e30

VMEM_LIMIT = 56 * 1024 * 1024

BF16 = jnp.bfloat16
F32 = jnp.float32


def _cparams(sem):
    return pltpu.CompilerParams(dimension_semantics=sem, vmem_limit_bytes=VMEM_LIMIT)


def _dot(a, b):
    return jnp.dot(a, b, preferred_element_type=F32)


def _dot_nt(a, b):
    return lax.dot_general(a, b, (((1,), (1,)), ((), ())), preferred_element_type=F32)


def _dot_tn(a, b):
    return lax.dot_general(a, b, (((0,), (0,)), ((), ())), preferred_element_type=F32)


def _sigmoid(x):
    return 0.5 * (jnp.tanh(0.5 * x) + 1.0)


def _softplus(x):
    return jnp.maximum(x, 0.0) + jnp.log1p(jnp.exp(-jnp.abs(x)))


def _mm_kernel(*refs, act, n_first, n_split, with_ada):
    if with_ada:
        _ada_slab(*refs[-5:-2], refs[-1])
        refs = refs[:-5] + (refs[-2],)
    w_ref, o_ref = refs[-2:]
    slab = w_ref.shape[-1] // n_split

    def tile(a_ref):
        a = a_ref[...]
        for c in range(n_split):
            cols = slice(c * slab, (c + 1) * slab)
            acc = _dot(a, w_ref[:, cols].astype(BF16))
            if act == "relu2":
                r = jnp.maximum(acc, 0.0)
                acc = r * r
            o_ref[:, cols] = acc.astype(o_ref.dtype)

    if len(refs) == 3:
        tile(refs[0])
    else:
        i = pl.program_id(0)
        pl.when(i < n_first)(lambda: tile(refs[0]))
        pl.when(i >= n_first)(lambda: tile(refs[1]))


def _mm_kernel_kgrid(a_ref, w_ref, o_ref):
    @pl.when(pl.program_id(2) == 0)
    def _():
        o_ref[...] = jnp.zeros(o_ref.shape, o_ref.dtype)

    o_ref[...] += _dot(a_ref[...], w_ref[...].astype(BF16))


def matmul(a, w, *, layer=0, a2=None, out_dtype=F32, act=None, tm=1024, tn=512, tk=4096, n_split=1,
           a_buffers=2, ada=None, name="matmul"):
    m1, k = a.shape
    m = m1 + (0 if a2 is None else a2.shape[0])
    k2, n = w.shape[-2:]
    assert k == k2 and a.dtype == BF16
    lead = () if w.ndim == 2 else (None,)
    at = () if w.ndim == 2 else (layer,)
    tm, tn, tk = min(tm, m), min(tn, n), min(tk, k)
    assert m1 % tm == 0 and m % tm == 0 and n % tn == 0 and k % tk == 0
    nk = k // tk
    n_first = m1 // tm
    if nk == 1:
        a_mode = {} if a_buffers == 2 else {"pipeline_mode": pl.Buffered(a_buffers)}
        a_specs = [pl.BlockSpec((tm, k), lambda i, j: (jnp.minimum(i, n_first - 1), 0), **a_mode)]
        a_args = [a]
        if a2 is not None:
            assert a2.dtype == BF16 and a2.shape[1] == k
            a_specs.append(pl.BlockSpec((tm, k), lambda i, j: (jnp.maximum(i - n_first, 0), 0), **a_mode))
            a_args.append(a2)
        grid = (m // tm, n // tn)
        in_specs = a_specs + [pl.BlockSpec(lead + (k, tn), lambda i, j: at + (0, j))]
        args = a_args + [w]
        out_specs = pl.BlockSpec((tm, tn), lambda i, j: (i, j))
        out_shape = jax.ShapeDtypeStruct((m, n), out_dtype)
        if ada is not None:
            ada_w, ada_b, act_b, ada_layer = ada
            assert grid[0] * grid[1] * LANES == N_MOD
            step = lambda i, j: i * grid[1] + j
            in_specs += [pl.BlockSpec((None, D_MODEL, LANES), lambda i, j: (ada_layer, 0, step(i, j))),
                         pl.BlockSpec((N_COND, D_MODEL, LANES), lambda i, j: (0, 0, 0),
                                      pipeline_mode=pl.Buffered(1)),
                         pl.BlockSpec((None, 1, LANES), lambda i, j: (ada_layer, 0, step(i, j)))]
            args += [ada_w, act_b, ada_b.reshape(DEPTH, 1, N_MOD)]
            out_specs = [out_specs, pl.BlockSpec((MOD_ROWS, LANES), lambda i, j: (0, step(i, j)))]
            out_shape = [out_shape, jax.ShapeDtypeStruct((MOD_ROWS, N_MOD), F32)]
        return pl.pallas_call(
            functools.partial(_mm_kernel, act=act, n_first=n_first, n_split=n_split,
                              with_ada=ada is not None),
            grid=grid,
            in_specs=in_specs,
            out_specs=out_specs,
            out_shape=out_shape,
            compiler_params=_cparams(("parallel", "arbitrary")),
            name=name,
        )(*args)
    assert act is None and a2 is None and out_dtype == F32
    return pl.pallas_call(
        _mm_kernel_kgrid,
        grid=(m // tm, n // tn, nk),
        in_specs=[pl.BlockSpec((tm, tk), lambda i, j, kk: (i, kk)),
                  pl.BlockSpec(lead + (tk, tn), lambda i, j, kk: at + (kk, j))],
        out_specs=pl.BlockSpec((tm, tn), lambda i, j, kk: (i, j)),
        out_shape=jax.ShapeDtypeStruct((m, n), out_dtype),
        compiler_params=_cparams(("parallel", "arbitrary", "arbitrary")),
        name=name,
    )(a, w)


WIDE = dict(tn=1024, n_split=2, a_buffers=1)


N_MOD = 6 * D_MODEL
N_COND = 1 + DEC_BATCH


def _silu_kernel(c_ref, o_ref):
    c = c_ref[...]
    o_ref[...] = c * jax.nn.sigmoid(c)


def silu_rows(cond):
    return pl.pallas_call(
        _silu_kernel,
        out_shape=jax.ShapeDtypeStruct(cond.shape, F32),
        name="silu_cond",
    )(cond)


def _ada_kernel(a_ref, w_ref, b_ref, o_ref):
    o_ref[...] = _dot(a_ref[...].astype(BF16), w_ref[...].astype(BF16)) + b_ref[...]


def adaln_layer(act, ada_w, ada_b, layer, *, tn=512):
    out = pl.pallas_call(
        _ada_kernel,
        grid=(N_MOD // tn,),
        in_specs=[pl.BlockSpec((MOD_ROWS, D_MODEL), lambda j: (0, 0)),
                  pl.BlockSpec((None, D_MODEL, tn), lambda j: (layer, 0, j)),
                  pl.BlockSpec((None, 1, tn), lambda j: (layer, 0, j))],
        out_specs=pl.BlockSpec((MOD_ROWS, tn), lambda j: (0, j)),
        out_shape=jax.ShapeDtypeStruct((MOD_ROWS, N_MOD), F32),
        compiler_params=_cparams(("arbitrary",)),
        name="adaln",
    )(act, ada_w, ada_b.reshape(DEPTH, 1, N_MOD))
    return out.reshape(MOD_ROWS, 6, 1, D_MODEL)


def _ada_slab(adaw_ref, actb_ref, adab_ref, mod_ref):
    sub = 8
    accs = [jnp.zeros((sub, LANES), F32) for _ in range(N_COND)]
    for t in range(D_MODEL // sub):
        rs = slice(t * sub, (t + 1) * sub)
        w = adaw_ref[rs, :]
        for r in range(N_COND):
            accs[r] = accs[r] + w * actb_ref[r, rs, :]
    rows = [jnp.sum(a, axis=0, keepdims=True) for a in accs]
    rows.append(jnp.zeros((MOD_ROWS - N_COND, LANES), F32))
    mod_ref[...] = jnp.concatenate(rows, axis=0) + adab_ref[...]


ROW_TILE = 256


def _row_group(i):
    return jnp.maximum(0, (i * ROW_TILE - N_CTX) // DEC_SEQ + 1)


def _rms(x, g):
    return x * lax.rsqrt(jnp.mean(x * x, axis=-1, keepdims=True) + EPS) * g


def _mod_spec(which):
    return pl.BlockSpec((None, None, 1, D_MODEL), lambda i: (_row_group(i), which, 0, 0))


def _gain_spec(layer, which):
    return pl.BlockSpec((None, None, 1, D_MODEL), lambda i: (layer, which, 0, 0))


_ROWS_SPEC = pl.BlockSpec((ROW_TILE, D_MODEL), lambda i: (i, 0))


CTX_TILES = N_CTX // ROW_TILE
_CTX_ROWS_SPEC = pl.BlockSpec((ROW_TILE, D_MODEL), lambda i: (jnp.minimum(i, CTX_TILES - 1), 0))
_LAT_ROWS_SPEC = pl.BlockSpec((ROW_TILE, D_MODEL), lambda i: (jnp.maximum(i - CTX_TILES, 0), 0))


def _x_specs(x):
    return [_CTX_ROWS_SPEC, _LAT_ROWS_SPEC] if isinstance(x, tuple) else [_ROWS_SPEC]


def _x_args(x):
    return list(x) if isinstance(x, tuple) else [x]


def _load_rows(refs):
    if len(refs) == 1:
        return refs[0][...]
    return jnp.where(pl.program_id(0) < CTX_TILES, refs[0][...], refs[1][...])


def _norm_mod_kernel(*refs):
    g_ref, sh_ref, sc_ref, h_ref = refs[-4:]
    h = _rms(_load_rows(refs[:-4]), g_ref[...]) * (1.0 + sc_ref[...]) + sh_ref[...]
    h_ref[...] = h.astype(h_ref.dtype)


def norm_mod(x, gains, mods, layer):
    return pl.pallas_call(
        _norm_mod_kernel,
        grid=(N_TOK // ROW_TILE,),
        in_specs=_x_specs(x) + [_gain_spec(layer, 0), _mod_spec(0), _mod_spec(1)],
        out_specs=_ROWS_SPEC,
        out_shape=jax.ShapeDtypeStruct((N_TOK, D_MODEL), BF16),
        compiler_params=_cparams(("parallel",)),
        name="norm_mod",
    )(*_x_args(x), gains, mods[layer], mods[layer])


def _resid_kernel(*refs, n_x, with_h, split_out):
    x_refs, (y_ref, ga_ref, gate_ref), rest = refs[:n_x], refs[n_x:n_x + 3], refs[n_x + 3:]
    xn = _load_rows(x_refs) + gate_ref[...] * _rms(y_ref[...], ga_ref[...])
    if with_h:
        gb_ref, sh_ref, sc_ref = rest[:3]
        rest = rest[3:]
        h_ref = rest[-1]
        h = _rms(xn, gb_ref[...]) * (1.0 + sc_ref[...]) + sh_ref[...]
        h_ref[...] = h.astype(h_ref.dtype)
    if split_out:
        i = pl.program_id(0)

        @pl.when(i < CTX_TILES)
        def _():
            rest[0][...] = xn

        @pl.when(i >= CTX_TILES)
        def _():
            rest[1][...] = xn
    else:
        rest[0][...] = xn


def resid_norm_mod(x, y, gains, mods, *, layer, ga, gate, nxt, split_out=False):
    in_specs = _x_specs(x) + [_ROWS_SPEC, _gain_spec(layer, ga), _mod_spec(gate)]
    args = _x_args(x) + [y, gains, mods[layer]]
    if split_out:
        out_specs = [_CTX_ROWS_SPEC, _LAT_ROWS_SPEC]
        out_shape = [jax.ShapeDtypeStruct((N_CTX, D_MODEL), F32), jax.ShapeDtypeStruct((N_LAT, D_MODEL), F32)]
    else:
        out_specs = [_ROWS_SPEC]
        out_shape = [jax.ShapeDtypeStruct((N_TOK, D_MODEL), F32)]
    if nxt is not None:
        nl, ng, nsh, nsc = nxt
        in_specs += [_gain_spec(nl, ng), _mod_spec(nsh), _mod_spec(nsc)]
        args += [gains, mods[nl], mods[nl]]
        out_specs.append(_ROWS_SPEC)
        out_shape.append(jax.ShapeDtypeStruct((N_TOK, D_MODEL), BF16))
    res = pl.pallas_call(
        functools.partial(_resid_kernel, n_x=len(_x_args(x)), with_h=nxt is not None, split_out=split_out),
        grid=(N_TOK // ROW_TILE,),
        in_specs=in_specs,
        out_specs=out_specs,
        out_shape=out_shape,
        compiler_params=_cparams(("arbitrary",) if split_out else ("parallel",)),
        name="resid_norm_mod",
    )(*args)
    x_new = (res[0], res[1]) if split_out else res[0]
    return x_new, (res[-1] if nxt is not None else None)


def _head_rms(x, g):
    return x * lax.rsqrt(jnp.mean(x * x, axis=-1, keepdims=True) + EPS) * g


def _rope(x, cos, sin):
    lane = lax.broadcasted_iota(jnp.int32, x.shape, 1)
    partner = jnp.where((lane & 63) < 32, pltpu.roll(x, LANES - 32, 1), pltpu.roll(x, 32, 1))
    return x * cos + partner * sin


def _rope_tables():
    t = jnp.arange(DEC_SEQ)
    half = HEAD_DIM // 2
    inv_freq = 1.0 / (ROPE_THETA ** (jnp.arange(0, half, 2, dtype=F32) / half))
    ang_r = (t // GRID_W).astype(F32)[:, None] * inv_freq
    ang_c = (t % GRID_W).astype(F32)[:, None] * inv_freq
    cos = jnp.concatenate([jnp.cos(ang_r)] * 2 + [jnp.cos(ang_c)] * 2, axis=-1)
    sin = jnp.concatenate([-jnp.sin(ang_r), jnp.sin(ang_r), -jnp.sin(ang_c), jnp.sin(ang_c)], axis=-1)
    return cos, sin


def _softmax_pv(scores, values):
    m = functools.reduce(jnp.maximum, [jnp.max(s, axis=-1, keepdims=True) for s in scores])
    es = [jnp.exp(s - m) for s in scores]
    l = functools.reduce(jnp.add, [jnp.sum(e, axis=-1, keepdims=True) for e in es])
    o = functools.reduce(jnp.add, [_dot(e.astype(BF16), v) for e, v in zip(es, values)])
    return o / l


def _ctx_attn_kernel(q_ref, k_ref, v_ref, *rest, n_kv, group, normed):
    if normed:
        qg_ref, kg_ref, o_ref, ko_ref, vo_ref = rest
    else:
        o_ref, ko_ref, vo_ref = rest
    scale = HEAD_DIM ** -0.5
    for j in range(n_kv):
        ks = slice(j * HEAD_DIM, (j + 1) * HEAD_DIM)
        k = k_ref[:, ks]
        v = v_ref[:, ks]
        if normed:
            k = _head_rms(k, kg_ref[...])
        ko_ref[:, ks] = k
        vo_ref[:, ks] = v
        kb = k.astype(BF16)
        vb = v.astype(BF16)
        for g in range(group):
            qs = slice((j * group + g) * HEAD_DIM, (j * group + g + 1) * HEAD_DIM)
            q = q_ref[:, qs]
            if normed:
                q = _head_rms(q, qg_ref[...])
            s = _dot_nt(q.astype(BF16), kb) * scale
            o_ref[:, qs] = _softmax_pv([s], [vb]).astype(o_ref.dtype)


CTX_COLS = 1024


def na_context_attn(qkv):
    nb = D_MODEL // CTX_COLS
    blk = pl.BlockSpec((SEQ, CTX_COLS), lambda b, j: (b, j))
    kv_shape = jax.ShapeDtypeStruct((N_CTX, D_MODEL), F32)
    return pl.pallas_call(
        functools.partial(_ctx_attn_kernel, n_kv=CTX_COLS // HEAD_DIM, group=1, normed=False),
        grid=(BATCH, nb),
        in_specs=[blk,
                  pl.BlockSpec((SEQ, CTX_COLS), lambda b, j: (b, nb + j)),
                  pl.BlockSpec((SEQ, CTX_COLS), lambda b, j: (b, 2 * nb + j))],
        out_specs=[blk, blk, blk],
        out_shape=[jax.ShapeDtypeStruct((N_CTX, D_MODEL), BF16), kv_shape, kv_shape],
        compiler_params=_cparams(("parallel", "arbitrary")),
        name="na_ctx_attn",
    )(qkv, qkv, qkv)


def gqa_context_attn(qkv, q_gain, k_gain):
    n_kv = CTX_COLS // (GQA_GROUP * HEAD_DIM)
    kv_cols = n_kv * HEAD_DIM
    nb = D_MODEL // CTX_COLS
    k0 = D_MODEL // kv_cols
    v0 = k0 + GQA_KV_HEADS * HEAD_DIM // kv_cols
    kv_shape = jax.ShapeDtypeStruct((N_CTX, GQA_KV_HEADS * HEAD_DIM), F32)
    gain_spec = pl.BlockSpec((1, HEAD_DIM), lambda b, j: (0, 0))
    return pl.pallas_call(
        functools.partial(_ctx_attn_kernel, n_kv=n_kv, group=GQA_GROUP, normed=True),
        grid=(BATCH, nb),
        in_specs=[pl.BlockSpec((SEQ, CTX_COLS), lambda b, j: (b, j)),
                  pl.BlockSpec((SEQ, kv_cols), lambda b, j: (b, k0 + j)),
                  pl.BlockSpec((SEQ, kv_cols), lambda b, j: (b, v0 + j)),
                  gain_spec, gain_spec],
        out_specs=[pl.BlockSpec((SEQ, CTX_COLS), lambda b, j: (b, j)),
                   pl.BlockSpec((SEQ, kv_cols), lambda b, j: (b, j)),
                   pl.BlockSpec((SEQ, kv_cols), lambda b, j: (b, j))],
        out_shape=[jax.ShapeDtypeStruct((N_CTX, D_MODEL), BF16), kv_shape, kv_shape],
        compiler_params=_cparams(("parallel", "arbitrary")),
        name="gqa_ctx_attn",
    )(qkv, qkv, qkv, q_gain.reshape(1, HEAD_DIM), k_gain.reshape(1, HEAD_DIM))


def _gqa_lat_kernel(q_ref, k_ref, v_ref, kc_ref, vc_ref, cos_ref, sin_ref, qg_ref, kg_ref, o_ref):
    scale = HEAD_DIM ** -0.5
    cos, sin = cos_ref[...], sin_ref[...]
    kc = kc_ref[...].astype(BF16)
    vc = vc_ref[...].astype(BF16)
    k = _rope(_head_rms(k_ref[...], kg_ref[...]), cos, sin).astype(BF16)
    v = v_ref[...].astype(BF16)
    for g in range(GQA_GROUP):
        qs = slice(g * HEAD_DIM, (g + 1) * HEAD_DIM)
        q = _rope(_head_rms(q_ref[:, qs], qg_ref[...]), cos, sin).astype(BF16)
        s_ctx = _dot_nt(q, kc) * scale
        s_lat = _dot_nt(q, k) * scale
        o_ref[:, qs] = _softmax_pv([s_ctx, s_lat], [vc, v]).astype(o_ref.dtype)


def gqa_latent_attn(qkv, cache_k, cache_v, q_gain, k_gain):
    cos, sin = _rope_tables()
    qcols = GQA_GROUP * HEAD_DIM
    rb0 = N_CTX // DEC_SEQ
    k0 = D_MODEL // HEAD_DIM
    v0 = k0 + GQA_KV_HEADS
    cache_spec = pl.BlockSpec((None, PAST_LEN, HEAD_DIM), lambda b, j: (b, 0, j))
    table_spec = pl.BlockSpec((DEC_SEQ, HEAD_DIM), lambda b, j: (0, 0))
    gain_spec = pl.BlockSpec((1, HEAD_DIM), lambda b, j: (0, 0))
    return pl.pallas_call(
        _gqa_lat_kernel,
        grid=(DEC_BATCH, GQA_KV_HEADS),
        in_specs=[pl.BlockSpec((DEC_SEQ, qcols), lambda b, j: (rb0 + b, j)),
                  pl.BlockSpec((DEC_SEQ, HEAD_DIM), lambda b, j: (rb0 + b, k0 + j)),
                  pl.BlockSpec((DEC_SEQ, HEAD_DIM), lambda b, j: (rb0 + b, v0 + j)),
                  cache_spec, cache_spec, table_spec, table_spec, gain_spec, gain_spec],
        out_specs=pl.BlockSpec((DEC_SEQ, qcols), lambda b, j: (b, j)),
        out_shape=jax.ShapeDtypeStruct((N_LAT, D_MODEL), BF16),
        compiler_params=_cparams(("parallel", "arbitrary")),
        name="gqa_lat_attn",
    )(qkv, qkv, qkv, cache_k, cache_v, cos, sin, q_gain.reshape(1, HEAD_DIM), k_gain.reshape(1, HEAD_DIM))


NA_ROWS = DEC_SEQ // GRID_W
NA_WR = min(WIN_R, NA_ROWS)
NA_WKEYS = NA_WR * GRID_W


NA_REL_ROWS = 2 * WIN_R - 1
NA_ROW_PAIRS = NA_WR // 2


def _na_bias_table(rpb):
    col = jnp.arange(GRID_W)
    cs = jnp.clip(col - WIN_C // 2, 0, GRID_W - WIN_C)
    col_ok = (col[None, :] >= cs[:, None]) & (col[None, :] < cs[:, None] + WIN_C)
    dc = jnp.clip(col[None, :] - col[:, None] + WIN_C - 1, 0, 2 * WIN_C - 2)
    blocks = jnp.where(col_ok[None, None], rpb[:, :, dc], NEG_BIG).astype(F32)
    return jnp.concatenate([blocks[:, :-1], blocks[:, 1:]], axis=-1)


def _na_lat_kernel(q_ref, k_ref, v_ref, kc_ref, vc_ref, bias_ref, o_ref,
                   kb_ref, vb_ref, slat_ref, elat_ref, olat_ref):
    scale = HEAD_DIM ** -0.5
    kb_ref[...] = k_ref[...].astype(BF16)
    vb_ref[...] = v_ref[...].astype(BF16)

    def window(r):
        rs = min(max(r - NA_WR // 2, 0), NA_ROWS - NA_WR)
        return (slice(r * GRID_W, (r + 1) * GRID_W), slice(rs * GRID_W, rs * GRID_W + NA_WKEYS),
                rs - r + WIN_R - 1)

    for r in range(NA_ROWS):
        q_rows, k_rows, rel0 = window(r)
        bias = jnp.concatenate([bias_ref[rel0 + 2 * p] for p in range(NA_ROW_PAIRS)], axis=1)
        slat_ref[q_rows, :] = _dot_nt(q_ref[q_rows, :].astype(BF16), kb_ref[k_rows, :]) * scale + bias
    s_lat = slat_ref[...]
    s_ctx = _dot_nt(q_ref[...].astype(BF16), kc_ref[...].astype(BF16)) * scale
    m = jnp.maximum(jnp.max(s_lat, axis=-1, keepdims=True), jnp.max(s_ctx, axis=-1, keepdims=True))
    e_lat = jnp.exp(s_lat - m)
    e_ctx = jnp.exp(s_ctx - m)
    l = jnp.sum(e_lat, axis=-1, keepdims=True) + jnp.sum(e_ctx, axis=-1, keepdims=True)
    elat_ref[...] = e_lat.astype(BF16)
    o_ctx = _dot(e_ctx.astype(BF16), vc_ref[...].astype(BF16))
    for r in range(NA_ROWS):
        q_rows, k_rows, _ = window(r)
        olat_ref[q_rows, :] = _dot(elat_ref[q_rows, :], vb_ref[k_rows, :])
    o_ref[...] = ((olat_ref[...] + o_ctx) / l).astype(o_ref.dtype)


def na_latent_attn(qkv, cache_k, cache_v, rpb):
    bias = _na_bias_table(rpb)
    rb0 = N_CTX // DEC_SEQ
    cache_spec = pl.BlockSpec((None, PAST_LEN, HEAD_DIM), lambda h, b: (b, 0, h))
    return pl.pallas_call(
        _na_lat_kernel,
        grid=(NA_HEADS, DEC_BATCH),
        in_specs=[pl.BlockSpec((DEC_SEQ, HEAD_DIM), lambda h, b: (rb0 + b, h)),
                  pl.BlockSpec((DEC_SEQ, HEAD_DIM), lambda h, b: (rb0 + b, NA_HEADS + h)),
                  pl.BlockSpec((DEC_SEQ, HEAD_DIM), lambda h, b: (rb0 + b, 2 * NA_HEADS + h)),
                  cache_spec, cache_spec,
                  pl.BlockSpec((None, NA_REL_ROWS - 1, GRID_W, 2 * GRID_W), lambda h, b: (h, 0, 0, 0))],
        out_specs=pl.BlockSpec((DEC_SEQ, HEAD_DIM), lambda h, b: (b, h)),
        out_shape=jax.ShapeDtypeStruct((N_LAT, D_MODEL), BF16),
        scratch_shapes=[pltpu.VMEM((DEC_SEQ, HEAD_DIM), BF16), pltpu.VMEM((DEC_SEQ, HEAD_DIM), BF16),
                        pltpu.VMEM((DEC_SEQ, NA_WKEYS), F32), pltpu.VMEM((DEC_SEQ, NA_WKEYS), BF16),
                        pltpu.VMEM((DEC_SEQ, HEAD_DIM), F32)],
        compiler_params=_cparams(("parallel", "arbitrary")),
        name="na_lat_attn",
    )(qkv, qkv, qkv, cache_k, cache_v, bias)


LRU_ROWS = 2048
LRU_CTX_STEPS = N_CTX // LRU_ROWS
LRU_SEQ_PAD = 8
LRU_PITCH_PAD = 4
LRU_SCAN_ROWS = max(LRU_ROWS // SEQ * (SEQ + LRU_PITCH_PAD), LRU_ROWS // DEC_SEQ * (DEC_SEQ + LRU_PITCH_PAD))


def _lru_body(xr_ref, gb_ref, cw_ref, cb_ref, wg_ref, bg_ref, lam_ref, h0_ref, y_ref, fin_ref,
              af_ref, uf_ref, ab_ref, ub_ref, hf_ref, hb_ref, *, n_seq, seq_len):
    rows = n_seq * seq_len
    halves = [slice(c * LANES, (c + 1) * LANES) for c in range(LRU_BW // LANES)]
    pitch = seq_len + LRU_PITCH_PAD
    seqs = [(slice(s * seq_len, (s + 1) * seq_len), slice(s * pitch, s * pitch + seq_len))
            for s in range(n_seq)]
    x = xr_ref[...]
    t = lax.broadcasted_iota(jnp.int32, (rows, LRU_BW), 0) & (seq_len - 1)
    cw = cw_ref[...]
    xf = (jnp.where(t >= 2, pltpu.roll(x, 2, 0), 0.0) * cw[0:1]
          + jnp.where(t >= 1, pltpu.roll(x, 1, 0), 0.0) * cw[1:2]
          + x * cw[2:3]
          + jnp.where(t < seq_len - 1, pltpu.roll(x, rows - 1, 0), 0.0) * cw[3:4]) + cb_ref[...]
    xb = xf.astype(BF16)
    sp = _softplus(-lam_ref[...])
    for d, (a_ref, u_ref) in enumerate(((af_ref, uf_ref), (ab_ref, ub_ref))):
        bg = bg_ref[d]
        r_gate = _sigmoid(_dot(xb, wg_ref[d, 0].astype(BF16)) + bg[0:1])
        i_gate = _sigmoid(_dot(xb, wg_ref[d, 1].astype(BF16)) + bg[1:2])
        a = jnp.exp(-LRU_C * r_gate * sp[d:d + 1])
        u = jnp.sqrt(1.0 - a * a) * i_gate * xf
        for c, cols in enumerate(halves):
            for src, dst in seqs:
                a_ref[c, dst, :] = a[src, cols]
                u_ref[c, dst, :] = u[src, cols]

    def step(i, carry):
        rf = pl.ds(i, n_seq, stride=pitch)
        rb = pl.ds(seq_len - 1 - i, n_seq, stride=pitch)
        out = []
        for c in range(len(halves)):
            hf = af_ref[c, rf, :] * carry[2 * c] + uf_ref[c, rf, :]
            hf_ref[c, rf, :] = hf
            hb = ab_ref[c, rb, :] * carry[2 * c + 1] + ub_ref[c, rb, :]
            hb_ref[c, rb, :] = hb
            out += [hf, hb]
        return tuple(out)

    init = tuple(h0_ref[d, 0:n_seq, cols] for cols in halves for d in range(2))
    fin = lax.fori_loop(0, seq_len, step, init, unroll=4)
    fin_ref[...] = jnp.zeros(fin_ref.shape, F32)
    gate = jax.nn.gelu(gb_ref[...])
    for c, cols in enumerate(halves):
        fin_ref[0, 0:n_seq, cols] = fin[2 * c]
        fin_ref[1, 0:n_seq, cols] = fin[2 * c + 1]
        for src, dst in seqs:
            h = hf_ref[c, dst, :] + hb_ref[c, dst, :]
            y_ref[src, cols] = (h * gate[src, cols]).astype(y_ref.dtype)


def _lru_kernel(*refs):
    i = pl.program_id(0)

    @pl.when(i < LRU_CTX_STEPS)
    def _():
        _lru_body(*refs, n_seq=LRU_ROWS // SEQ, seq_len=SEQ)

    @pl.when(i >= LRU_CTX_STEPS)
    def _():
        _lru_body(*refs, n_seq=LRU_ROWS // DEC_SEQ, seq_len=DEC_SEQ)


def rglru_mix(proj, state, conv_w, conv_b, w_gates, b_gates, lam):
    assert LRU_ROWS // SEQ == LRU_SEQ_PAD and LRU_ROWS // DEC_SEQ == DEC_BATCH
    n_steps = N_TOK // LRU_ROWS
    h0 = jnp.zeros((2, n_steps * LRU_SEQ_PAD, D_RNN), F32)
    h0 = h0.at[:, BATCH:BATCH + DEC_BATCH].set(state.transpose(1, 0, 2))
    blk = lambda i, n: (i, n)
    chan = lambda i, n: (0, n)
    y, fin = pl.pallas_call(
        _lru_kernel,
        grid=(n_steps, LRU_BLOCKS),
        in_specs=[pl.BlockSpec((LRU_ROWS, LRU_BW), blk),
                  pl.BlockSpec((LRU_ROWS, LRU_BW), lambda i, n: (i, LRU_BLOCKS + n)),
                  pl.BlockSpec((CONV_W, LRU_BW), chan),
                  pl.BlockSpec((1, LRU_BW), chan),
                  pl.BlockSpec((2, 2, None, LRU_BW, LRU_BW), lambda i, n: (0, 0, n, 0, 0)),
                  pl.BlockSpec((2, 2, LRU_BW), lambda i, n: (0, 0, n)),
                  pl.BlockSpec((2, LRU_BW), chan),
                  pl.BlockSpec((2, LRU_SEQ_PAD, LRU_BW), lambda i, n: (0, i, n))],
        out_specs=[pl.BlockSpec((LRU_ROWS, LRU_BW), blk),
                   pl.BlockSpec((2, LRU_SEQ_PAD, LRU_BW), lambda i, n: (0, i, n))],
        out_shape=[jax.ShapeDtypeStruct((N_TOK, D_RNN), BF16),
                   jax.ShapeDtypeStruct((2, n_steps * LRU_SEQ_PAD, D_RNN), F32)],
        scratch_shapes=[pltpu.VMEM((LRU_BW // LANES, LRU_SCAN_ROWS, LANES), F32)] * 6,
        compiler_params=_cparams(("parallel", "arbitrary")),
        name="rglru_mix",
    )(proj, proj, conv_w, conv_b.reshape(1, D_RNN), w_gates, b_gates, lam, h0)
    return y, fin[:, :BATCH].transpose(1, 0, 2)


ML_GATE_COLS = 4 * ML_HEADS


def _pick_lane(x, idx):
    lane = lax.broadcasted_iota(jnp.int32, x.shape, 1)
    return jnp.sum(jnp.where(lane == idx, x, 0.0), axis=1, keepdims=True)


def _pick_row(x, idx):
    row = lax.broadcasted_iota(jnp.int32, x.shape, 0)
    return jnp.sum(jnp.where(row == idx, x, 0.0), axis=0, keepdims=True)


def _mlstm_kernel(q_ref, k_ref, v_ref, og_ref, g_ref, gt_ref, bg_ref, bgt_ref, ng_ref, *rest,
                  seq_len, zero_init, with_state):
    rest = list(rest)
    if not zero_init:
        c0_ref, n0_ref, m0_ref = rest[:3]
        rest = rest[3:]
    y_ref = rest.pop(0)
    if with_state:
        co_ref, no_ref, mo_ref = rest[:3]
        rest = rest[3:]
    (h_ref, c_ref, n_ref, dmat_ref, qk_ref, bcum_ref, rmax_ref, dec_ref, blast_ref, dmax_ref) = rest
    head = pl.program_id(1)
    n_chunks = seq_len // ML_CHUNK
    k_scale = ML_DK ** -0.5
    sub = lax.broadcasted_iota(jnp.int32, (ML_CHUNK, ML_CHUNK), 0)
    lane = lax.broadcasted_iota(jnp.int32, (ML_CHUNK, ML_CHUNK), 1)

    if zero_init:
        c_ref[...] = jnp.zeros(c_ref.shape, F32)
        n_ref[...] = jnp.zeros(n_ref.shape, F32)
        m_init = (jnp.zeros((1, 1), F32),) * 2
    else:
        c_ref[...] = c0_ref[...]
        n_ref[...] = n0_ref[...]
        m_init = (m0_ref[0], m0_ref[1])

    for c in range(n_chunks):
        rows = slice(c * ML_CHUNK, (c + 1) * ML_CHUNK)
        g = g_ref[rows, :] + bg_ref[...]
        gt = gt_ref[:, rows] + bgt_ref[...]
        qk_ref[c] = _dot_nt(q_ref[rows, :].astype(BF16), (k_ref[rows, :] * k_scale).astype(BF16))
        for d in range(2):
            valid = (lane <= sub) if d == 0 else (lane >= sub)
            valid_t = (sub <= lane) if d == 0 else (sub >= lane)
            i_col = d * 2 * ML_HEADS + head
            f_col = i_col + ML_HEADS
            li_col = _pick_lane(g, i_col)
            lf_col = -_softplus(-_pick_lane(g, f_col))
            li_row = _pick_row(gt, i_col)
            lf_row = -_softplus(-_pick_row(gt, f_col))
            bcum_col = jnp.sum(jnp.where(valid, lf_row, 0.0), axis=1, keepdims=True)
            bcum_row = jnp.sum(jnp.where(valid_t, lf_col, 0.0), axis=0, keepdims=True)
            b_last = jnp.sum(lf_row, axis=1, keepdims=True)
            dmat = jnp.where(valid, bcum_col - bcum_row + li_row, NEG_BIG)
            dec = b_last - bcum_col + li_col
            idx = d * n_chunks + c
            dmat_ref[idx] = dmat
            rmax_ref[idx] = jnp.max(dmat, axis=1, keepdims=True)
            bcum_ref[idx] = bcum_col
            dec_ref[idx] = dec
            blast_ref[idx] = b_last
            dmax_ref[idx] = jnp.max(dec, axis=0, keepdims=True)

    def one_chunk(d, c_idx, m_prev):
        idx = d * n_chunks + c_idx
        rows = pl.ds(pl.multiple_of(c_idx * ML_CHUNK, ML_CHUNK), ML_CHUNK)
        m_inter = bcum_ref[idx] + m_prev
        m_t = jnp.maximum(m_inter, rmax_ref[idx])
        qf = q_ref[rows, :]
        qb = qf.astype(BF16)
        kf = k_ref[rows, :] * k_scale
        vb = v_ref[rows, :].astype(BF16)
        s = qk_ref[c_idx] * jnp.exp(dmat_ref[idx] - m_t)
        inter = jnp.exp(m_inter - m_t)
        num = _dot(s.astype(BF16), vb) + inter * _dot(qb, c_ref[d].astype(BF16))
        den = (jnp.sum(s, axis=1, keepdims=True)
               + inter * jnp.sum(qf * n_ref[d], axis=1, keepdims=True))
        h_ref[d, rows, :] = num / jnp.maximum(jnp.abs(den), jnp.exp(-m_t))
        b_last = blast_ref[idx]
        m_new = jnp.maximum(b_last + m_prev, dmax_ref[idx])
        kw = kf * jnp.exp(dec_ref[idx] - m_new)
        carry_scale = jnp.exp(b_last + m_prev - m_new)
        c_ref[d] = carry_scale * c_ref[d] + _dot_tn(kw.astype(BF16), vb)
        n_ref[d] = carry_scale * n_ref[d] + jnp.sum(kw, axis=0, keepdims=True)
        return m_new

    def both(ci, m_prev):
        return one_chunk(0, ci, m_prev[0]), one_chunk(1, n_chunks - 1 - ci, m_prev[1])

    m_fin = lax.fori_loop(0, n_chunks, both, m_init)
    if with_state:
        co_ref[...] = c_ref[...]
        no_ref[...] = n_ref[...]
        mo_ref[0] = m_fin[0]
        mo_ref[1] = m_fin[1]

    hs = h_ref[0] + h_ref[1]
    hn = hs * lax.rsqrt(jnp.mean(hs * hs, axis=-1, keepdims=True) + EPS) * ng_ref[...]
    y_ref[...] = (jax.nn.sigmoid(og_ref[...]) * hn).astype(y_ref.dtype)


def mlstm_mix(proj, gates, gates_t, b_gates, norm_g, *, row0, n_seq, seq_len, state=None, with_state):
    rb0 = row0 // seq_len
    kq = ML_HEADS * ML_DK // ML_DK
    v0 = 2 * ML_HEADS * ML_DK // ML_DV
    o0 = v0 + ML_HEADS
    zero_init = state is None
    n_chunks = seq_len // ML_CHUNK
    bias = jnp.pad(b_gates.reshape(1, ML_GATE_COLS), ((0, 0), (0, LANES - ML_GATE_COLS)))
    in_specs = [pl.BlockSpec((seq_len, ML_DK), lambda b, h: (rb0 + b, h)),
                pl.BlockSpec((seq_len, ML_DK), lambda b, h: (rb0 + b, kq + h)),
                pl.BlockSpec((seq_len, ML_DV), lambda b, h: (rb0 + b, v0 + h)),
                pl.BlockSpec((seq_len, ML_DV), lambda b, h: (rb0 + b, o0 + h)),
                pl.BlockSpec((seq_len, LANES), lambda b, h: (rb0 + b, 0)),
                pl.BlockSpec((ML_GATE_COLS, seq_len), lambda b, h: (0, rb0 + b)),
                pl.BlockSpec((1, LANES), lambda b, h: (0, 0)),
                pl.BlockSpec((ML_GATE_COLS, 1), lambda b, h: (0, 0)),
                pl.BlockSpec((1, ML_DV), lambda b, h: (0, h))]
    args = [proj, proj, proj, proj, gates, gates_t, bias, b_gates.reshape(ML_GATE_COLS, 1),
            norm_g.reshape(1, ML_HEADS * ML_DV)]
    c_spec = pl.BlockSpec((None, 2, None, ML_DK, ML_DV), lambda b, h: (b, 0, h, 0, 0))
    n_spec = pl.BlockSpec((None, 2, None, 1, ML_DK), lambda b, h: (b, 0, h, 0, 0))
    m_spec = pl.BlockSpec((None, 2, None, 1, 1), lambda b, h: (b, 0, h, 0, 0))
    if not zero_init:
        c0, n0, m0 = state
        in_specs += [c_spec, n_spec, m_spec]
        args += [c0, n0.reshape(n_seq, 2, ML_HEADS, 1, ML_DK), m0.reshape(n_seq, 2, ML_HEADS, 1, 1)]
    out_specs = [pl.BlockSpec((seq_len, ML_DV), lambda b, h: (b, h))]
    out_shape = [jax.ShapeDtypeStruct((n_seq * seq_len, ML_HEADS * ML_DV), BF16)]
    if with_state:
        out_specs += [c_spec, n_spec, m_spec]
        out_shape += [jax.ShapeDtypeStruct((n_seq, 2, ML_HEADS, ML_DK, ML_DV), F32),
                      jax.ShapeDtypeStruct((n_seq, 2, ML_HEADS, 1, ML_DK), F32),
                      jax.ShapeDtypeStruct((n_seq, 2, ML_HEADS, 1, 1), F32)]
    res = pl.pallas_call(
        functools.partial(_mlstm_kernel, seq_len=seq_len, zero_init=zero_init, with_state=with_state),
        grid=(n_seq, ML_HEADS),
        in_specs=in_specs,
        out_specs=out_specs,
        out_shape=out_shape,
        scratch_shapes=[pltpu.VMEM((2, seq_len, ML_DV), F32), pltpu.VMEM((2, ML_DK, ML_DV), F32),
                        pltpu.VMEM((2, 1, ML_DK), F32),
                        pltpu.VMEM((2 * n_chunks, ML_CHUNK, ML_CHUNK), F32),
                        pltpu.VMEM((n_chunks, ML_CHUNK, ML_CHUNK), F32)]
                       + [pltpu.VMEM((2 * n_chunks, ML_CHUNK, 1), F32)] * 3
                       + [pltpu.VMEM((2 * n_chunks, 1, 1), F32)] * 2,
        compiler_params=_cparams(("parallel", "arbitrary")),
        name="mlstm_mix",
    )(*args)
    if not with_state:
        return res[0]
    y, c_fin, n_fin, m_fin = res
    return y, c_fin, n_fin.reshape(n_seq, 2, ML_HEADS, ML_DK), m_fin.reshape(n_seq, 2, ML_HEADS)


def kernel(x_prompt, x_sample, cache_na_k, cache_na_v, state_lru, state_mlstm_C, state_mlstm_n, state_mlstm_m, cache_gqa_k, cache_gqa_v, c, c_ctx, ada_w, ada_b, norm_g, mlp_up, mlp_down, na_w_qkv, na_rpb, na_w_o, lru_w_in, lru_conv_w, lru_conv_b, lru_w_gates, lru_b_gates, lru_lambda, lru_w_o, ml_w_in, ml_w_gates, ml_b_gates, ml_norm_g, ml_w_o, gqa_w_qkv, gqa_q_norm, gqa_k_norm, gqa_w_o):
    x = (x_prompt.reshape(N_CTX, D_MODEL), x_sample.reshape(N_LAT, D_MODEL))
    cond = jnp.concatenate([c_ctx[None], c, jnp.zeros((MOD_ROWS - 1 - DEC_BATCH, D_MODEL), F32)], axis=0)
    act = silu_rows(cond)
    mods = [adaln_layer(act, ada_w, ada_b, 0)]
    act_b = jnp.broadcast_to(act[:N_COND, :, None], (N_COND, D_MODEL, LANES))
    gains = norm_g.reshape(DEPTH, 4, 1, D_MODEL)

    outs = {}
    h = norm_mod(x, gains, mods, 0)
    for i in range(DEPTH):
        kind = i % 4
        if kind == 0:
            qkv = matmul(h, na_w_qkv[0], name="na_qkv", **WIDE)
            shp = (BATCH, 1, SEQ, NA_HEADS, HEAD_DIM)
            yp, kp, vp = na_context_attn(qkv)
            outs['na_k'], outs['na_v'] = kp.reshape(shp), vp.reshape(shp)
            yl = na_latent_attn(qkv, cache_na_k[:, 0].reshape(DEC_BATCH, PAST_LEN, D_MODEL),
                                cache_na_v[:, 0].reshape(DEC_BATCH, PAST_LEN, D_MODEL), na_rpb[0])
            y = matmul(yp, na_w_o[0], a2=yl, name="na_out")
        elif kind == 1:
            proj = matmul(h, lru_w_in[0], name="lru_in", **WIDE)
            pre, fin = rglru_mix(proj, state_lru[:, 0], lru_conv_w[0], lru_conv_b[0], lru_w_gates[0],
                                 lru_b_gates[0], lru_lambda[0])
            outs['lru'] = fin[:, None]
            y = matmul(pre, lru_w_o[0], name="lru_out")
        elif kind == 2:
            proj = matmul(h, ml_w_in[0], name="ml_in", **WIDE)
            wg = jnp.pad(ml_w_gates[0], ((0, 0), (0, LANES - ML_GATE_COLS)))
            gates = matmul(h, wg, tn=LANES, name="ml_gates")
            gates_t = gates[:, :ML_GATE_COLS].T
            yp, cp, np_, mp = mlstm_mix(proj, gates, gates_t, ml_b_gates[0], ml_norm_g[0], row0=0,
                                        n_seq=BATCH, seq_len=SEQ, with_state=True)
            yl = mlstm_mix(proj, gates, gates_t, ml_b_gates[0], ml_norm_g[0], row0=N_CTX,
                           n_seq=DEC_BATCH, seq_len=DEC_SEQ, with_state=False,
                           state=(state_mlstm_C[:, 0], state_mlstm_n[:, 0], state_mlstm_m[:, 0]))
            outs['mc'], outs['mn'], outs['mm'] = cp[:, None], np_[:, None], mp[:, None]
            y = matmul(yp, ml_w_o[0], a2=yl, name="ml_out")
        else:
            qkv = matmul(h, gqa_w_qkv[0], name="gqa_qkv", **WIDE)
            kv_cols = GQA_KV_HEADS * HEAD_DIM
            yp, kp, vp = gqa_context_attn(qkv, gqa_q_norm[0], gqa_k_norm[0])
            yl = gqa_latent_attn(qkv, cache_gqa_k[:, 0].reshape(DEC_BATCH, PAST_LEN, kv_cols),
                                 cache_gqa_v[:, 0].reshape(DEC_BATCH, PAST_LEN, kv_cols),
                                 gqa_q_norm[0], gqa_k_norm[0])
            shp = (BATCH, 1, SEQ, GQA_KV_HEADS, HEAD_DIM)
            outs['gk'], outs['gv'] = kp.reshape(shp), vp.reshape(shp)
            y = matmul(yp, gqa_w_o[0], a2=yl, name="gqa_out")

        x, h2 = resid_norm_mod(x, y, gains, mods, layer=i, ga=1, gate=2, nxt=(i, 2, 3, 4))
        last = i + 1 == DEPTH
        if last:
            u = matmul(h2, mlp_up, layer=i, out_dtype=BF16, act="relu2", name="mlp_up", **WIDE)
        else:
            u, mod = matmul(h2, mlp_up, layer=i, out_dtype=BF16, act="relu2", name="mlp_up_ada",
                            ada=(ada_w, ada_b, act_b, i + 1))
            mods.append(mod.reshape(MOD_ROWS, 6, 1, D_MODEL))
        z = matmul(u, mlp_down, layer=i, tn=1024, tk=2048, name="mlp_down")
        nxt = None if last else (i + 1, 0, 0, 1)
        x, h = resid_norm_mod(x, z, gains, mods, layer=i, ga=3, gate=5, nxt=nxt, split_out=last)

    return (x[0].reshape(BATCH, SEQ, D_MODEL), x[1].reshape(DEC_BATCH, DEC_SEQ, D_MODEL),
            outs['na_k'], outs['na_v'], outs['lru'], outs['mc'], outs['mn'], outs['mm'], outs['gk'], outs['gv'])
```

```python
import functools

import jax
import jax.numpy as jnp
from jax import lax
from jax.experimental import pallas as pl
from jax.experimental.pallas import tpu as pltpu

D_MODEL = 4096
BATCH = 16
SEQ = 256
DEPTH = 4
DEC_BATCH = 2
DEC_SEQ = 1024
PAST_LEN = 512
GRID_W = 64
HEAD_DIM = 128
NA_HEADS = 32
WIN_R = 8
WIN_C = 16
GQA_HEADS = 32
GQA_KV_HEADS = 8
GQA_GROUP = GQA_HEADS // GQA_KV_HEADS
ROPE_THETA = 10000.0
D_RNN = D_MODEL
LRU_BLOCKS = 16
LRU_BW = 256
CONV_W = 4
LRU_C = 8.0
ML_HEADS = 8
ML_DK = 256
ML_DV = 512
ML_CHUNK = 128
D_FF = 4 * D_MODEL
EPS = 1e-6

N_CTX = BATCH * SEQ
N_LAT = DEC_BATCH * DEC_SEQ
N_TOK = N_CTX + N_LAT
MOD_ROWS = 8
LANES = 128
NEG_BIG = -1e30
LOG2E = 1.4426950408889634

VMEM_LIMIT = 56 * 1024 * 1024

BF16 = jnp.bfloat16
F32 = jnp.float32


def _cparams(sem):
    return pltpu.CompilerParams(dimension_semantics=sem, vmem_limit_bytes=VMEM_LIMIT)


def _dot(a, b):
    return jnp.dot(a, b, preferred_element_type=F32)


def _dot_nt(a, b):
    return lax.dot_general(a, b, (((1,), (1,)), ((), ())), preferred_element_type=F32)


def _dot_tn(a, b):
    return lax.dot_general(a, b, (((0,), (0,)), ((), ())), preferred_element_type=F32)


def _sigmoid(x):
    return 0.5 * (jnp.tanh(0.5 * x) + 1.0)


def _softplus(x):
    return jnp.maximum(x, 0.0) + jnp.log1p(jnp.exp(-jnp.abs(x)))


def _mm_kernel(*refs, act, n_first, n_split, with_ada):
    if with_ada:
        _ada_slab(*refs[-5:-2], refs[-1])
        refs = refs[:-5] + (refs[-2],)
    w_ref, o_ref = refs[-2:]
    slab = w_ref.shape[-1] // n_split

    def tile(a_ref):
        a = a_ref[...]
        for c in range(n_split):
            cols = slice(c * slab, (c + 1) * slab)
            acc = _dot(a, w_ref[:, cols].astype(BF16))
            if act == "relu2":
                r = jnp.maximum(acc, 0.0)
                acc = r * r
            o_ref[:, cols] = acc.astype(o_ref.dtype)

    if len(refs) == 3:
        tile(refs[0])
    else:
        i = pl.program_id(0)
        pl.when(i < n_first)(lambda: tile(refs[0]))
        pl.when(i >= n_first)(lambda: tile(refs[1]))


def _mm_kernel_kgrid(a_ref, w_ref, o_ref):
    @pl.when(pl.program_id(2) == 0)
    def _():
        o_ref[...] = jnp.zeros(o_ref.shape, o_ref.dtype)

    o_ref[...] += _dot(a_ref[...], w_ref[...].astype(BF16))


def matmul(a, w, *, layer=0, a2=None, out_dtype=F32, act=None, tm=1024, tn=512, tk=4096, n_split=1,
           a_buffers=2, ada=None, name="matmul"):
    m1, k = a.shape
    m = m1 + (0 if a2 is None else a2.shape[0])
    k2, n = w.shape[-2:]
    assert k == k2 and a.dtype == BF16
    lead = () if w.ndim == 2 else (None,)
    at = () if w.ndim == 2 else (layer,)
    tm, tn, tk = min(tm, m), min(tn, n), min(tk, k)
    assert m1 % tm == 0 and m % tm == 0 and n % tn == 0 and k % tk == 0
    nk = k // tk
    n_first = m1 // tm
    if nk == 1:
        a_mode = {} if a_buffers == 2 else {"pipeline_mode": pl.Buffered(a_buffers)}
        a_specs = [pl.BlockSpec((tm, k), lambda i, j: (jnp.minimum(i, n_first - 1), 0), **a_mode)]
        a_args = [a]
        if a2 is not None:
            assert a2.dtype == BF16 and a2.shape[1] == k
            a_specs.append(pl.BlockSpec((tm, k), lambda i, j: (jnp.maximum(i - n_first, 0), 0), **a_mode))
            a_args.append(a2)
        grid = (m // tm, n // tn)
        in_specs = a_specs + [pl.BlockSpec(lead + (k, tn), lambda i, j: at + (0, j))]
        args = a_args + [w]
        out_specs = pl.BlockSpec((tm, tn), lambda i, j: (i, j))
        out_shape = jax.ShapeDtypeStruct((m, n), out_dtype)
        if ada is not None:
            ada_w, ada_b, act_b, ada_layer = ada
            assert grid[0] * grid[1] * LANES == N_MOD
            step = lambda i, j: i * grid[1] + j
            in_specs += [pl.BlockSpec((None, D_MODEL, LANES), lambda i, j: (ada_layer, 0, step(i, j))),
                         pl.BlockSpec((N_COND, D_MODEL, LANES), lambda i, j: (0, 0, 0),
                                      pipeline_mode=pl.Buffered(1)),
                         pl.BlockSpec((None, 1, LANES), lambda i, j: (ada_layer, 0, step(i, j)))]
            args += [ada_w, act_b, ada_b.reshape(DEPTH, 1, N_MOD)]
            out_specs = [out_specs, pl.BlockSpec((MOD_ROWS, LANES), lambda i, j: (0, step(i, j)))]
            out_shape = [out_shape, jax.ShapeDtypeStruct((MOD_ROWS, N_MOD), F32)]
        return pl.pallas_call(
            functools.partial(_mm_kernel, act=act, n_first=n_first, n_split=n_split,
                              with_ada=ada is not None),
            grid=grid,
            in_specs=in_specs,
            out_specs=out_specs,
            out_shape=out_shape,
            compiler_params=_cparams(("parallel", "arbitrary")),
            name=name,
        )(*args)
    assert act is None and a2 is None and out_dtype == F32
    return pl.pallas_call(
        _mm_kernel_kgrid,
        grid=(m // tm, n // tn, nk),
        in_specs=[pl.BlockSpec((tm, tk), lambda i, j, kk: (i, kk)),
                  pl.BlockSpec(lead + (tk, tn), lambda i, j, kk: at + (kk, j))],
        out_specs=pl.BlockSpec((tm, tn), lambda i, j, kk: (i, j)),
        out_shape=jax.ShapeDtypeStruct((m, n), out_dtype),
        compiler_params=_cparams(("parallel", "arbitrary", "arbitrary")),
        name=name,
    )(a, w)


WIDE = dict(tn=1024, n_split=2, a_buffers=1)


N_MOD = 6 * D_MODEL
N_COND = 1 + DEC_BATCH


def _silu_kernel(c_ref, o_ref):
    c = c_ref[...]
    o_ref[...] = c * jax.nn.sigmoid(c)


def silu_rows(cond):
    return pl.pallas_call(
        _silu_kernel,
        out_shape=jax.ShapeDtypeStruct(cond.shape, F32),
        name="silu_cond",
    )(cond)


def _ada_kernel(a_ref, w_ref, b_ref, o_ref):
    o_ref[...] = _dot(a_ref[...].astype(BF16), w_ref[...].astype(BF16)) + b_ref[...]


def adaln_layer(act, ada_w, ada_b, layer, *, tn=512):
    out = pl.pallas_call(
        _ada_kernel,
        grid=(N_MOD // tn,),
        in_specs=[pl.BlockSpec((MOD_ROWS, D_MODEL), lambda j: (0, 0)),
                  pl.BlockSpec((None, D_MODEL, tn), lambda j: (layer, 0, j)),
                  pl.BlockSpec((None, 1, tn), lambda j: (layer, 0, j))],
        out_specs=pl.BlockSpec((MOD_ROWS, tn), lambda j: (0, j)),
        out_shape=jax.ShapeDtypeStruct((MOD_ROWS, N_MOD), F32),
        compiler_params=_cparams(("arbitrary",)),
        name="adaln",
    )(act, ada_w, ada_b.reshape(DEPTH, 1, N_MOD))
    return out.reshape(MOD_ROWS, 6, 1, D_MODEL)


def _ada_slab(adaw_ref, actb_ref, adab_ref, mod_ref):
    sub = 8
    accs = [jnp.zeros((sub, LANES), F32) for _ in range(N_COND)]
    for t in range(D_MODEL // sub):
        rs = slice(t * sub, (t + 1) * sub)
        w = adaw_ref[rs, :]
        for r in range(N_COND):
            accs[r] = accs[r] + w * actb_ref[r, rs, :]
    rows = [jnp.sum(a, axis=0, keepdims=True) for a in accs]
    rows.append(jnp.zeros((MOD_ROWS - N_COND, LANES), F32))
    mod_ref[...] = jnp.concatenate(rows, axis=0) + adab_ref[...]


ROW_TILE = 256


def _row_group(i):
    return jnp.maximum(0, (i * ROW_TILE - N_CTX) // DEC_SEQ + 1)


def _rms(x, g):
    return x * lax.rsqrt(jnp.mean(x * x, axis=-1, keepdims=True) + EPS) * g


def _mod_spec(which):
    return pl.BlockSpec((None, None, 1, D_MODEL), lambda i: (_row_group(i), which, 0, 0))


def _gain_spec(layer, which):
    return pl.BlockSpec((None, None, 1, D_MODEL), lambda i: (layer, which, 0, 0))


_ROWS_SPEC = pl.BlockSpec((ROW_TILE, D_MODEL), lambda i: (i, 0))


CTX_TILES = N_CTX // ROW_TILE
_CTX_ROWS_SPEC = pl.BlockSpec((ROW_TILE, D_MODEL), lambda i: (jnp.minimum(i, CTX_TILES - 1), 0))
_LAT_ROWS_SPEC = pl.BlockSpec((ROW_TILE, D_MODEL), lambda i: (jnp.maximum(i - CTX_TILES, 0), 0))


def _x_specs(x):
    return [_CTX_ROWS_SPEC, _LAT_ROWS_SPEC] if isinstance(x, tuple) else [_ROWS_SPEC]


def _x_args(x):
    return list(x) if isinstance(x, tuple) else [x]


def _load_rows(refs):
    if len(refs) == 1:
        return refs[0][...]
    return jnp.where(pl.program_id(0) < CTX_TILES, refs[0][...], refs[1][...])


def _norm_mod_kernel(*refs):
    g_ref, sh_ref, sc_ref, h_ref = refs[-4:]
    h = _rms(_load_rows(refs[:-4]), g_ref[...]) * (1.0 + sc_ref[...]) + sh_ref[...]
    h_ref[...] = h.astype(h_ref.dtype)


def norm_mod(x, gains, mods, layer):
    return pl.pallas_call(
        _norm_mod_kernel,
        grid=(N_TOK // ROW_TILE,),
        in_specs=_x_specs(x) + [_gain_spec(layer, 0), _mod_spec(0), _mod_spec(1)],
        out_specs=_ROWS_SPEC,
        out_shape=jax.ShapeDtypeStruct((N_TOK, D_MODEL), BF16),
        compiler_params=_cparams(("parallel",)),
        name="norm_mod",
    )(*_x_args(x), gains, mods[layer], mods[layer])


def _resid_kernel(*refs, n_x, with_h, split_out):
    x_refs, (y_ref, ga_ref, gate_ref), rest = refs[:n_x], refs[n_x:n_x + 3], refs[n_x + 3:]
    xn = _load_rows(x_refs) + gate_ref[...] * _rms(y_ref[...], ga_ref[...])
    if with_h:
        gb_ref, sh_ref, sc_ref = rest[:3]
        rest = rest[3:]
        h_ref = rest[-1]
        h = _rms(xn, gb_ref[...]) * (1.0 + sc_ref[...]) + sh_ref[...]
        h_ref[...] = h.astype(h_ref.dtype)
    if split_out:
        i = pl.program_id(0)

        @pl.when(i < CTX_TILES)
        def _():
            rest[0][...] = xn

        @pl.when(i >= CTX_TILES)
        def _():
            rest[1][...] = xn
    else:
        rest[0][...] = xn


def resid_norm_mod(x, y, gains, mods, *, layer, ga, gate, nxt, split_out=False):
    in_specs = _x_specs(x) + [_ROWS_SPEC, _gain_spec(layer, ga), _mod_spec(gate)]
    args = _x_args(x) + [y, gains, mods[layer]]
    if split_out:
        out_specs = [_CTX_ROWS_SPEC, _LAT_ROWS_SPEC]
        out_shape = [jax.ShapeDtypeStruct((N_CTX, D_MODEL), F32), jax.ShapeDtypeStruct((N_LAT, D_MODEL), F32)]
    else:
        out_specs = [_ROWS_SPEC]
        out_shape = [jax.ShapeDtypeStruct((N_TOK, D_MODEL), F32)]
    if nxt is not None:
        nl, ng, nsh, nsc = nxt
        in_specs += [_gain_spec(nl, ng), _mod_spec(nsh), _mod_spec(nsc)]
        args += [gains, mods[nl], mods[nl]]
        out_specs.append(_ROWS_SPEC)
        out_shape.append(jax.ShapeDtypeStruct((N_TOK, D_MODEL), BF16))
    res = pl.pallas_call(
        functools.partial(_resid_kernel, n_x=len(_x_args(x)), with_h=nxt is not None, split_out=split_out),
        grid=(N_TOK // ROW_TILE,),
        in_specs=in_specs,
        out_specs=out_specs,
        out_shape=out_shape,
        compiler_params=_cparams(("arbitrary",) if split_out else ("parallel",)),
        name="resid_norm_mod",
    )(*args)
    x_new = (res[0], res[1]) if split_out else res[0]
    return x_new, (res[-1] if nxt is not None else None)


def _head_rms(x, g):
    return x * lax.rsqrt(jnp.mean(x * x, axis=-1, keepdims=True) + EPS) * g


def _rope(x, cos, sin):
    lane = lax.broadcasted_iota(jnp.int32, x.shape, 1)
    partner = jnp.where((lane & 63) < 32, pltpu.roll(x, LANES - 32, 1), pltpu.roll(x, 32, 1))
    return x * cos + partner * sin


def _rope_tables():
    t = jnp.arange(DEC_SEQ)
    half = HEAD_DIM // 2
    inv_freq = 1.0 / (ROPE_THETA ** (jnp.arange(0, half, 2, dtype=F32) / half))
    ang_r = (t // GRID_W).astype(F32)[:, None] * inv_freq
    ang_c = (t % GRID_W).astype(F32)[:, None] * inv_freq
    cos = jnp.concatenate([jnp.cos(ang_r)] * 2 + [jnp.cos(ang_c)] * 2, axis=-1)
    sin = jnp.concatenate([-jnp.sin(ang_r), jnp.sin(ang_r), -jnp.sin(ang_c), jnp.sin(ang_c)], axis=-1)
    return cos, sin


def _softmax_pv(scores, values, exp=jnp.exp):
    m = functools.reduce(jnp.maximum, [jnp.max(s, axis=-1, keepdims=True) for s in scores])
    es = [exp(s - m) for s in scores]
    l = functools.reduce(jnp.add, [jnp.sum(e, axis=-1, keepdims=True) for e in es])
    o = functools.reduce(jnp.add, [_dot(e.astype(BF16), v) for e, v in zip(es, values)])
    return o / l


def _ctx_attn_kernel(q_ref, k_ref, v_ref, *rest, n_kv, group, normed):
    if normed:
        qg_ref, kg_ref, o_ref, ko_ref, vo_ref = rest
    else:
        o_ref, ko_ref, vo_ref = rest
    scale = HEAD_DIM ** -0.5
    for j in range(n_kv):
        ks = slice(j * HEAD_DIM, (j + 1) * HEAD_DIM)
        k = k_ref[:, ks]
        v = v_ref[:, ks]
        if normed:
            k = _head_rms(k, kg_ref[...])
        ko_ref[:, ks] = k
        vo_ref[:, ks] = v
        kb = k.astype(BF16)
        vb = v.astype(BF16)
        for g in range(group):
            qs = slice((j * group + g) * HEAD_DIM, (j * group + g + 1) * HEAD_DIM)
            q = q_ref[:, qs]
            if normed:
                q = _head_rms(q, qg_ref[...])
            s = _dot_nt(q.astype(BF16), kb) * scale
            o_ref[:, qs] = _softmax_pv([s], [vb]).astype(o_ref.dtype)


CTX_COLS = 1024


def na_context_attn(qkv):
    nb = D_MODEL // CTX_COLS
    blk = pl.BlockSpec((SEQ, CTX_COLS), lambda b, j: (b, j))
    kv_shape = jax.ShapeDtypeStruct((N_CTX, D_MODEL), F32)
    return pl.pallas_call(
        functools.partial(_ctx_attn_kernel, n_kv=CTX_COLS // HEAD_DIM, group=1, normed=False),
        grid=(BATCH, nb),
        in_specs=[blk,
                  pl.BlockSpec((SEQ, CTX_COLS), lambda b, j: (b, nb + j)),
                  pl.BlockSpec((SEQ, CTX_COLS), lambda b, j: (b, 2 * nb + j))],
        out_specs=[blk, blk, blk],
        out_shape=[jax.ShapeDtypeStruct((N_CTX, D_MODEL), BF16), kv_shape, kv_shape],
        compiler_params=_cparams(("parallel", "arbitrary")),
        name="na_ctx_attn",
    )(qkv, qkv, qkv)


def gqa_context_attn(qkv, q_gain, k_gain):
    n_kv = CTX_COLS // (GQA_GROUP * HEAD_DIM)
    kv_cols = n_kv * HEAD_DIM
    nb = D_MODEL // CTX_COLS
    k0 = D_MODEL // kv_cols
    v0 = k0 + GQA_KV_HEADS * HEAD_DIM // kv_cols
    kv_shape = jax.ShapeDtypeStruct((N_CTX, GQA_KV_HEADS * HEAD_DIM), F32)
    gain_spec = pl.BlockSpec((1, HEAD_DIM), lambda b, j: (0, 0))
    return pl.pallas_call(
        functools.partial(_ctx_attn_kernel, n_kv=n_kv, group=GQA_GROUP, normed=True),
        grid=(BATCH, nb),
        in_specs=[pl.BlockSpec((SEQ, CTX_COLS), lambda b, j: (b, j)),
                  pl.BlockSpec((SEQ, kv_cols), lambda b, j: (b, k0 + j)),
                  pl.BlockSpec((SEQ, kv_cols), lambda b, j: (b, v0 + j)),
                  gain_spec, gain_spec],
        out_specs=[pl.BlockSpec((SEQ, CTX_COLS), lambda b, j: (b, j)),
                   pl.BlockSpec((SEQ, kv_cols), lambda b, j: (b, j)),
                   pl.BlockSpec((SEQ, kv_cols), lambda b, j: (b, j))],
        out_shape=[jax.ShapeDtypeStruct((N_CTX, D_MODEL), BF16), kv_shape, kv_shape],
        compiler_params=_cparams(("parallel", "arbitrary")),
        name="gqa_ctx_attn",
    )(qkv, qkv, qkv, q_gain.reshape(1, HEAD_DIM), k_gain.reshape(1, HEAD_DIM))


def _gqa_lat_kernel(q_ref, k_ref, v_ref, kc_ref, vc_ref, cos_ref, sin_ref, qg_ref, kg_ref, o_ref):
    q_scale = HEAD_DIM ** -0.5 * LOG2E
    cos, sin = cos_ref[...], sin_ref[...]
    kc = kc_ref[...].astype(BF16)
    vc = vc_ref[...].astype(BF16)
    k = _rope(_head_rms(k_ref[...], kg_ref[...]), cos, sin).astype(BF16)
    v = v_ref[...].astype(BF16)
    for g in range(GQA_GROUP):
        qs = slice(g * HEAD_DIM, (g + 1) * HEAD_DIM)
        q = (_rope(_head_rms(q_ref[:, qs], qg_ref[...]), cos, sin) * q_scale).astype(BF16)
        scores = [_dot_nt(q, kc), _dot_nt(q, k)]
        o_ref[:, qs] = _softmax_pv(scores, [vc, v], exp=jnp.exp2).astype(o_ref.dtype)


def gqa_latent_attn(qkv, cache_k, cache_v, q_gain, k_gain):
    cos, sin = _rope_tables()
    qcols = GQA_GROUP * HEAD_DIM
    rb0 = N_CTX // DEC_SEQ
    k0 = D_MODEL // HEAD_DIM
    v0 = k0 + GQA_KV_HEADS
    cache_spec = pl.BlockSpec((None, PAST_LEN, HEAD_DIM), lambda b, j: (b, 0, j))
    table_spec = pl.BlockSpec((DEC_SEQ, HEAD_DIM), lambda b, j: (0, 0))
    gain_spec = pl.BlockSpec((1, HEAD_DIM), lambda b, j: (0, 0))
    return pl.pallas_call(
        _gqa_lat_kernel,
        grid=(DEC_BATCH, GQA_KV_HEADS),
        in_specs=[pl.BlockSpec((DEC_SEQ, qcols), lambda b, j: (rb0 + b, j)),
                  pl.BlockSpec((DEC_SEQ, HEAD_DIM), lambda b, j: (rb0 + b, k0 + j)),
                  pl.BlockSpec((DEC_SEQ, HEAD_DIM), lambda b, j: (rb0 + b, v0 + j)),
                  cache_spec, cache_spec, table_spec, table_spec, gain_spec, gain_spec],
        out_specs=pl.BlockSpec((DEC_SEQ, qcols), lambda b, j: (b, j)),
        out_shape=jax.ShapeDtypeStruct((N_LAT, D_MODEL), BF16),
        compiler_params=_cparams(("parallel", "arbitrary")),
        name="gqa_lat_attn",
    )(qkv, qkv, qkv, cache_k, cache_v, cos, sin, q_gain.reshape(1, HEAD_DIM), k_gain.reshape(1, HEAD_DIM))


NA_ROWS = DEC_SEQ // GRID_W
NA_WR = min(WIN_R, NA_ROWS)
NA_WKEYS = NA_WR * GRID_W


NA_REL_ROWS = 2 * WIN_R - 1
NA_ROW_PAIRS = NA_WR // 2


def _na_bias_table(rpb):
    col = jnp.arange(GRID_W)
    cs = jnp.clip(col - WIN_C // 2, 0, GRID_W - WIN_C)
    col_ok = (col[None, :] >= cs[:, None]) & (col[None, :] < cs[:, None] + WIN_C)
    dc = jnp.clip(col[None, :] - col[:, None] + WIN_C - 1, 0, 2 * WIN_C - 2)
    blocks = jnp.where(col_ok[None, None], rpb[:, :, dc], NEG_BIG).astype(F32)
    return jnp.concatenate([blocks[:, :-1], blocks[:, 1:]], axis=-1)


def _na_lat_kernel(q_ref, k_ref, v_ref, kc_ref, vc_ref, bias_ref, o_ref,
                   kb_ref, vb_ref, slat_ref, elat_ref, olat_ref):
    scale = HEAD_DIM ** -0.5
    kb_ref[...] = k_ref[...].astype(BF16)
    vb_ref[...] = v_ref[...].astype(BF16)

    def window(r):
        rs = min(max(r - NA_WR // 2, 0), NA_ROWS - NA_WR)
        return (slice(r * GRID_W, (r + 1) * GRID_W), slice(rs * GRID_W, rs * GRID_W + NA_WKEYS),
                rs - r + WIN_R - 1)

    for r in range(NA_ROWS):
        q_rows, k_rows, rel0 = window(r)
        bias = jnp.concatenate([bias_ref[rel0 + 2 * p] for p in range(NA_ROW_PAIRS)], axis=1)
        slat_ref[q_rows, :] = _dot_nt(q_ref[q_rows, :].astype(BF16), kb_ref[k_rows, :]) * scale + bias
    s_lat = slat_ref[...]
    s_ctx = _dot_nt(q_ref[...].astype(BF16), kc_ref[...].astype(BF16)) * scale
    m = jnp.maximum(jnp.max(s_lat, axis=-1, keepdims=True), jnp.max(s_ctx, axis=-1, keepdims=True))
    e_lat = jnp.exp(s_lat - m)
    e_ctx = jnp.exp(s_ctx - m)
    l = jnp.sum(e_lat, axis=-1, keepdims=True) + jnp.sum(e_ctx, axis=-1, keepdims=True)
    elat_ref[...] = e_lat.astype(BF16)
    o_ctx = _dot(e_ctx.astype(BF16), vc_ref[...].astype(BF16))
    for r in range(NA_ROWS):
        q_rows, k_rows, _ = window(r)
        olat_ref[q_rows, :] = _dot(elat_ref[q_rows, :], vb_ref[k_rows, :])
    o_ref[...] = ((olat_ref[...] + o_ctx) / l).astype(o_ref.dtype)


def na_latent_attn(qkv, cache_k, cache_v, rpb):
    bias = _na_bias_table(rpb)
    rb0 = N_CTX // DEC_SEQ
    cache_spec = pl.BlockSpec((None, PAST_LEN, HEAD_DIM), lambda h, b: (b, 0, h))
    return pl.pallas_call(
        _na_lat_kernel,
        grid=(NA_HEADS, DEC_BATCH),
        in_specs=[pl.BlockSpec((DEC_SEQ, HEAD_DIM), lambda h, b: (rb0 + b, h)),
                  pl.BlockSpec((DEC_SEQ, HEAD_DIM), lambda h, b: (rb0 + b, NA_HEADS + h)),
                  pl.BlockSpec((DEC_SEQ, HEAD_DIM), lambda h, b: (rb0 + b, 2 * NA_HEADS + h)),
                  cache_spec, cache_spec,
                  pl.BlockSpec((None, NA_REL_ROWS - 1, GRID_W, 2 * GRID_W), lambda h, b: (h, 0, 0, 0))],
        out_specs=pl.BlockSpec((DEC_SEQ, HEAD_DIM), lambda h, b: (b, h)),
        out_shape=jax.ShapeDtypeStruct((N_LAT, D_MODEL), BF16),
        scratch_shapes=[pltpu.VMEM((DEC_SEQ, HEAD_DIM), BF16), pltpu.VMEM((DEC_SEQ, HEAD_DIM), BF16),
                        pltpu.VMEM((DEC_SEQ, NA_WKEYS), F32), pltpu.VMEM((DEC_SEQ, NA_WKEYS), BF16),
                        pltpu.VMEM((DEC_SEQ, HEAD_DIM), F32)],
        compiler_params=_cparams(("parallel", "arbitrary")),
        name="na_lat_attn",
    )(qkv, qkv, qkv, cache_k, cache_v, bias)


LRU_ROWS = 2048
LRU_CTX_STEPS = N_CTX // LRU_ROWS
LRU_SEQ_PAD = 8
LRU_PITCH_PAD = 4
LRU_SCAN_ROWS = max(LRU_ROWS // SEQ * (SEQ + LRU_PITCH_PAD), LRU_ROWS // DEC_SEQ * (DEC_SEQ + LRU_PITCH_PAD))


def _lru_body(xr_ref, gb_ref, cw_ref, cb_ref, wg_ref, bg_ref, lam_ref, h0_ref, y_ref, fin_ref,
              af_ref, uf_ref, ab_ref, ub_ref, hf_ref, hb_ref, *, n_seq, seq_len):
    rows = n_seq * seq_len
    halves = [slice(c * LANES, (c + 1) * LANES) for c in range(LRU_BW // LANES)]
    pitch = seq_len + LRU_PITCH_PAD
    seqs = [(slice(s * seq_len, (s + 1) * seq_len), slice(s * pitch, s * pitch + seq_len))
            for s in range(n_seq)]
    x = xr_ref[...]
    t = lax.broadcasted_iota(jnp.int32, (rows, LRU_BW), 0) & (seq_len - 1)
    cw = cw_ref[...]
    xf = (jnp.where(t >= 2, pltpu.roll(x, 2, 0), 0.0) * cw[0:1]
          + jnp.where(t >= 1, pltpu.roll(x, 1, 0), 0.0) * cw[1:2]
          + x * cw[2:3]
          + jnp.where(t < seq_len - 1, pltpu.roll(x, rows - 1, 0), 0.0) * cw[3:4]) + cb_ref[...]
    xb = xf.astype(BF16)
    sp = _softplus(-lam_ref[...])
    xh = 0.5 * xf
    for d, (a_ref, u_ref) in enumerate(((af_ref, uf_ref), (ab_ref, ub_ref))):
        bg = 0.5 * bg_ref[d]
        t_r = jnp.tanh(_dot(xb, (0.5 * wg_ref[d, 0]).astype(BF16)) + bg[0:1])
        t_i = jnp.tanh(_dot(xb, (0.5 * wg_ref[d, 1]).astype(BF16)) + bg[1:2])
        decay = (-0.5 * LRU_C * LOG2E) * sp[d:d + 1]
        a = jnp.exp2(decay * (t_r + 1.0))
        u = jnp.sqrt(1.0 - a * a) * (t_i + 1.0) * xh
        for c, cols in enumerate(halves):
            for src, dst in seqs:
                a_ref[c, dst, :] = a[src, cols]
                u_ref[c, dst, :] = u[src, cols]

    def step(i, carry):
        rf = pl.ds(i, n_seq, stride=pitch)
        rb = pl.ds(seq_len - 1 - i, n_seq, stride=pitch)
        out = []
        for c in range(len(halves)):
            hf = af_ref[c, rf, :] * carry[2 * c] + uf_ref[c, rf, :]
            hf_ref[c, rf, :] = hf
            hb = ab_ref[c, rb, :] * carry[2 * c + 1] + ub_ref[c, rb, :]
            hb_ref[c, rb, :] = hb
            out += [hf, hb]
        return tuple(out)

    init = tuple(h0_ref[d, 0:n_seq, cols] for cols in halves for d in range(2))
    fin = lax.fori_loop(0, seq_len, step, init, unroll=4)
    fin_ref[...] = jnp.zeros(fin_ref.shape, F32)
    gate = jax.nn.gelu(gb_ref[...])
    for c, cols in enumerate(halves):
        fin_ref[0, 0:n_seq, cols] = fin[2 * c]
        fin_ref[1, 0:n_seq, cols] = fin[2 * c + 1]
        for src, dst in seqs:
            h = hf_ref[c, dst, :] + hb_ref[c, dst, :]
            y_ref[src, cols] = (h * gate[src, cols]).astype(y_ref.dtype)


def _lru_kernel(*refs):
    i = pl.program_id(0)

    @pl.when(i < LRU_CTX_STEPS)
    def _():
        _lru_body(*refs, n_seq=LRU_ROWS // SEQ, seq_len=SEQ)

    @pl.when(i >= LRU_CTX_STEPS)
    def _():
        _lru_body(*refs, n_seq=LRU_ROWS // DEC_SEQ, seq_len=DEC_SEQ)


def rglru_mix(proj, state, conv_w, conv_b, w_gates, b_gates, lam):
    assert LRU_ROWS // SEQ == LRU_SEQ_PAD and LRU_ROWS // DEC_SEQ == DEC_BATCH
    n_steps = N_TOK // LRU_ROWS
    h0 = jnp.zeros((2, n_steps * LRU_SEQ_PAD, D_RNN), F32)
    h0 = h0.at[:, BATCH:BATCH + DEC_BATCH].set(state.transpose(1, 0, 2))
    blk = lambda i, n: (i, n)
    chan = lambda i, n: (0, n)
    y, fin = pl.pallas_call(
        _lru_kernel,
        grid=(n_steps, LRU_BLOCKS),
        in_specs=[pl.BlockSpec((LRU_ROWS, LRU_BW), blk),
                  pl.BlockSpec((LRU_ROWS, LRU_BW), lambda i, n: (i, LRU_BLOCKS + n)),
                  pl.BlockSpec((CONV_W, LRU_BW), chan),
                  pl.BlockSpec((1, LRU_BW), chan),
                  pl.BlockSpec((2, 2, None, LRU_BW, LRU_BW), lambda i, n: (0, 0, n, 0, 0)),
                  pl.BlockSpec((2, 2, LRU_BW), lambda i, n: (0, 0, n)),
                  pl.BlockSpec((2, LRU_BW), chan),
                  pl.BlockSpec((2, LRU_SEQ_PAD, LRU_BW), lambda i, n: (0, i, n))],
        out_specs=[pl.BlockSpec((LRU_ROWS, LRU_BW), blk),
                   pl.BlockSpec((2, LRU_SEQ_PAD, LRU_BW), lambda i, n: (0, i, n))],
        out_shape=[jax.ShapeDtypeStruct((N_TOK, D_RNN), BF16),
                   jax.ShapeDtypeStruct((2, n_steps * LRU_SEQ_PAD, D_RNN), F32)],
        scratch_shapes=[pltpu.VMEM((LRU_BW // LANES, LRU_SCAN_ROWS, LANES), F32)] * 6,
        compiler_params=_cparams(("parallel", "arbitrary")),
        name="rglru_mix",
    )(proj, proj, conv_w, conv_b.reshape(1, D_RNN), w_gates, b_gates, lam, h0)
    return y, fin[:, :BATCH].transpose(1, 0, 2)


ML_GATE_COLS = 4 * ML_HEADS


def _pick_lane(x, idx):
    lane = lax.broadcasted_iota(jnp.int32, x.shape, 1)
    return jnp.sum(jnp.where(lane == idx, x, 0.0), axis=1, keepdims=True)


def _pick_row(x, idx):
    row = lax.broadcasted_iota(jnp.int32, x.shape, 0)
    return jnp.sum(jnp.where(row == idx, x, 0.0), axis=0, keepdims=True)


def _mlstm_kernel(q_ref, k_ref, v_ref, og_ref, g_ref, gt_ref, bg_ref, bgt_ref, ng_ref, *rest,
                  seq_len, zero_init, with_state):
    rest = list(rest)
    if not zero_init:
        c0_ref, n0_ref, m0_ref = rest[:3]
        rest = rest[3:]
    y_ref = rest.pop(0)
    if with_state:
        co_ref, no_ref, mo_ref = rest[:3]
        rest = rest[3:]
    (h_ref, c_ref, n_ref, dmat_ref, qk_ref, bcum_ref, rmax_ref, dec_ref, blast_ref, dmax_ref) = rest
    head = pl.program_id(1)
    n_chunks = seq_len // ML_CHUNK
    k_scale = ML_DK ** -0.5
    sub = lax.broadcasted_iota(jnp.int32, (ML_CHUNK, ML_CHUNK), 0)
    lane = lax.broadcasted_iota(jnp.int32, (ML_CHUNK, ML_CHUNK), 1)

    if zero_init:
        c_ref[...] = jnp.zeros(c_ref.shape, F32)
        n_ref[...] = jnp.zeros(n_ref.shape, F32)
        m_init = (jnp.zeros((1, 1), F32),) * 2
    else:
        c_ref[...] = c0_ref[...]
        n_ref[...] = n0_ref[...]
        m_init = (m0_ref[0], m0_ref[1])

    for c in range(n_chunks):
        rows = slice(c * ML_CHUNK, (c + 1) * ML_CHUNK)
        g = g_ref[rows, :] + bg_ref[...]
        gt = gt_ref[:, rows] + bgt_ref[...]
        qk_ref[c] = _dot_nt(q_ref[rows, :].astype(BF16), (k_ref[rows, :] * k_scale).astype(BF16))
        for d in range(2):
            valid = (lane <= sub) if d == 0 else (lane >= sub)
            valid_t = (sub <= lane) if d == 0 else (sub >= lane)
            i_col = d * 2 * ML_HEADS + head
            f_col = i_col + ML_HEADS
            li_col = _pick_lane(g, i_col)
            lf_col = -_softplus(-_pick_lane(g, f_col))
            li_row = _pick_row(gt, i_col)
            lf_row = -_softplus(-_pick_row(gt, f_col))
            bcum_col = jnp.sum(jnp.where(valid, lf_row, 0.0), axis=1, keepdims=True)
            bcum_row = jnp.sum(jnp.where(valid_t, lf_col, 0.0), axis=0, keepdims=True)
            b_last = jnp.sum(lf_row, axis=1, keepdims=True)
            dmat = jnp.where(valid, bcum_col - bcum_row + li_row, NEG_BIG)
            dec = b_last - bcum_col + li_col
            idx = d * n_chunks + c
            dmat_ref[idx] = dmat
            rmax_ref[idx] = jnp.max(dmat, axis=1, keepdims=True)
            bcum_ref[idx] = bcum_col
            dec_ref[idx] = dec
            blast_ref[idx] = b_last
            dmax_ref[idx] = jnp.max(dec, axis=0, keepdims=True)

    def one_chunk(d, c_idx, m_prev):
        idx = d * n_chunks + c_idx
        rows = pl.ds(pl.multiple_of(c_idx * ML_CHUNK, ML_CHUNK), ML_CHUNK)
        m_inter = bcum_ref[idx] + m_prev
        m_t = jnp.maximum(m_inter, rmax_ref[idx])
        qf = q_ref[rows, :]
        qb = qf.astype(BF16)
        kf = k_ref[rows, :] * k_scale
        vb = v_ref[rows, :].astype(BF16)
        s = qk_ref[c_idx] * jnp.exp(dmat_ref[idx] - m_t)
        inter = jnp.exp(m_inter - m_t)
        num = _dot(s.astype(BF16), vb) + inter * _dot(qb, c_ref[d].astype(BF16))
        den = (jnp.sum(s, axis=1, keepdims=True)
               + inter * jnp.sum(qf * n_ref[d], axis=1, keepdims=True))
        h_ref[d, rows, :] = num / jnp.maximum(jnp.abs(den), jnp.exp(-m_t))
        b_last = blast_ref[idx]
        m_new = jnp.maximum(b_last + m_prev, dmax_ref[idx])
        kw = kf * jnp.exp(dec_ref[idx] - m_new)
        carry_scale = jnp.exp(b_last + m_prev - m_new)
        c_ref[d] = carry_scale * c_ref[d] + _dot_tn(kw.astype(BF16), vb)
        n_ref[d] = carry_scale * n_ref[d] + jnp.sum(kw, axis=0, keepdims=True)
        return m_new

    def both(ci, m_prev):
        return one_chunk(0, ci, m_prev[0]), one_chunk(1, n_chunks - 1 - ci, m_prev[1])

    m_fin = lax.fori_loop(0, n_chunks, both, m_init)
    if with_state:
        co_ref[...] = c_ref[...]
        no_ref[...] = n_ref[...]
        mo_ref[0] = m_fin[0]
        mo_ref[1] = m_fin[1]

    hs = h_ref[0] + h_ref[1]
    hn = hs * lax.rsqrt(jnp.mean(hs * hs, axis=-1, keepdims=True) + EPS) * ng_ref[...]
    y_ref[...] = (_sigmoid(og_ref[...]) * hn).astype(y_ref.dtype)


def mlstm_mix(proj, gates, gates_t, b_gates, norm_g, *, row0, n_seq, seq_len, state=None, with_state):
    rb0 = row0 // seq_len
    kq = ML_HEADS * ML_DK // ML_DK
    v0 = 2 * ML_HEADS * ML_DK // ML_DV
    o0 = v0 + ML_HEADS
    zero_init = state is None
    n_chunks = seq_len // ML_CHUNK
    bias = jnp.pad(b_gates.reshape(1, ML_GATE_COLS), ((0, 0), (0, LANES - ML_GATE_COLS)))
    in_specs = [pl.BlockSpec((seq_len, ML_DK), lambda b, h: (rb0 + b, h)),
                pl.BlockSpec((seq_len, ML_DK), lambda b, h: (rb0 + b, kq + h)),
                pl.BlockSpec((seq_len, ML_DV), lambda b, h: (rb0 + b, v0 + h)),
                pl.BlockSpec((seq_len, ML_DV), lambda b, h: (rb0 + b, o0 + h)),
                pl.BlockSpec((seq_len, LANES), lambda b, h: (rb0 + b, 0)),
                pl.BlockSpec((ML_GATE_COLS, seq_len), lambda b, h: (0, rb0 + b)),
                pl.BlockSpec((1, LANES), lambda b, h: (0, 0)),
                pl.BlockSpec((ML_GATE_COLS, 1), lambda b, h: (0, 0)),
                pl.BlockSpec((1, ML_DV), lambda b, h: (0, h))]
    args = [proj, proj, proj, proj, gates, gates_t, bias, b_gates.reshape(ML_GATE_COLS, 1),
            norm_g.reshape(1, ML_HEADS * ML_DV)]
    c_spec = pl.BlockSpec((None, 2, None, ML_DK, ML_DV), lambda b, h: (b, 0, h, 0, 0))
    n_spec = pl.BlockSpec((None, 2, None, 1, ML_DK), lambda b, h: (b, 0, h, 0, 0))
    m_spec = pl.BlockSpec((None, 2, None, 1, 1), lambda b, h: (b, 0, h, 0, 0))
    if not zero_init:
        c0, n0, m0 = state
        in_specs += [c_spec, n_spec, m_spec]
        args += [c0, n0.reshape(n_seq, 2, ML_HEADS, 1, ML_DK), m0.reshape(n_seq, 2, ML_HEADS, 1, 1)]
    out_specs = [pl.BlockSpec((seq_len, ML_DV), lambda b, h: (b, h))]
    out_shape = [jax.ShapeDtypeStruct((n_seq * seq_len, ML_HEADS * ML_DV), BF16)]
    if with_state:
        out_specs += [c_spec, n_spec, m_spec]
        out_shape += [jax.ShapeDtypeStruct((n_seq, 2, ML_HEADS, ML_DK, ML_DV), F32),
                      jax.ShapeDtypeStruct((n_seq, 2, ML_HEADS, 1, ML_DK), F32),
                      jax.ShapeDtypeStruct((n_seq, 2, ML_HEADS, 1, 1), F32)]
    res = pl.pallas_call(
        functools.partial(_mlstm_kernel, seq_len=seq_len, zero_init=zero_init, with_state=with_state),
        grid=(n_seq, ML_HEADS),
        in_specs=in_specs,
        out_specs=out_specs,
        out_shape=out_shape,
        scratch_shapes=[pltpu.VMEM((2, seq_len, ML_DV), F32), pltpu.VMEM((2, ML_DK, ML_DV), F32),
                        pltpu.VMEM((2, 1, ML_DK), F32),
                        pltpu.VMEM((2 * n_chunks, ML_CHUNK, ML_CHUNK), F32),
                        pltpu.VMEM((n_chunks, ML_CHUNK, ML_CHUNK), F32)]
                       + [pltpu.VMEM((2 * n_chunks, ML_CHUNK, 1), F32)] * 3
                       + [pltpu.VMEM((2 * n_chunks, 1, 1), F32)] * 2,
        compiler_params=_cparams(("parallel", "arbitrary")),
        name="mlstm_mix",
    )(*args)
    if not with_state:
        return res[0]
    y, c_fin, n_fin, m_fin = res
    return y, c_fin, n_fin.reshape(n_seq, 2, ML_HEADS, ML_DK), m_fin.reshape(n_seq, 2, ML_HEADS)


def kernel(x_prompt, x_sample, cache_na_k, cache_na_v, state_lru, state_mlstm_C, state_mlstm_n, state_mlstm_m, cache_gqa_k, cache_gqa_v, c, c_ctx, ada_w, ada_b, norm_g, mlp_up, mlp_down, na_w_qkv, na_rpb, na_w_o, lru_w_in, lru_conv_w, lru_conv_b, lru_w_gates, lru_b_gates, lru_lambda, lru_w_o, ml_w_in, ml_w_gates, ml_b_gates, ml_norm_g, ml_w_o, gqa_w_qkv, gqa_q_norm, gqa_k_norm, gqa_w_o):
    x = (x_prompt.reshape(N_CTX, D_MODEL), x_sample.reshape(N_LAT, D_MODEL))
    cond = jnp.concatenate([c_ctx[None], c, jnp.zeros((MOD_ROWS - 1 - DEC_BATCH, D_MODEL), F32)], axis=0)
    act = silu_rows(cond)
    mods = [adaln_layer(act, ada_w, ada_b, 0)]
    act_b = jnp.broadcast_to(act[:N_COND, :, None], (N_COND, D_MODEL, LANES))
    gains = norm_g.reshape(DEPTH, 4, 1, D_MODEL)

    outs = {}
    h = norm_mod(x, gains, mods, 0)
    for i in range(DEPTH):
        kind = i % 4
        if kind == 0:
            qkv = matmul(h, na_w_qkv[0], name="na_qkv", **WIDE)
            shp = (BATCH, 1, SEQ, NA_HEADS, HEAD_DIM)
            yp, kp, vp = na_context_attn(qkv)
            outs['na_k'], outs['na_v'] = kp.reshape(shp), vp.reshape(shp)
            yl = na_latent_attn(qkv, cache_na_k[:, 0].reshape(DEC_BATCH, PAST_LEN, D_MODEL),
                                cache_na_v[:, 0].reshape(DEC_BATCH, PAST_LEN, D_MODEL), na_rpb[0])
            y = matmul(yp, na_w_o[0], a2=yl, name="na_out")
        elif kind == 1:
            proj = matmul(h, lru_w_in[0], name="lru_in", **WIDE)
            pre, fin = rglru_mix(proj, state_lru[:, 0], lru_conv_w[0], lru_conv_b[0], lru_w_gates[0],
                                 lru_b_gates[0], lru_lambda[0])
            outs['lru'] = fin[:, None]
            y = matmul(pre, lru_w_o[0], name="lru_out")
        elif kind == 2:
            proj = matmul(h, ml_w_in[0], name="ml_in", **WIDE)
            wg = jnp.pad(ml_w_gates[0], ((0, 0), (0, LANES - ML_GATE_COLS)))
            gates = matmul(h, wg, tn=LANES, name="ml_gates")
            gates_t = gates[:, :ML_GATE_COLS].T
            yp, cp, np_, mp = mlstm_mix(proj, gates, gates_t, ml_b_gates[0], ml_norm_g[0], row0=0,
                                        n_seq=BATCH, seq_len=SEQ, with_state=True)
            yl = mlstm_mix(proj, gates, gates_t, ml_b_gates[0], ml_norm_g[0], row0=N_CTX,
                           n_seq=DEC_BATCH, seq_len=DEC_SEQ, with_state=False,
                           state=(state_mlstm_C[:, 0], state_mlstm_n[:, 0], state_mlstm_m[:, 0]))
            outs['mc'], outs['mn'], outs['mm'] = cp[:, None], np_[:, None], mp[:, None]
            y = matmul(yp, ml_w_o[0], a2=yl, name="ml_out")
        else:
            qkv = matmul(h, gqa_w_qkv[0], name="gqa_qkv", **WIDE)
            kv_cols = GQA_KV_HEADS * HEAD_DIM
            yp, kp, vp = gqa_context_attn(qkv, gqa_q_norm[0], gqa_k_norm[0])
            yl = gqa_latent_attn(qkv, cache_gqa_k[:, 0].reshape(DEC_BATCH, PAST_LEN, kv_cols),
                                 cache_gqa_v[:, 0].reshape(DEC_BATCH, PAST_LEN, kv_cols),
                                 gqa_q_norm[0], gqa_k_norm[0])
            shp = (BATCH, 1, SEQ, GQA_KV_HEADS, HEAD_DIM)
            outs['gk'], outs['gv'] = kp.reshape(shp), vp.reshape(shp)
            y = matmul(yp, gqa_w_o[0], a2=yl, name="gqa_out")

        x, h2 = resid_norm_mod(x, y, gains, mods, layer=i, ga=1, gate=2, nxt=(i, 2, 3, 4))
        last = i + 1 == DEPTH
        if last:
            u = matmul(h2, mlp_up, layer=i, out_dtype=BF16, act="relu2", name="mlp_up", **WIDE)
        else:
            u, mod = matmul(h2, mlp_up, layer=i, out_dtype=BF16, act="relu2", name="mlp_up_ada",
                            ada=(ada_w, ada_b, act_b, i + 1))
            mods.append(mod.reshape(MOD_ROWS, 6, 1, D_MODEL))
        z = matmul(u, mlp_down, layer=i, tn=1024, tk=2048, name="mlp_down")
        nxt = None if last else (i + 1, 0, 0, 1)
        x, h = resid_norm_mod(x, z, gains, mods, layer=i, ga=3, gate=5, nxt=nxt, split_out=last)

    return (x[0].reshape(BATCH, SEQ, D_MODEL), x[1].reshape(DEC_BATCH, DEC_SEQ, D_MODEL),
            outs['na_k'], outs['na_v'], outs['lru'], outs['mc'], outs['mn'], outs['mm'], outs['gk'], outs['gv'])
```

```python
import functools

import jax
import jax.numpy as jnp
from jax import lax
from jax.experimental import pallas as pl
from jax.experimental.pallas import tpu as pltpu

D_MODEL = 4096
BATCH = 16
SEQ = 256
DEPTH = 4
DEC_BATCH = 2
DEC_SEQ = 1024
PAST_LEN = 512
GRID_W = 64
HEAD_DIM = 128
NA_HEADS = 32
WIN_R = 8
WIN_C = 16
GQA_HEADS = 32
GQA_KV_HEADS = 8
GQA_GROUP = GQA_HEADS // GQA_KV_HEADS
ROPE_THETA = 10000.0
D_RNN = D_MODEL
LRU_BLOCKS = 16
LRU_BW = 256
CONV_W = 4
LRU_C = 8.0
ML_HEADS = 8
ML_DK = 256
ML_DV = 512
ML_CHUNK = 128
D_FF = 4 * D_MODEL
EPS = 1e-6

N_CTX = BATCH * SEQ
N_LAT = DEC_BATCH * DEC_SEQ
N_TOK = N_CTX + N_LAT
MOD_ROWS = 8
LANES = 128
NEG_BIG = -1e30
LOG2E = 1.4426950408889634

VMEM_LIMIT = 56 * 1024 * 1024

BF16 = jnp.bfloat16
F32 = jnp.float32


def _cparams(sem):
    return pltpu.CompilerParams(dimension_semantics=sem, vmem_limit_bytes=VMEM_LIMIT)


def _dot(a, b):
    return jnp.dot(a, b, preferred_element_type=F32)


def _dot_nt(a, b):
    return lax.dot_general(a, b, (((1,), (1,)), ((), ())), preferred_element_type=F32)


def _dot_tn(a, b):
    return lax.dot_general(a, b, (((0,), (0,)), ((), ())), preferred_element_type=F32)


def _sigmoid(x):
    return 0.5 * (jnp.tanh(0.5 * x) + 1.0)


def _softplus(x):
    return jnp.maximum(x, 0.0) + jnp.log1p(jnp.exp(-jnp.abs(x)))


def _mm_kernel(*refs, act, n_first, n_split, with_ada):
    if with_ada:
        _ada_slab(*refs[-5:-2], refs[-1])
        refs = refs[:-5] + (refs[-2],)
    w_ref, o_ref = refs[-2:]
    slab = w_ref.shape[-1] // n_split

    def tile(a_ref):
        a = a_ref[...]
        for c in range(n_split):
            cols = slice(c * slab, (c + 1) * slab)
            acc = _dot(a, w_ref[:, cols].astype(BF16))
            if act == "relu2":
                r = jnp.maximum(acc, 0.0)
                acc = r * r
            o_ref[:, cols] = acc.astype(o_ref.dtype)

    if len(refs) == 3:
        tile(refs[0])
    else:
        i = pl.program_id(0)
        pl.when(i < n_first)(lambda: tile(refs[0]))
        pl.when(i >= n_first)(lambda: tile(refs[1]))


def _mm_kernel_kgrid(a_ref, w_ref, o_ref):
    @pl.when(pl.program_id(2) == 0)
    def _():
        o_ref[...] = jnp.zeros(o_ref.shape, o_ref.dtype)

    o_ref[...] += _dot(a_ref[...], w_ref[...].astype(BF16))


def matmul(a, w, *, layer=0, a2=None, out_dtype=F32, act=None, tm=1024, tn=512, tk=4096, n_split=1,
           a_buffers=2, ada=None, name="matmul"):
    m1, k = a.shape
    m = m1 + (0 if a2 is None else a2.shape[0])
    k2, n = w.shape[-2:]
    assert k == k2 and a.dtype == BF16
    lead = () if w.ndim == 2 else (None,)
    at = () if w.ndim == 2 else (layer,)
    tm, tn, tk = min(tm, m), min(tn, n), min(tk, k)
    assert m1 % tm == 0 and m % tm == 0 and n % tn == 0 and k % tk == 0
    nk = k // tk
    n_first = m1 // tm
    if nk == 1:
        a_mode = {} if a_buffers == 2 else {"pipeline_mode": pl.Buffered(a_buffers)}
        a_specs = [pl.BlockSpec((tm, k), lambda i, j: (jnp.minimum(i, n_first - 1), 0), **a_mode)]
        a_args = [a]
        if a2 is not None:
            assert a2.dtype == BF16 and a2.shape[1] == k
            a_specs.append(pl.BlockSpec((tm, k), lambda i, j: (jnp.maximum(i - n_first, 0), 0), **a_mode))
            a_args.append(a2)
        grid = (m // tm, n // tn)
        in_specs = a_specs + [pl.BlockSpec(lead + (k, tn), lambda i, j: at + (0, j))]
        args = a_args + [w]
        out_specs = pl.BlockSpec((tm, tn), lambda i, j: (i, j))
        out_shape = jax.ShapeDtypeStruct((m, n), out_dtype)
        if ada is not None:
            ada_w, ada_b, act_b, ada_layer = ada
            assert grid[0] * grid[1] * LANES == N_MOD
            step = lambda i, j: i * grid[1] + j
            in_specs += [pl.BlockSpec((None, D_MODEL, LANES), lambda i, j: (ada_layer, 0, step(i, j))),
                         pl.BlockSpec((N_COND, D_MODEL, LANES), lambda i, j: (0, 0, 0),
                                      pipeline_mode=pl.Buffered(1)),
                         pl.BlockSpec((None, 1, LANES), lambda i, j: (ada_layer, 0, step(i, j)))]
            args += [ada_w, act_b, ada_b.reshape(DEPTH, 1, N_MOD)]
            out_specs = [out_specs, pl.BlockSpec((MOD_ROWS, LANES), lambda i, j: (0, step(i, j)))]
            out_shape = [out_shape, jax.ShapeDtypeStruct((MOD_ROWS, N_MOD), F32)]
        return pl.pallas_call(
            functools.partial(_mm_kernel, act=act, n_first=n_first, n_split=n_split,
                              with_ada=ada is not None),
            grid=grid,
            in_specs=in_specs,
            out_specs=out_specs,
            out_shape=out_shape,
            compiler_params=_cparams(("parallel", "arbitrary")),
            name=name,
        )(*args)
    assert act is None and a2 is None and out_dtype == F32
    return pl.pallas_call(
        _mm_kernel_kgrid,
        grid=(m // tm, n // tn, nk),
        in_specs=[pl.BlockSpec((tm, tk), lambda i, j, kk: (i, kk)),
                  pl.BlockSpec(lead + (tk, tn), lambda i, j, kk: at + (kk, j))],
        out_specs=pl.BlockSpec((tm, tn), lambda i, j, kk: (i, j)),
        out_shape=jax.ShapeDtypeStruct((m, n), out_dtype),
        compiler_params=_cparams(("parallel", "arbitrary", "arbitrary")),
        name=name,
    )(a, w)


WIDE = dict(tn=1024, n_split=2, a_buffers=1)


N_MOD = 6 * D_MODEL
N_COND = 1 + DEC_BATCH


def _silu_kernel(c_ref, o_ref):
    c = c_ref[...]
    o_ref[...] = c * jax.nn.sigmoid(c)


def silu_rows(cond):
    return pl.pallas_call(
        _silu_kernel,
        out_shape=jax.ShapeDtypeStruct(cond.shape, F32),
        name="silu_cond",
    )(cond)


def _ada_kernel(a_ref, w_ref, b_ref, o_ref):
    o_ref[...] = _dot(a_ref[...].astype(BF16), w_ref[...].astype(BF16)) + b_ref[...]


def adaln_layer(act, ada_w, ada_b, layer, *, tn=512):
    out = pl.pallas_call(
        _ada_kernel,
        grid=(N_MOD // tn,),
        in_specs=[pl.BlockSpec((MOD_ROWS, D_MODEL), lambda j: (0, 0)),
                  pl.BlockSpec((None, D_MODEL, tn), lambda j: (layer, 0, j)),
                  pl.BlockSpec((None, 1, tn), lambda j: (layer, 0, j))],
        out_specs=pl.BlockSpec((MOD_ROWS, tn), lambda j: (0, j)),
        out_shape=jax.ShapeDtypeStruct((MOD_ROWS, N_MOD), F32),
        compiler_params=_cparams(("arbitrary",)),
        name="adaln",
    )(act, ada_w, ada_b.reshape(DEPTH, 1, N_MOD))
    return out.reshape(MOD_ROWS, 6, 1, D_MODEL)


def _ada_slab(adaw_ref, actb_ref, adab_ref, mod_ref):
    sub = 8
    accs = [jnp.zeros((sub, LANES), F32) for _ in range(N_COND)]
    for t in range(D_MODEL // sub):
        rs = slice(t * sub, (t + 1) * sub)
        w = adaw_ref[rs, :]
        for r in range(N_COND):
            accs[r] = accs[r] + w * actb_ref[r, rs, :]
    rows = [jnp.sum(a, axis=0, keepdims=True) for a in accs]
    rows.append(jnp.zeros((MOD_ROWS - N_COND, LANES), F32))
    mod_ref[...] = jnp.concatenate(rows, axis=0) + adab_ref[...]


ROW_TILE = 256


def _row_group(i):
    return jnp.maximum(0, (i * ROW_TILE - N_CTX) // DEC_SEQ + 1)


def _rms(x, g):
    return x * lax.rsqrt(jnp.mean(x * x, axis=-1, keepdims=True) + EPS) * g


def _mod_spec(which):
    return pl.BlockSpec((None, None, 1, D_MODEL), lambda i: (_row_group(i), which, 0, 0))


def _gain_spec(layer, which):
    return pl.BlockSpec((None, None, 1, D_MODEL), lambda i: (layer, which, 0, 0))


_ROWS_SPEC = pl.BlockSpec((ROW_TILE, D_MODEL), lambda i: (i, 0))


CTX_TILES = N_CTX // ROW_TILE
_CTX_ROWS_SPEC = pl.BlockSpec((ROW_TILE, D_MODEL), lambda i: (jnp.minimum(i, CTX_TILES - 1), 0))
_LAT_ROWS_SPEC = pl.BlockSpec((ROW_TILE, D_MODEL), lambda i: (jnp.maximum(i - CTX_TILES, 0), 0))


def _x_specs(x):
    return [_CTX_ROWS_SPEC, _LAT_ROWS_SPEC] if isinstance(x, tuple) else [_ROWS_SPEC]


def _x_args(x):
    return list(x) if isinstance(x, tuple) else [x]


def _load_rows(refs):
    if len(refs) == 1:
        return refs[0][...]
    return jnp.where(pl.program_id(0) < CTX_TILES, refs[0][...], refs[1][...])


def _norm_mod_kernel(*refs):
    g_ref, sh_ref, sc_ref, h_ref = refs[-4:]
    h = _rms(_load_rows(refs[:-4]), g_ref[...]) * (1.0 + sc_ref[...]) + sh_ref[...]
    h_ref[...] = h.astype(h_ref.dtype)


def norm_mod(x, gains, mods, layer):
    return pl.pallas_call(
        _norm_mod_kernel,
        grid=(N_TOK // ROW_TILE,),
        in_specs=_x_specs(x) + [_gain_spec(layer, 0), _mod_spec(0), _mod_spec(1)],
        out_specs=_ROWS_SPEC,
        out_shape=jax.ShapeDtypeStruct((N_TOK, D_MODEL), BF16),
        compiler_params=_cparams(("parallel",)),
        name="norm_mod",
    )(*_x_args(x), gains, mods[layer], mods[layer])


def _resid_kernel(*refs, n_x, with_h, split_out):
    x_refs, (y_ref, ga_ref, gate_ref), rest = refs[:n_x], refs[n_x:n_x + 3], refs[n_x + 3:]
    xn = _load_rows(x_refs) + gate_ref[...] * _rms(y_ref[...], ga_ref[...])
    if with_h:
        gb_ref, sh_ref, sc_ref = rest[:3]
        rest = rest[3:]
        h_ref = rest[-1]
        h = _rms(xn, gb_ref[...]) * (1.0 + sc_ref[...]) + sh_ref[...]
        h_ref[...] = h.astype(h_ref.dtype)
    if split_out:
        i = pl.program_id(0)

        @pl.when(i < CTX_TILES)
        def _():
            rest[0][...] = xn

        @pl.when(i >= CTX_TILES)
        def _():
            rest[1][...] = xn
    else:
        rest[0][...] = xn


def resid_norm_mod(x, y, gains, mods, *, layer, ga, gate, nxt, split_out=False):
    in_specs = _x_specs(x) + [_ROWS_SPEC, _gain_spec(layer, ga), _mod_spec(gate)]
    args = _x_args(x) + [y, gains, mods[layer]]
    if split_out:
        out_specs = [_CTX_ROWS_SPEC, _LAT_ROWS_SPEC]
        out_shape = [jax.ShapeDtypeStruct((N_CTX, D_MODEL), F32), jax.ShapeDtypeStruct((N_LAT, D_MODEL), F32)]
    else:
        out_specs = [_ROWS_SPEC]
        out_shape = [jax.ShapeDtypeStruct((N_TOK, D_MODEL), F32)]
    if nxt is not None:
        nl, ng, nsh, nsc = nxt
        in_specs += [_gain_spec(nl, ng), _mod_spec(nsh), _mod_spec(nsc)]
        args += [gains, mods[nl], mods[nl]]
        out_specs.append(_ROWS_SPEC)
        out_shape.append(jax.ShapeDtypeStruct((N_TOK, D_MODEL), BF16))
    res = pl.pallas_call(
        functools.partial(_resid_kernel, n_x=len(_x_args(x)), with_h=nxt is not None, split_out=split_out),
        grid=(N_TOK // ROW_TILE,),
        in_specs=in_specs,
        out_specs=out_specs,
        out_shape=out_shape,
        compiler_params=_cparams(("arbitrary",) if split_out else ("parallel",)),
        name="resid_norm_mod",
    )(*args)
    x_new = (res[0], res[1]) if split_out else res[0]
    return x_new, (res[-1] if nxt is not None else None)


def _head_rms(x, g):
    return x * lax.rsqrt(jnp.mean(x * x, axis=-1, keepdims=True) + EPS) * g


def _rope(x, cos, sin):
    lane = lax.broadcasted_iota(jnp.int32, x.shape, 1)
    partner = jnp.where((lane & 63) < 32, pltpu.roll(x, LANES - 32, 1), pltpu.roll(x, 32, 1))
    return x * cos + partner * sin


def _rope_tables():
    t = jnp.arange(DEC_SEQ)
    half = HEAD_DIM // 2
    inv_freq = 1.0 / (ROPE_THETA ** (jnp.arange(0, half, 2, dtype=F32) / half))
    ang_r = (t // GRID_W).astype(F32)[:, None] * inv_freq
    ang_c = (t % GRID_W).astype(F32)[:, None] * inv_freq
    cos = jnp.concatenate([jnp.cos(ang_r)] * 2 + [jnp.cos(ang_c)] * 2, axis=-1)
    sin = jnp.concatenate([-jnp.sin(ang_r), jnp.sin(ang_r), -jnp.sin(ang_c), jnp.sin(ang_c)], axis=-1)
    return cos, sin


def _softmax_pv(scores, values, exp=jnp.exp):
    m = functools.reduce(jnp.maximum, [jnp.max(s, axis=-1, keepdims=True) for s in scores])
    es = [exp(s - m) for s in scores]
    l = functools.reduce(jnp.add, [jnp.sum(e, axis=-1, keepdims=True) for e in es])
    o = functools.reduce(jnp.add, [_dot(e.astype(BF16), v) for e, v in zip(es, values)])
    return o / l


def _ctx_attn_kernel(q_ref, k_ref, v_ref, *rest, n_kv, group, normed):
    if normed:
        qg_ref, kg_ref, o_ref, ko_ref, vo_ref = rest
    else:
        o_ref, ko_ref, vo_ref = rest
    scale = HEAD_DIM ** -0.5
    for j in range(n_kv):
        ks = slice(j * HEAD_DIM, (j + 1) * HEAD_DIM)
        k = k_ref[:, ks]
        v = v_ref[:, ks]
        if normed:
            k = _head_rms(k, kg_ref[...])
        ko_ref[:, ks] = k
        vo_ref[:, ks] = v
        kb = k.astype(BF16)
        vb = v.astype(BF16)
        for g in range(group):
            qs = slice((j * group + g) * HEAD_DIM, (j * group + g + 1) * HEAD_DIM)
            q = q_ref[:, qs]
            if normed:
                q = _head_rms(q, qg_ref[...])
            s = _dot_nt(q.astype(BF16), kb) * scale
            o_ref[:, qs] = _softmax_pv([s], [vb]).astype(o_ref.dtype)


CTX_COLS = 1024


def na_context_attn(qkv):
    nb = D_MODEL // CTX_COLS
    blk = pl.BlockSpec((SEQ, CTX_COLS), lambda b, j: (b, j))
    kv_shape = jax.ShapeDtypeStruct((N_CTX, D_MODEL), F32)
    return pl.pallas_call(
        functools.partial(_ctx_attn_kernel, n_kv=CTX_COLS // HEAD_DIM, group=1, normed=False),
        grid=(BATCH, nb),
        in_specs=[blk,
                  pl.BlockSpec((SEQ, CTX_COLS), lambda b, j: (b, nb + j)),
                  pl.BlockSpec((SEQ, CTX_COLS), lambda b, j: (b, 2 * nb + j))],
        out_specs=[blk, blk, blk],
        out_shape=[jax.ShapeDtypeStruct((N_CTX, D_MODEL), BF16), kv_shape, kv_shape],
        compiler_params=_cparams(("parallel", "arbitrary")),
        name="na_ctx_attn",
    )(qkv, qkv, qkv)


def gqa_context_attn(qkv, q_gain, k_gain):
    n_kv = CTX_COLS // (GQA_GROUP * HEAD_DIM)
    kv_cols = n_kv * HEAD_DIM
    nb = D_MODEL // CTX_COLS
    k0 = D_MODEL // kv_cols
    v0 = k0 + GQA_KV_HEADS * HEAD_DIM // kv_cols
    kv_shape = jax.ShapeDtypeStruct((N_CTX, GQA_KV_HEADS * HEAD_DIM), F32)
    gain_spec = pl.BlockSpec((1, HEAD_DIM), lambda b, j: (0, 0))
    return pl.pallas_call(
        functools.partial(_ctx_attn_kernel, n_kv=n_kv, group=GQA_GROUP, normed=True),
        grid=(BATCH, nb),
        in_specs=[pl.BlockSpec((SEQ, CTX_COLS), lambda b, j: (b, j)),
                  pl.BlockSpec((SEQ, kv_cols), lambda b, j: (b, k0 + j)),
                  pl.BlockSpec((SEQ, kv_cols), lambda b, j: (b, v0 + j)),
                  gain_spec, gain_spec],
        out_specs=[pl.BlockSpec((SEQ, CTX_COLS), lambda b, j: (b, j)),
                   pl.BlockSpec((SEQ, kv_cols), lambda b, j: (b, j)),
                   pl.BlockSpec((SEQ, kv_cols), lambda b, j: (b, j))],
        out_shape=[jax.ShapeDtypeStruct((N_CTX, D_MODEL), BF16), kv_shape, kv_shape],
        compiler_params=_cparams(("parallel", "arbitrary")),
        name="gqa_ctx_attn",
    )(qkv, qkv, qkv, q_gain.reshape(1, HEAD_DIM), k_gain.reshape(1, HEAD_DIM))


def _gqa_lat_kernel(q_ref, k_ref, v_ref, kc_ref, vc_ref, cos_ref, sin_ref, qg_ref, kg_ref, o_ref):
    q_scale = HEAD_DIM ** -0.5 * LOG2E
    cos, sin = cos_ref[...], sin_ref[...]
    kc = kc_ref[...].astype(BF16)
    vc = vc_ref[...].astype(BF16)
    k = _rope(_head_rms(k_ref[...], kg_ref[...]), cos, sin).astype(BF16)
    v = v_ref[...].astype(BF16)
    for g in range(GQA_GROUP):
        qs = slice(g * HEAD_DIM, (g + 1) * HEAD_DIM)
        q = (_rope(_head_rms(q_ref[:, qs], qg_ref[...]), cos, sin) * q_scale).astype(BF16)
        scores = [_dot_nt(q, kc), _dot_nt(q, k)]
        o_ref[:, qs] = _softmax_pv(scores, [vc, v], exp=jnp.exp2).astype(o_ref.dtype)


def gqa_latent_attn(qkv, cache_k, cache_v, q_gain, k_gain):
    cos, sin = _rope_tables()
    qcols = GQA_GROUP * HEAD_DIM
    rb0 = N_CTX // DEC_SEQ
    k0 = D_MODEL // HEAD_DIM
    v0 = k0 + GQA_KV_HEADS
    cache_spec = pl.BlockSpec((None, PAST_LEN, HEAD_DIM), lambda b, j: (b, 0, j))
    table_spec = pl.BlockSpec((DEC_SEQ, HEAD_DIM), lambda b, j: (0, 0))
    gain_spec = pl.BlockSpec((1, HEAD_DIM), lambda b, j: (0, 0))
    return pl.pallas_call(
        _gqa_lat_kernel,
        grid=(DEC_BATCH, GQA_KV_HEADS),
        in_specs=[pl.BlockSpec((DEC_SEQ, qcols), lambda b, j: (rb0 + b, j)),
                  pl.BlockSpec((DEC_SEQ, HEAD_DIM), lambda b, j: (rb0 + b, k0 + j)),
                  pl.BlockSpec((DEC_SEQ, HEAD_DIM), lambda b, j: (rb0 + b, v0 + j)),
                  cache_spec, cache_spec, table_spec, table_spec, gain_spec, gain_spec],
        out_specs=pl.BlockSpec((DEC_SEQ, qcols), lambda b, j: (b, j)),
        out_shape=jax.ShapeDtypeStruct((N_LAT, D_MODEL), BF16),
        compiler_params=_cparams(("parallel", "arbitrary")),
        name="gqa_lat_attn",
    )(qkv, qkv, qkv, cache_k, cache_v, cos, sin, q_gain.reshape(1, HEAD_DIM), k_gain.reshape(1, HEAD_DIM))


NA_ROWS = DEC_SEQ // GRID_W
NA_WR = min(WIN_R, NA_ROWS)
NA_WKEYS = NA_WR * GRID_W


NA_REL_ROWS = 2 * WIN_R - 1
NA_ROW_PAIRS = NA_WR // 2


def _na_bias_table(rpb):
    col = jnp.arange(GRID_W)
    cs = jnp.clip(col - WIN_C // 2, 0, GRID_W - WIN_C)
    col_ok = (col[None, :] >= cs[:, None]) & (col[None, :] < cs[:, None] + WIN_C)
    dc = jnp.clip(col[None, :] - col[:, None] + WIN_C - 1, 0, 2 * WIN_C - 2)
    blocks = jnp.where(col_ok[None, None], rpb[:, :, dc], NEG_BIG).astype(F32)
    return jnp.concatenate([blocks[:, :-1], blocks[:, 1:]], axis=-1)


def _na_lat_kernel(q_ref, k_ref, v_ref, kc_ref, vc_ref, bias_ref, o_ref,
                   kb_ref, vb_ref, slat_ref, elat_ref, olat_ref):
    scale = HEAD_DIM ** -0.5
    kb_ref[...] = k_ref[...].astype(BF16)
    vb_ref[...] = v_ref[...].astype(BF16)

    def window(r):
        rs = min(max(r - NA_WR // 2, 0), NA_ROWS - NA_WR)
        return (slice(r * GRID_W, (r + 1) * GRID_W), slice(rs * GRID_W, rs * GRID_W + NA_WKEYS),
                rs - r + WIN_R - 1)

    for r in range(NA_ROWS):
        q_rows, k_rows, rel0 = window(r)
        bias = jnp.concatenate([bias_ref[rel0 + 2 * p] for p in range(NA_ROW_PAIRS)], axis=1)
        slat_ref[q_rows, :] = _dot_nt(q_ref[q_rows, :].astype(BF16), kb_ref[k_rows, :]) * scale + bias
    s_lat = slat_ref[...]
    s_ctx = _dot_nt(q_ref[...].astype(BF16), kc_ref[...].astype(BF16)) * scale
    m = jnp.maximum(jnp.max(s_lat, axis=-1, keepdims=True), jnp.max(s_ctx, axis=-1, keepdims=True))
    e_lat = jnp.exp(s_lat - m)
    e_ctx = jnp.exp(s_ctx - m)
    l = jnp.sum(e_lat, axis=-1, keepdims=True) + jnp.sum(e_ctx, axis=-1, keepdims=True)
    elat_ref[...] = e_lat.astype(BF16)
    o_ctx = _dot(e_ctx.astype(BF16), vc_ref[...].astype(BF16))
    for r in range(NA_ROWS):
        q_rows, k_rows, _ = window(r)
        olat_ref[q_rows, :] = _dot(elat_ref[q_rows, :], vb_ref[k_rows, :])
    o_ref[...] = ((olat_ref[...] + o_ctx) / l).astype(o_ref.dtype)


def na_latent_attn(qkv, cache_k, cache_v, rpb):
    bias = _na_bias_table(rpb)
    rb0 = N_CTX // DEC_SEQ
    cache_spec = pl.BlockSpec((None, PAST_LEN, HEAD_DIM), lambda h, b: (b, 0, h))
    return pl.pallas_call(
        _na_lat_kernel,
        grid=(NA_HEADS, DEC_BATCH),
        in_specs=[pl.BlockSpec((DEC_SEQ, HEAD_DIM), lambda h, b: (rb0 + b, h)),
                  pl.BlockSpec((DEC_SEQ, HEAD_DIM), lambda h, b: (rb0 + b, NA_HEADS + h)),
                  pl.BlockSpec((DEC_SEQ, HEAD_DIM), lambda h, b: (rb0 + b, 2 * NA_HEADS + h)),
                  cache_spec, cache_spec,
                  pl.BlockSpec((None, NA_REL_ROWS - 1, GRID_W, 2 * GRID_W), lambda h, b: (h, 0, 0, 0))],
        out_specs=pl.BlockSpec((DEC_SEQ, HEAD_DIM), lambda h, b: (b, h)),
        out_shape=jax.ShapeDtypeStruct((N_LAT, D_MODEL), BF16),
        scratch_shapes=[pltpu.VMEM((DEC_SEQ, HEAD_DIM), BF16), pltpu.VMEM((DEC_SEQ, HEAD_DIM), BF16),
                        pltpu.VMEM((DEC_SEQ, NA_WKEYS), F32), pltpu.VMEM((DEC_SEQ, NA_WKEYS), BF16),
                        pltpu.VMEM((DEC_SEQ, HEAD_DIM), F32)],
        compiler_params=_cparams(("parallel", "arbitrary")),
        name="na_lat_attn",
    )(qkv, qkv, qkv, cache_k, cache_v, bias)


LRU_ROWS = 2048
LRU_CTX_STEPS = N_CTX // LRU_ROWS
LRU_SEQ_PAD = 8
LRU_PITCH_PAD = 4
LRU_SCAN_ROWS = max(LRU_ROWS // SEQ * (SEQ + LRU_PITCH_PAD), LRU_ROWS // DEC_SEQ * (DEC_SEQ + LRU_PITCH_PAD))


def _lru_body(xr_ref, gb_ref, cw_ref, cb_ref, wg_ref, bg_ref, lam_ref, h0_ref, y_ref, fin_ref,
              af_ref, uf_ref, ab_ref, ub_ref, hf_ref, hb_ref, *, n_seq, seq_len):
    rows = n_seq * seq_len
    halves = [slice(c * LANES, (c + 1) * LANES) for c in range(LRU_BW // LANES)]
    pitch = seq_len + LRU_PITCH_PAD
    seqs = [(slice(s * seq_len, (s + 1) * seq_len), slice(s * pitch, s * pitch + seq_len))
            for s in range(n_seq)]
    x = xr_ref[...]
    t = lax.broadcasted_iota(jnp.int32, (rows, LRU_BW), 0) & (seq_len - 1)
    cw = cw_ref[...]
    xf = (jnp.where(t >= 2, pltpu.roll(x, 2, 0), 0.0) * cw[0:1]
          + jnp.where(t >= 1, pltpu.roll(x, 1, 0), 0.0) * cw[1:2]
          + x * cw[2:3]
          + jnp.where(t < seq_len - 1, pltpu.roll(x, rows - 1, 0), 0.0) * cw[3:4]) + cb_ref[...]
    xb = xf.astype(BF16)
    sp = _softplus(-lam_ref[...])
    xh = 0.5 * xf
    for d, (a_ref, u_ref) in enumerate(((af_ref, uf_ref), (ab_ref, ub_ref))):
        bg = 0.5 * bg_ref[d]
        t_r = jnp.tanh(_dot(xb, (0.5 * wg_ref[d, 0]).astype(BF16)) + bg[0:1])
        t_i = jnp.tanh(_dot(xb, (0.5 * wg_ref[d, 1]).astype(BF16)) + bg[1:2])
        decay = (-0.5 * LRU_C * LOG2E) * sp[d:d + 1]
        a = jnp.exp2(decay * (t_r + 1.0))
        u = jnp.sqrt(1.0 - a * a) * (t_i + 1.0) * xh
        for c, cols in enumerate(halves):
            for src, dst in seqs:
                a_ref[c, dst, :] = a[src, cols]
                u_ref[c, dst, :] = u[src, cols]

    def step(i, carry):
        rf = pl.ds(i, n_seq, stride=pitch)
        rb = pl.ds(seq_len - 1 - i, n_seq, stride=pitch)
        out = []
        for c in range(len(halves)):
            hf = af_ref[c, rf, :] * carry[2 * c] + uf_ref[c, rf, :]
            hf_ref[c, rf, :] = hf
            hb = ab_ref[c, rb, :] * carry[2 * c + 1] + ub_ref[c, rb, :]
            hb_ref[c, rb, :] = hb
            out += [hf, hb]
        return tuple(out)

    init = tuple(h0_ref[d, 0:n_seq, cols] for cols in halves for d in range(2))
    fin = lax.fori_loop(0, seq_len, step, init, unroll=4)
    fin_ref[...] = jnp.zeros(fin_ref.shape, F32)
    gate = jax.nn.gelu(gb_ref[...])
    for c, cols in enumerate(halves):
        fin_ref[0, 0:n_seq, cols] = fin[2 * c]
        fin_ref[1, 0:n_seq, cols] = fin[2 * c + 1]
        for src, dst in seqs:
            h = hf_ref[c, dst, :] + hb_ref[c, dst, :]
            y_ref[src, cols] = (h * gate[src, cols]).astype(y_ref.dtype)


def _lru_kernel(*refs):
    i = pl.program_id(0)

    @pl.when(i < LRU_CTX_STEPS)
    def _():
        _lru_body(*refs, n_seq=LRU_ROWS // SEQ, seq_len=SEQ)

    @pl.when(i >= LRU_CTX_STEPS)
    def _():
        _lru_body(*refs, n_seq=LRU_ROWS // DEC_SEQ, seq_len=DEC_SEQ)


def rglru_mix(proj, state, conv_w, conv_b, w_gates, b_gates, lam):
    assert LRU_ROWS // SEQ == LRU_SEQ_PAD and LRU_ROWS // DEC_SEQ == DEC_BATCH
    n_steps = N_TOK // LRU_ROWS
    h0 = jnp.zeros((2, n_steps * LRU_SEQ_PAD, D_RNN), F32)
    h0 = h0.at[:, BATCH:BATCH + DEC_BATCH].set(state.transpose(1, 0, 2))
    blk = lambda i, n: (i, n)
    chan = lambda i, n: (0, n)
    y, fin = pl.pallas_call(
        _lru_kernel,
        grid=(n_steps, LRU_BLOCKS),
        in_specs=[pl.BlockSpec((LRU_ROWS, LRU_BW), blk),
                  pl.BlockSpec((LRU_ROWS, LRU_BW), lambda i, n: (i, LRU_BLOCKS + n)),
                  pl.BlockSpec((CONV_W, LRU_BW), chan),
                  pl.BlockSpec((1, LRU_BW), chan),
                  pl.BlockSpec((2, 2, None, LRU_BW, LRU_BW), lambda i, n: (0, 0, n, 0, 0)),
                  pl.BlockSpec((2, 2, LRU_BW), lambda i, n: (0, 0, n)),
                  pl.BlockSpec((2, LRU_BW), chan),
                  pl.BlockSpec((2, LRU_SEQ_PAD, LRU_BW), lambda i, n: (0, i, n))],
        out_specs=[pl.BlockSpec((LRU_ROWS, LRU_BW), blk),
                   pl.BlockSpec((2, LRU_SEQ_PAD, LRU_BW), lambda i, n: (0, i, n))],
        out_shape=[jax.ShapeDtypeStruct((N_TOK, D_RNN), BF16),
                   jax.ShapeDtypeStruct((2, n_steps * LRU_SEQ_PAD, D_RNN), F32)],
        scratch_shapes=[pltpu.VMEM((LRU_BW // LANES, LRU_SCAN_ROWS, LANES), F32)] * 6,
        compiler_params=_cparams(("parallel", "arbitrary")),
        name="rglru_mix",
    )(proj, proj, conv_w, conv_b.reshape(1, D_RNN), w_gates, b_gates, lam, h0)
    return y, fin[:, :BATCH].transpose(1, 0, 2)


ML_GATE_COLS = 4 * ML_HEADS


def _pick_lane(x, idx):
    lane = lax.broadcasted_iota(jnp.int32, x.shape, 1)
    return jnp.sum(jnp.where(lane == idx, x, 0.0), axis=1, keepdims=True)


def _pick_row(x, idx):
    row = lax.broadcasted_iota(jnp.int32, x.shape, 0)
    return jnp.sum(jnp.where(row == idx, x, 0.0), axis=0, keepdims=True)


def _mlstm_kernel(q_ref, k_ref, v_ref, og_ref, g_ref, gt_ref, bg_ref, bgt_ref, ng_ref, *rest,
                  seq_len, zero_init, with_state):
    rest = list(rest)
    if not zero_init:
        c0_ref, n0_ref, m0_ref = rest[:3]
        rest = rest[3:]
    y_ref = rest.pop(0)
    if with_state:
        co_ref, no_ref, mo_ref = rest[:3]
        rest = rest[3:]
    (h_ref, c_ref, n_ref, dmat_ref, qk_ref, bcum_ref, rmax_ref, dec_ref, blast_ref, dmax_ref) = rest
    head = pl.program_id(1)
    n_chunks = seq_len // ML_CHUNK
    k_scale = ML_DK ** -0.5
    sub = lax.broadcasted_iota(jnp.int32, (ML_CHUNK, ML_CHUNK), 0)
    lane = lax.broadcasted_iota(jnp.int32, (ML_CHUNK, ML_CHUNK), 1)

    if zero_init:
        c_ref[...] = jnp.zeros(c_ref.shape, F32)
        n_ref[...] = jnp.zeros(n_ref.shape, F32)
        m_init = (jnp.zeros((1, 1), F32),) * 2
    else:
        c_ref[...] = c0_ref[...]
        n_ref[...] = n0_ref[...]
        m_init = (m0_ref[0], m0_ref[1])

    for c in range(n_chunks):
        rows = slice(c * ML_CHUNK, (c + 1) * ML_CHUNK)
        g = g_ref[rows, :] + bg_ref[...]
        gt = gt_ref[:, rows] + bgt_ref[...]
        qk_ref[c] = _dot_nt(q_ref[rows, :].astype(BF16), (k_ref[rows, :] * k_scale).astype(BF16))
        for d in range(2):
            valid = (lane <= sub) if d == 0 else (lane >= sub)
            valid_t = (sub <= lane) if d == 0 else (sub >= lane)
            i_col = d * 2 * ML_HEADS + head
            f_col = i_col + ML_HEADS
            li_col = _pick_lane(g, i_col)
            lf_col = -_softplus(-_pick_lane(g, f_col))
            li_row = _pick_row(gt, i_col)
            lf_row = -_softplus(-_pick_row(gt, f_col))
            bcum_col = jnp.sum(jnp.where(valid, lf_row, 0.0), axis=1, keepdims=True)
            bcum_row = jnp.sum(jnp.where(valid_t, lf_col, 0.0), axis=0, keepdims=True)
            b_last = jnp.sum(lf_row, axis=1, keepdims=True)
            dmat = jnp.where(valid, bcum_col - bcum_row + li_row, NEG_BIG)
            dec = b_last - bcum_col + li_col
            idx = d * n_chunks + c
            dmat_ref[idx] = dmat
            rmax_ref[idx] = jnp.max(dmat, axis=1, keepdims=True)
            bcum_ref[idx] = bcum_col
            dec_ref[idx] = dec
            blast_ref[idx] = b_last
            dmax_ref[idx] = jnp.max(dec, axis=0, keepdims=True)

    def one_chunk(d, c_idx, m_prev):
        idx = d * n_chunks + c_idx
        rows = pl.ds(pl.multiple_of(c_idx * ML_CHUNK, ML_CHUNK), ML_CHUNK)
        m_inter = bcum_ref[idx] + m_prev
        m_t = jnp.maximum(m_inter, rmax_ref[idx])
        qf = q_ref[rows, :]
        qb = qf.astype(BF16)
        kf = k_ref[rows, :] * k_scale
        vb = v_ref[rows, :].astype(BF16)
        s = qk_ref[c_idx] * jnp.exp(dmat_ref[idx] - m_t)
        inter = jnp.exp(m_inter - m_t)
        num = _dot(s.astype(BF16), vb) + inter * _dot(qb, c_ref[d].astype(BF16))
        den = (jnp.sum(s, axis=1, keepdims=True)
               + inter * jnp.sum(qf * n_ref[d], axis=1, keepdims=True))
        h_ref[d, rows, :] = num / jnp.maximum(jnp.abs(den), jnp.exp(-m_t))
        b_last = blast_ref[idx]
        m_new = jnp.maximum(b_last + m_prev, dmax_ref[idx])
        kw = kf * jnp.exp(dec_ref[idx] - m_new)
        carry_scale = jnp.exp(b_last + m_prev - m_new)
        c_ref[d] = carry_scale * c_ref[d] + _dot_tn(kw.astype(BF16), vb)
        n_ref[d] = carry_scale * n_ref[d] + jnp.sum(kw, axis=0, keepdims=True)
        return m_new

    def both(ci, m_prev):
        return one_chunk(0, ci, m_prev[0]), one_chunk(1, n_chunks - 1 - ci, m_prev[1])

    m_fin = lax.fori_loop(0, n_chunks, both, m_init)
    if with_state:
        co_ref[...] = c_ref[...]
        no_ref[...] = n_ref[...]
        mo_ref[0] = m_fin[0]
        mo_ref[1] = m_fin[1]

    hs = h_ref[0] + h_ref[1]
    hn = hs * lax.rsqrt(jnp.mean(hs * hs, axis=-1, keepdims=True) + EPS) * ng_ref[...]
    y_ref[...] = (_sigmoid(og_ref[...]) * hn).astype(y_ref.dtype)


def mlstm_mix(proj, gates, gates_t, b_gates, norm_g, *, row0, n_seq, seq_len, state=None, with_state):
    rb0 = row0 // seq_len
    kq = ML_HEADS * ML_DK // ML_DK
    v0 = 2 * ML_HEADS * ML_DK // ML_DV
    o0 = v0 + ML_HEADS
    zero_init = state is None
    n_chunks = seq_len // ML_CHUNK
    bias = jnp.pad(b_gates.reshape(1, ML_GATE_COLS), ((0, 0), (0, LANES - ML_GATE_COLS)))
    in_specs = [pl.BlockSpec((seq_len, ML_DK), lambda b, h: (rb0 + b, h)),
                pl.BlockSpec((seq_len, ML_DK), lambda b, h: (rb0 + b, kq + h)),
                pl.BlockSpec((seq_len, ML_DV), lambda b, h: (rb0 + b, v0 + h)),
                pl.BlockSpec((seq_len, ML_DV), lambda b, h: (rb0 + b, o0 + h)),
                pl.BlockSpec((seq_len, LANES), lambda b, h: (rb0 + b, 0)),
                pl.BlockSpec((ML_GATE_COLS, seq_len), lambda b, h: (0, rb0 + b)),
                pl.BlockSpec((1, LANES), lambda b, h: (0, 0)),
                pl.BlockSpec((ML_GATE_COLS, 1), lambda b, h: (0, 0)),
                pl.BlockSpec((1, ML_DV), lambda b, h: (0, h))]
    args = [proj, proj, proj, proj, gates, gates_t, bias, b_gates.reshape(ML_GATE_COLS, 1),
            norm_g.reshape(1, ML_HEADS * ML_DV)]
    c_spec = pl.BlockSpec((None, 2, None, ML_DK, ML_DV), lambda b, h: (b, 0, h, 0, 0))
    n_spec = pl.BlockSpec((None, 2, None, 1, ML_DK), lambda b, h: (b, 0, h, 0, 0))
    m_spec = pl.BlockSpec((None, 2, None, 1, 1), lambda b, h: (b, 0, h, 0, 0))
    if not zero_init:
        c0, n0, m0 = state
        in_specs += [c_spec, n_spec, m_spec]
        args += [c0, n0.reshape(n_seq, 2, ML_HEADS, 1, ML_DK), m0.reshape(n_seq, 2, ML_HEADS, 1, 1)]
    out_specs = [pl.BlockSpec((seq_len, ML_DV), lambda b, h: (b, h))]
    out_shape = [jax.ShapeDtypeStruct((n_seq * seq_len, ML_HEADS * ML_DV), BF16)]
    if with_state:
        out_specs += [c_spec, n_spec, m_spec]
        out_shape += [jax.ShapeDtypeStruct((n_seq, 2, ML_HEADS, ML_DK, ML_DV), F32),
                      jax.ShapeDtypeStruct((n_seq, 2, ML_HEADS, 1, ML_DK), F32),
                      jax.ShapeDtypeStruct((n_seq, 2, ML_HEADS, 1, 1), F32)]
    res = pl.pallas_call(
        functools.partial(_mlstm_kernel, seq_len=seq_len, zero_init=zero_init, with_state=with_state),
        grid=(n_seq, ML_HEADS),
        in_specs=in_specs,
        out_specs=out_specs,
        out_shape=out_shape,
        scratch_shapes=[pltpu.VMEM((2, seq_len, ML_DV), F32), pltpu.VMEM((2, ML_DK, ML_DV), F32),
                        pltpu.VMEM((2, 1, ML_DK), F32),
                        pltpu.VMEM((2 * n_chunks, ML_CHUNK, ML_CHUNK), F32),
                        pltpu.VMEM((n_chunks, ML_CHUNK, ML_CHUNK), F32)]
                       + [pltpu.VMEM((2 * n_chunks, ML_CHUNK, 1), F32)] * 3
                       + [pltpu.VMEM((2 * n_chunks, 1, 1), F32)] * 2,
        compiler_params=_cparams(("parallel", "arbitrary")),
        name="mlstm_mix",
    )(*args)
    if not with_state:
        return res[0]
    y, c_fin, n_fin, m_fin = res
    return y, c_fin, n_fin.reshape(n_seq, 2, ML_HEADS, ML_DK), m_fin.reshape(n_seq, 2, ML_HEADS)


def kernel(x_prompt, x_sample, cache_na_k, cache_na_v, state_lru, state_mlstm_C, state_mlstm_n, state_mlstm_m, cache_gqa_k, cache_gqa_v, c, c_ctx, ada_w, ada_b, norm_g, mlp_up, mlp_down, na_w_qkv, na_rpb, na_w_o, lru_w_in, lru_conv_w, lru_conv_b, lru_w_gates, lru_b_gates, lru_lambda, lru_w_o, ml_w_in, ml_w_gates, ml_b_gates, ml_norm_g, ml_w_o, gqa_w_qkv, gqa_q_norm, gqa_k_norm, gqa_w_o):
    x = (x_prompt.reshape(N_CTX, D_MODEL), x_sample.reshape(N_LAT, D_MODEL))
    cond = jnp.concatenate([c_ctx[None], c, jnp.zeros((MOD_ROWS - 1 - DEC_BATCH, D_MODEL), F32)], axis=0)
    act = silu_rows(cond)
    mods = [adaln_layer(act, ada_w, ada_b, 0)]
    act_b = jnp.broadcast_to(act[:N_COND, :, None], (N_COND, D_MODEL, LANES))
    gains = norm_g.reshape(DEPTH, 4, 1, D_MODEL)

    outs = {}
    h = norm_mod(x, gains, mods, 0)
    for i in range(DEPTH):
        kind = i % 4
        if kind == 0:
            qkv = matmul(h, na_w_qkv[0], name="na_qkv", **WIDE)
            shp = (BATCH, 1, SEQ, NA_HEADS, HEAD_DIM)
            yp, kp, vp = na_context_attn(qkv)
            outs['na_k'], outs['na_v'] = kp.reshape(shp), vp.reshape(shp)
            yl = na_latent_attn(qkv, cache_na_k[:, 0].reshape(DEC_BATCH, PAST_LEN, D_MODEL),
                                cache_na_v[:, 0].reshape(DEC_BATCH, PAST_LEN, D_MODEL), na_rpb[0])
            y = matmul(yp, na_w_o[0], a2=yl, name="na_out")
        elif kind == 1:
            proj = matmul(h, lru_w_in[0], name="lru_in", **WIDE)
            pre, fin = rglru_mix(proj, state_lru[:, 0], lru_conv_w[0], lru_conv_b[0], lru_w_gates[0],
                                 lru_b_gates[0], lru_lambda[0])
            outs['lru'] = fin[:, None]
            y = matmul(pre, lru_w_o[0], name="lru_out")
        elif kind == 2:
            proj = matmul(h, ml_w_in[0], name="ml_in", **WIDE)
            wg = jnp.pad(ml_w_gates[0], ((0, 0), (0, LANES - ML_GATE_COLS)))
            gates = matmul(h, wg, tn=LANES, name="ml_gates")
            gates_t = gates[:, :ML_GATE_COLS].T
            yp, cp, np_, mp = mlstm_mix(proj, gates, gates_t, ml_b_gates[0], ml_norm_g[0], row0=0,
                                        n_seq=BATCH, seq_len=SEQ, with_state=True)
            yl = mlstm_mix(proj, gates, gates_t, ml_b_gates[0], ml_norm_g[0], row0=N_CTX,
                           n_seq=DEC_BATCH, seq_len=DEC_SEQ, with_state=False,
                           state=(state_mlstm_C[:, 0], state_mlstm_n[:, 0], state_mlstm_m[:, 0]))
            outs['mc'], outs['mn'], outs['mm'] = cp[:, None], np_[:, None], mp[:, None]
            y = matmul(yp, ml_w_o[0], a2=yl, name="ml_out")
        else:
            qkv = matmul(h, gqa_w_qkv[0], name="gqa_qkv", **WIDE)
            kv_cols = GQA_KV_HEADS * HEAD_DIM
            yp, kp, vp = gqa_context_attn(qkv, gqa_q_norm[0], gqa_k_norm[0])
            yl = gqa_latent_attn(qkv, cache_gqa_k[:, 0].reshape(DEC_BATCH, PAST_LEN, kv_cols),
                                 cache_gqa_v[:, 0].reshape(DEC_BATCH, PAST_LEN, kv_cols),
                                 gqa_q_norm[0], gqa_k_norm[0])
            shp = (BATCH, 1, SEQ, GQA_KV_HEADS, HEAD_DIM)
            outs['gk'], outs['gv'] = kp.reshape(shp), vp.reshape(shp)
            y = matmul(yp, gqa_w_o[0], a2=yl, name="gqa_out")

        x, h2 = resid_norm_mod(x, y, gains, mods, layer=i, ga=1, gate=2, nxt=(i, 2, 3, 4))
        last = i + 1 == DEPTH
        if last:
            u = matmul(h2, mlp_up, layer=i, out_dtype=BF16, act="relu2", name="mlp_up", **WIDE)
        else:
            u, mod = matmul(h2, mlp_up, layer=i, out_dtype=BF16, act="relu2", name="mlp_up_ada",
                            ada=(ada_w, ada_b, act_b, i + 1))
            mods.append(mod.reshape(MOD_ROWS, 6, 1, D_MODEL))
        z = matmul(u, mlp_down, layer=i, tm=2048, tn=1024, tk=1024, name="mlp_down")
        nxt = None if last else (i + 1, 0, 0, 1)
        x, h = resid_norm_mod(x, z, gains, mods, layer=i, ga=3, gate=5, nxt=nxt, split_out=last)

    return (x[0].reshape(BATCH, SEQ, D_MODEL), x[1].reshape(DEC_BATCH, DEC_SEQ, D_MODEL),
            outs['na_k'], outs['na_v'], outs['lru'], outs['mc'], outs['mn'], outs['mm'], outs['gk'], outs['gv'])
```

```python
import functools

import jax
import jax.numpy as jnp
from jax import lax
from jax.experimental import pallas as pl
from jax.experimental.pallas import tpu as pltpu

D_MODEL = 4096
BATCH = 16
SEQ = 256
DEPTH = 4
DEC_BATCH = 2
DEC_SEQ = 1024
PAST_LEN = 512
GRID_W = 64
HEAD_DIM = 128
NA_HEADS = 32
WIN_R = 8
WIN_C = 16
GQA_HEADS = 32
GQA_KV_HEADS = 8
GQA_GROUP = GQA_HEADS // GQA_KV_HEADS
ROPE_THETA = 10000.0
D_RNN = D_MODEL
LRU_BLOCKS = 16
LRU_BW = 256
CONV_W = 4
LRU_C = 8.0
ML_HEADS = 8
ML_DK = 256
ML_DV = 512
ML_CHUNK = 128
D_FF = 4 * D_MODEL
EPS = 1e-6

N_CTX = BATCH * SEQ
N_LAT = DEC_BATCH * DEC_SEQ
N_TOK = N_CTX + N_LAT
MOD_ROWS = 8
LANES = 128
NEG_BIG = -1e30
LOG2E = 1.4426950408889634

VMEM_LIMIT = 56 * 1024 * 1024

BF16 = jnp.bfloat16
F32 = jnp.float32


def _cparams(sem):
    return pltpu.CompilerParams(dimension_semantics=sem, vmem_limit_bytes=VMEM_LIMIT)


def _dot(a, b):
    return jnp.dot(a, b, preferred_element_type=F32)


def _dot_nt(a, b):
    return lax.dot_general(a, b, (((1,), (1,)), ((), ())), preferred_element_type=F32)


def _dot_tn(a, b):
    return lax.dot_general(a, b, (((0,), (0,)), ((), ())), preferred_element_type=F32)


def _sigmoid(x):
    return 0.5 * (jnp.tanh(0.5 * x) + 1.0)


def _softplus(x):
    return jnp.maximum(x, 0.0) + jnp.log1p(jnp.exp(-jnp.abs(x)))


def _mm_kernel(*refs, act, n_first, n_split, with_ada):
    if with_ada:
        _ada_slab(*refs[-5:-2], refs[-1])
        refs = refs[:-5] + (refs[-2],)
    w_ref, o_ref = refs[-2:]
    slab = w_ref.shape[-1] // n_split

    def tile(a_ref):
        a = a_ref[...]
        for c in range(n_split):
            cols = slice(c * slab, (c + 1) * slab)
            acc = _dot(a, w_ref[:, cols].astype(BF16))
            if act == "relu2":
                r = jnp.maximum(acc, 0.0)
                acc = r * r
            o_ref[:, cols] = acc.astype(o_ref.dtype)

    if len(refs) == 3:
        tile(refs[0])
    else:
        i = pl.program_id(0)
        pl.when(i < n_first)(lambda: tile(refs[0]))
        pl.when(i >= n_first)(lambda: tile(refs[1]))


def _mm_kernel_kgrid(a_ref, w_ref, o_ref):
    k = pl.program_id(2)

    @pl.when(k == 0)
    def _():
        o_ref[...] = _dot(a_ref[...], w_ref[...].astype(BF16))

    @pl.when(k > 0)
    def _():
        o_ref[...] += _dot(a_ref[...], w_ref[...].astype(BF16))


def matmul(a, w, *, layer=0, a2=None, out_dtype=F32, act=None, tm=1024, tn=512, tk=4096, n_split=1,
           a_buffers=2, ada=None, name="matmul"):
    m1, k = a.shape
    m = m1 + (0 if a2 is None else a2.shape[0])
    k2, n = w.shape[-2:]
    assert k == k2 and a.dtype == BF16
    lead = () if w.ndim == 2 else (None,)
    at = () if w.ndim == 2 else (layer,)
    tm, tn, tk = min(tm, m), min(tn, n), min(tk, k)
    assert m1 % tm == 0 and m % tm == 0 and n % tn == 0 and k % tk == 0
    nk = k // tk
    n_first = m1 // tm
    if nk == 1:
        a_mode = {} if a_buffers == 2 else {"pipeline_mode": pl.Buffered(a_buffers)}
        a_specs = [pl.BlockSpec((tm, k), lambda i, j: (jnp.minimum(i, n_first - 1), 0), **a_mode)]
        a_args = [a]
        if a2 is not None:
            assert a2.dtype == BF16 and a2.shape[1] == k
            a_specs.append(pl.BlockSpec((tm, k), lambda i, j: (jnp.maximum(i - n_first, 0), 0), **a_mode))
            a_args.append(a2)
        grid = (m // tm, n // tn)
        in_specs = a_specs + [pl.BlockSpec(lead + (k, tn), lambda i, j: at + (0, j))]
        args = a_args + [w]
        out_specs = pl.BlockSpec((tm, tn), lambda i, j: (i, j))
        out_shape = jax.ShapeDtypeStruct((m, n), out_dtype)
        if ada is not None:
            ada_w, ada_b, act_b, ada_layer = ada
            assert grid[0] * grid[1] * LANES == N_MOD
            step = lambda i, j: i * grid[1] + j
            in_specs += [pl.BlockSpec((None, D_MODEL, LANES), lambda i, j: (ada_layer, 0, step(i, j))),
                         pl.BlockSpec((N_COND, D_MODEL, LANES), lambda i, j: (0, 0, 0),
                                      pipeline_mode=pl.Buffered(1)),
                         pl.BlockSpec((None, 1, LANES), lambda i, j: (ada_layer, 0, step(i, j)))]
            args += [ada_w, act_b, ada_b.reshape(DEPTH, 1, N_MOD)]
            out_specs = [out_specs, pl.BlockSpec((MOD_ROWS, LANES), lambda i, j: (0, step(i, j)))]
            out_shape = [out_shape, jax.ShapeDtypeStruct((MOD_ROWS, N_MOD), F32)]
        return pl.pallas_call(
            functools.partial(_mm_kernel, act=act, n_first=n_first, n_split=n_split,
                              with_ada=ada is not None),
            grid=grid,
            in_specs=in_specs,
            out_specs=out_specs,
            out_shape=out_shape,
            compiler_params=_cparams(("parallel", "arbitrary")),
            name=name,
        )(*args)
    assert act is None and a2 is None and out_dtype == F32
    return pl.pallas_call(
        _mm_kernel_kgrid,
        grid=(m // tm, n // tn, nk),
        in_specs=[pl.BlockSpec((tm, tk), lambda i, j, kk: (i, kk)),
                  pl.BlockSpec(lead + (tk, tn), lambda i, j, kk: at + (kk, j))],
        out_specs=pl.BlockSpec((tm, tn), lambda i, j, kk: (i, j)),
        out_shape=jax.ShapeDtypeStruct((m, n), out_dtype),
        compiler_params=_cparams(("parallel", "arbitrary", "arbitrary")),
        name=name,
    )(a, w)


WIDE = dict(tn=1024, n_split=2, a_buffers=1)


N_MOD = 6 * D_MODEL
N_COND = 1 + DEC_BATCH


def _silu_kernel(c_ref, o_ref):
    c = c_ref[...]
    o_ref[...] = c * jax.nn.sigmoid(c)


def silu_rows(cond):
    return pl.pallas_call(
        _silu_kernel,
        out_shape=jax.ShapeDtypeStruct(cond.shape, F32),
        name="silu_cond",
    )(cond)


def _ada_kernel(a_ref, w_ref, b_ref, o_ref):
    o_ref[...] = _dot(a_ref[...].astype(BF16), w_ref[...].astype(BF16)) + b_ref[...]


def adaln_layer(act, ada_w, ada_b, layer, *, tn=512):
    out = pl.pallas_call(
        _ada_kernel,
        grid=(N_MOD // tn,),
        in_specs=[pl.BlockSpec((MOD_ROWS, D_MODEL), lambda j: (0, 0)),
                  pl.BlockSpec((None, D_MODEL, tn), lambda j: (layer, 0, j)),
                  pl.BlockSpec((None, 1, tn), lambda j: (layer, 0, j))],
        out_specs=pl.BlockSpec((MOD_ROWS, tn), lambda j: (0, j)),
        out_shape=jax.ShapeDtypeStruct((MOD_ROWS, N_MOD), F32),
        compiler_params=_cparams(("arbitrary",)),
        name="adaln",
    )(act, ada_w, ada_b.reshape(DEPTH, 1, N_MOD))
    return out.reshape(MOD_ROWS, 6, 1, D_MODEL)


def _ada_slab(adaw_ref, actb_ref, adab_ref, mod_ref):
    sub = 8
    accs = [jnp.zeros((sub, LANES), F32) for _ in range(N_COND)]
    for t in range(D_MODEL // sub):
        rs = slice(t * sub, (t + 1) * sub)
        w = adaw_ref[rs, :]
        for r in range(N_COND):
            accs[r] = accs[r] + w * actb_ref[r, rs, :]
    rows = [jnp.sum(a, axis=0, keepdims=True) for a in accs]
    rows.append(jnp.zeros((MOD_ROWS - N_COND, LANES), F32))
    mod_ref[...] = jnp.concatenate(rows, axis=0) + adab_ref[...]


ROW_TILE = 256


def _row_group(i):
    return jnp.maximum(0, (i * ROW_TILE - N_CTX) // DEC_SEQ + 1)


def _rms(x, g):
    return x * lax.rsqrt(jnp.mean(x * x, axis=-1, keepdims=True) + EPS) * g


def _mod_spec(which):
    return pl.BlockSpec((None, None, 1, D_MODEL), lambda i: (_row_group(i), which, 0, 0))


def _gain_spec(layer, which):
    return pl.BlockSpec((None, None, 1, D_MODEL), lambda i: (layer, which, 0, 0))


_ROWS_SPEC = pl.BlockSpec((ROW_TILE, D_MODEL), lambda i: (i, 0))


CTX_TILES = N_CTX // ROW_TILE
_CTX_ROWS_SPEC = pl.BlockSpec((ROW_TILE, D_MODEL), lambda i: (jnp.minimum(i, CTX_TILES - 1), 0))
_LAT_ROWS_SPEC = pl.BlockSpec((ROW_TILE, D_MODEL), lambda i: (jnp.maximum(i - CTX_TILES, 0), 0))


def _x_specs(x):
    return [_CTX_ROWS_SPEC, _LAT_ROWS_SPEC] if isinstance(x, tuple) else [_ROWS_SPEC]


def _x_args(x):
    return list(x) if isinstance(x, tuple) else [x]


def _load_rows(refs):
    if len(refs) == 1:
        return refs[0][...]
    return jnp.where(pl.program_id(0) < CTX_TILES, refs[0][...], refs[1][...])


def _norm_mod_kernel(*refs):
    g_ref, sh_ref, sc_ref, h_ref = refs[-4:]
    h = _rms(_load_rows(refs[:-4]), g_ref[...]) * (1.0 + sc_ref[...]) + sh_ref[...]
    h_ref[...] = h.astype(h_ref.dtype)


def norm_mod(x, gains, mods, layer):
    return pl.pallas_call(
        _norm_mod_kernel,
        grid=(N_TOK // ROW_TILE,),
        in_specs=_x_specs(x) + [_gain_spec(layer, 0), _mod_spec(0), _mod_spec(1)],
        out_specs=_ROWS_SPEC,
        out_shape=jax.ShapeDtypeStruct((N_TOK, D_MODEL), BF16),
        compiler_params=_cparams(("parallel",)),
        name="norm_mod",
    )(*_x_args(x), gains, mods[layer], mods[layer])


def _resid_kernel(*refs, n_x, with_h, split_out):
    x_refs, (y_ref, ga_ref, gate_ref), rest = refs[:n_x], refs[n_x:n_x + 3], refs[n_x + 3:]
    xn = _load_rows(x_refs) + gate_ref[...] * _rms(y_ref[...], ga_ref[...])
    if with_h:
        gb_ref, sh_ref, sc_ref = rest[:3]
        rest = rest[3:]
        h_ref = rest[-1]
        h = _rms(xn, gb_ref[...]) * (1.0 + sc_ref[...]) + sh_ref[...]
        h_ref[...] = h.astype(h_ref.dtype)
    if split_out:
        i = pl.program_id(0)

        @pl.when(i < CTX_TILES)
        def _():
            rest[0][...] = xn

        @pl.when(i >= CTX_TILES)
        def _():
            rest[1][...] = xn
    else:
        rest[0][...] = xn


def resid_norm_mod(x, y, gains, mods, *, layer, ga, gate, nxt, split_out=False):
    in_specs = _x_specs(x) + [_ROWS_SPEC, _gain_spec(layer, ga), _mod_spec(gate)]
    args = _x_args(x) + [y, gains, mods[layer]]
    if split_out:
        out_specs = [_CTX_ROWS_SPEC, _LAT_ROWS_SPEC]
        out_shape = [jax.ShapeDtypeStruct((N_CTX, D_MODEL), F32), jax.ShapeDtypeStruct((N_LAT, D_MODEL), F32)]
    else:
        out_specs = [_ROWS_SPEC]
        out_shape = [jax.ShapeDtypeStruct((N_TOK, D_MODEL), F32)]
    if nxt is not None:
        nl, ng, nsh, nsc = nxt
        in_specs += [_gain_spec(nl, ng), _mod_spec(nsh), _mod_spec(nsc)]
        args += [gains, mods[nl], mods[nl]]
        out_specs.append(_ROWS_SPEC)
        out_shape.append(jax.ShapeDtypeStruct((N_TOK, D_MODEL), BF16))
    res = pl.pallas_call(
        functools.partial(_resid_kernel, n_x=len(_x_args(x)), with_h=nxt is not None, split_out=split_out),
        grid=(N_TOK // ROW_TILE,),
        in_specs=in_specs,
        out_specs=out_specs,
        out_shape=out_shape,
        compiler_params=_cparams(("arbitrary",) if split_out else ("parallel",)),
        name="resid_norm_mod",
    )(*args)
    x_new = (res[0], res[1]) if split_out else res[0]
    return x_new, (res[-1] if nxt is not None else None)


def _head_rms(x, g):
    return x * lax.rsqrt(jnp.mean(x * x, axis=-1, keepdims=True) + EPS) * g


def _rope(x, cos, sin):
    lane = lax.broadcasted_iota(jnp.int32, x.shape, 1)
    partner = jnp.where((lane & 63) < 32, pltpu.roll(x, LANES - 32, 1), pltpu.roll(x, 32, 1))
    return x * cos + partner * sin


def _rope_tables():
    t = jnp.arange(DEC_SEQ)
    half = HEAD_DIM // 2
    inv_freq = 1.0 / (ROPE_THETA ** (jnp.arange(0, half, 2, dtype=F32) / half))
    ang_r = (t // GRID_W).astype(F32)[:, None] * inv_freq
    ang_c = (t % GRID_W).astype(F32)[:, None] * inv_freq
    cos = jnp.concatenate([jnp.cos(ang_r)] * 2 + [jnp.cos(ang_c)] * 2, axis=-1)
    sin = jnp.concatenate([-jnp.sin(ang_r), jnp.sin(ang_r), -jnp.sin(ang_c), jnp.sin(ang_c)], axis=-1)
    return cos, sin


def _softmax_pv(scores, values, exp=jnp.exp):
    m = functools.reduce(jnp.maximum, [jnp.max(s, axis=-1, keepdims=True) for s in scores])
    es = [exp(s - m) for s in scores]
    l = functools.reduce(jnp.add, [jnp.sum(e, axis=-1, keepdims=True) for e in es])
    o = functools.reduce(jnp.add, [_dot(e.astype(BF16), v) for e, v in zip(es, values)])
    return o / l


def _ctx_attn_kernel(q_ref, k_ref, v_ref, *rest, n_kv, group, normed):
    if normed:
        qg_ref, kg_ref, o_ref, ko_ref, vo_ref = rest
    else:
        o_ref, ko_ref, vo_ref = rest
    scale = HEAD_DIM ** -0.5
    for j in range(n_kv):
        ks = slice(j * HEAD_DIM, (j + 1) * HEAD_DIM)
        k = k_ref[:, ks]
        v = v_ref[:, ks]
        if normed:
            k = _head_rms(k, kg_ref[...])
        ko_ref[:, ks] = k
        vo_ref[:, ks] = v
        kb = k.astype(BF16)
        vb = v.astype(BF16)
        for g in range(group):
            qs = slice((j * group + g) * HEAD_DIM, (j * group + g + 1) * HEAD_DIM)
            q = q_ref[:, qs]
            if normed:
                q = _head_rms(q, qg_ref[...])
            s = _dot_nt(q.astype(BF16), kb) * scale
            o_ref[:, qs] = _softmax_pv([s], [vb]).astype(o_ref.dtype)


CTX_COLS = 1024


def na_context_attn(qkv):
    nb = D_MODEL // CTX_COLS
    blk = pl.BlockSpec((SEQ, CTX_COLS), lambda b, j: (b, j))
    kv_shape = jax.ShapeDtypeStruct((N_CTX, D_MODEL), F32)
    return pl.pallas_call(
        functools.partial(_ctx_attn_kernel, n_kv=CTX_COLS // HEAD_DIM, group=1, normed=False),
        grid=(BATCH, nb),
        in_specs=[blk,
                  pl.BlockSpec((SEQ, CTX_COLS), lambda b, j: (b, nb + j)),
                  pl.BlockSpec((SEQ, CTX_COLS), lambda b, j: (b, 2 * nb + j))],
        out_specs=[blk, blk, blk],
        out_shape=[jax.ShapeDtypeStruct((N_CTX, D_MODEL), BF16), kv_shape, kv_shape],
        compiler_params=_cparams(("parallel", "arbitrary")),
        name="na_ctx_attn",
    )(qkv, qkv, qkv)


def gqa_context_attn(qkv, q_gain, k_gain):
    n_kv = CTX_COLS // (GQA_GROUP * HEAD_DIM)
    kv_cols = n_kv * HEAD_DIM
    nb = D_MODEL // CTX_COLS
    k0 = D_MODEL // kv_cols
    v0 = k0 + GQA_KV_HEADS * HEAD_DIM // kv_cols
    kv_shape = jax.ShapeDtypeStruct((N_CTX, GQA_KV_HEADS * HEAD_DIM), F32)
    gain_spec = pl.BlockSpec((1, HEAD_DIM), lambda b, j: (0, 0))
    return pl.pallas_call(
        functools.partial(_ctx_attn_kernel, n_kv=n_kv, group=GQA_GROUP, normed=True),
        grid=(BATCH, nb),
        in_specs=[pl.BlockSpec((SEQ, CTX_COLS), lambda b, j: (b, j)),
                  pl.BlockSpec((SEQ, kv_cols), lambda b, j: (b, k0 + j)),
                  pl.BlockSpec((SEQ, kv_cols), lambda b, j: (b, v0 + j)),
                  gain_spec, gain_spec],
        out_specs=[pl.BlockSpec((SEQ, CTX_COLS), lambda b, j: (b, j)),
                   pl.BlockSpec((SEQ, kv_cols), lambda b, j: (b, j)),
                   pl.BlockSpec((SEQ, kv_cols), lambda b, j: (b, j))],
        out_shape=[jax.ShapeDtypeStruct((N_CTX, D_MODEL), BF16), kv_shape, kv_shape],
        compiler_params=_cparams(("parallel", "arbitrary")),
        name="gqa_ctx_attn",
    )(qkv, qkv, qkv, q_gain.reshape(1, HEAD_DIM), k_gain.reshape(1, HEAD_DIM))


def _gqa_lat_kernel(q_ref, k_ref, v_ref, kc_ref, vc_ref, cos_ref, sin_ref, qg_ref, kg_ref, o_ref):
    q_scale = HEAD_DIM ** -0.5 * LOG2E
    cos, sin = cos_ref[...], sin_ref[...]
    kc = kc_ref[...].astype(BF16)
    vc = vc_ref[...].astype(BF16)
    k = _rope(_head_rms(k_ref[...], kg_ref[...]), cos, sin).astype(BF16)
    v = v_ref[...].astype(BF16)
    for g in range(GQA_GROUP):
        qs = slice(g * HEAD_DIM, (g + 1) * HEAD_DIM)
        q = (_rope(_head_rms(q_ref[:, qs], qg_ref[...]), cos, sin) * q_scale).astype(BF16)
        scores = [_dot_nt(q, kc), _dot_nt(q, k)]
        o_ref[:, qs] = _softmax_pv(scores, [vc, v], exp=jnp.exp2).astype(o_ref.dtype)


def gqa_latent_attn(qkv, cache_k, cache_v, q_gain, k_gain):
    cos, sin = _rope_tables()
    qcols = GQA_GROUP * HEAD_DIM
    rb0 = N_CTX // DEC_SEQ
    k0 = D_MODEL // HEAD_DIM
    v0 = k0 + GQA_KV_HEADS
    cache_spec = pl.BlockSpec((None, PAST_LEN, HEAD_DIM), lambda b, j: (b, 0, j))
    table_spec = pl.BlockSpec((DEC_SEQ, HEAD_DIM), lambda b, j: (0, 0))
    gain_spec = pl.BlockSpec((1, HEAD_DIM), lambda b, j: (0, 0))
    return pl.pallas_call(
        _gqa_lat_kernel,
        grid=(DEC_BATCH, GQA_KV_HEADS),
        in_specs=[pl.BlockSpec((DEC_SEQ, qcols), lambda b, j: (rb0 + b, j)),
                  pl.BlockSpec((DEC_SEQ, HEAD_DIM), lambda b, j: (rb0 + b, k0 + j)),
                  pl.BlockSpec((DEC_SEQ, HEAD_DIM), lambda b, j: (rb0 + b, v0 + j)),
                  cache_spec, cache_spec, table_spec, table_spec, gain_spec, gain_spec],
        out_specs=pl.BlockSpec((DEC_SEQ, qcols), lambda b, j: (b, j)),
        out_shape=jax.ShapeDtypeStruct((N_LAT, D_MODEL), BF16),
        compiler_params=_cparams(("parallel", "arbitrary")),
        name="gqa_lat_attn",
    )(qkv, qkv, qkv, cache_k, cache_v, cos, sin, q_gain.reshape(1, HEAD_DIM), k_gain.reshape(1, HEAD_DIM))


NA_ROWS = DEC_SEQ // GRID_W
NA_WR = min(WIN_R, NA_ROWS)
NA_WKEYS = NA_WR * GRID_W


NA_REL_ROWS = 2 * WIN_R - 1
NA_ROW_PAIRS = NA_WR // 2


def _na_bias_table(rpb):
    col = jnp.arange(GRID_W)
    cs = jnp.clip(col - WIN_C // 2, 0, GRID_W - WIN_C)
    col_ok = (col[None, :] >= cs[:, None]) & (col[None, :] < cs[:, None] + WIN_C)
    dc = jnp.clip(col[None, :] - col[:, None] + WIN_C - 1, 0, 2 * WIN_C - 2)
    blocks = jnp.where(col_ok[None, None], rpb[:, :, dc], NEG_BIG).astype(F32)
    return jnp.concatenate([blocks[:, :-1], blocks[:, 1:]], axis=-1)


def _na_lat_kernel(q_ref, k_ref, v_ref, kc_ref, vc_ref, bias_ref, o_ref,
                   kb_ref, vb_ref, slat_ref, elat_ref, olat_ref):
    scale = HEAD_DIM ** -0.5
    kb_ref[...] = k_ref[...].astype(BF16)
    vb_ref[...] = v_ref[...].astype(BF16)

    def window(r):
        rs = min(max(r - NA_WR // 2, 0), NA_ROWS - NA_WR)
        return (slice(r * GRID_W, (r + 1) * GRID_W), slice(rs * GRID_W, rs * GRID_W + NA_WKEYS),
                rs - r + WIN_R - 1)

    for r in range(NA_ROWS):
        q_rows, k_rows, rel0 = window(r)
        bias = jnp.concatenate([bias_ref[rel0 + 2 * p] for p in range(NA_ROW_PAIRS)], axis=1)
        slat_ref[q_rows, :] = _dot_nt(q_ref[q_rows, :].astype(BF16), kb_ref[k_rows, :]) * scale + bias
    s_lat = slat_ref[...]
    s_ctx = _dot_nt(q_ref[...].astype(BF16), kc_ref[...].astype(BF16)) * scale
    m = jnp.maximum(jnp.max(s_lat, axis=-1, keepdims=True), jnp.max(s_ctx, axis=-1, keepdims=True))
    e_lat = jnp.exp(s_lat - m)
    e_ctx = jnp.exp(s_ctx - m)
    l = jnp.sum(e_lat, axis=-1, keepdims=True) + jnp.sum(e_ctx, axis=-1, keepdims=True)
    elat_ref[...] = e_lat.astype(BF16)
    o_ctx = _dot(e_ctx.astype(BF16), vc_ref[...].astype(BF16))
    for r in range(NA_ROWS):
        q_rows, k_rows, _ = window(r)
        olat_ref[q_rows, :] = _dot(elat_ref[q_rows, :], vb_ref[k_rows, :])
    o_ref[...] = ((olat_ref[...] + o_ctx) / l).astype(o_ref.dtype)


def na_latent_attn(qkv, cache_k, cache_v, rpb):
    bias = _na_bias_table(rpb)
    rb0 = N_CTX // DEC_SEQ
    cache_spec = pl.BlockSpec((None, PAST_LEN, HEAD_DIM), lambda h, b: (b, 0, h))
    return pl.pallas_call(
        _na_lat_kernel,
        grid=(NA_HEADS, DEC_BATCH),
        in_specs=[pl.BlockSpec((DEC_SEQ, HEAD_DIM), lambda h, b: (rb0 + b, h)),
                  pl.BlockSpec((DEC_SEQ, HEAD_DIM), lambda h, b: (rb0 + b, NA_HEADS + h)),
                  pl.BlockSpec((DEC_SEQ, HEAD_DIM), lambda h, b: (rb0 + b, 2 * NA_HEADS + h)),
                  cache_spec, cache_spec,
                  pl.BlockSpec((None, NA_REL_ROWS - 1, GRID_W, 2 * GRID_W), lambda h, b: (h, 0, 0, 0))],
        out_specs=pl.BlockSpec((DEC_SEQ, HEAD_DIM), lambda h, b: (b, h)),
        out_shape=jax.ShapeDtypeStruct((N_LAT, D_MODEL), BF16),
        scratch_shapes=[pltpu.VMEM((DEC_SEQ, HEAD_DIM), BF16), pltpu.VMEM((DEC_SEQ, HEAD_DIM), BF16),
                        pltpu.VMEM((DEC_SEQ, NA_WKEYS), F32), pltpu.VMEM((DEC_SEQ, NA_WKEYS), BF16),
                        pltpu.VMEM((DEC_SEQ, HEAD_DIM), F32)],
        compiler_params=_cparams(("parallel", "arbitrary")),
        name="na_lat_attn",
    )(qkv, qkv, qkv, cache_k, cache_v, bias)


LRU_ROWS = 2048
LRU_CTX_STEPS = N_CTX // LRU_ROWS
LRU_SEQ_PAD = 8
LRU_PITCH_PAD = 4
LRU_SCAN_ROWS = max(LRU_ROWS // SEQ * (SEQ + LRU_PITCH_PAD), LRU_ROWS // DEC_SEQ * (DEC_SEQ + LRU_PITCH_PAD))


def _lru_body(xr_ref, gb_ref, cw_ref, cb_ref, wg_ref, bg_ref, lam_ref, h0_ref, y_ref, fin_ref,
              af_ref, uf_ref, ab_ref, ub_ref, hf_ref, hb_ref, *, n_seq, seq_len):
    rows = n_seq * seq_len
    halves = [slice(c * LANES, (c + 1) * LANES) for c in range(LRU_BW // LANES)]
    pitch = seq_len + LRU_PITCH_PAD
    seqs = [(slice(s * seq_len, (s + 1) * seq_len), slice(s * pitch, s * pitch + seq_len))
            for s in range(n_seq)]
    x = xr_ref[...]
    t = lax.broadcasted_iota(jnp.int32, (rows, LRU_BW), 0) & (seq_len - 1)
    cw = cw_ref[...]
    xf = (jnp.where(t >= 2, pltpu.roll(x, 2, 0), 0.0) * cw[0:1]
          + jnp.where(t >= 1, pltpu.roll(x, 1, 0), 0.0) * cw[1:2]
          + x * cw[2:3]
          + jnp.where(t < seq_len - 1, pltpu.roll(x, rows - 1, 0), 0.0) * cw[3:4]) + cb_ref[...]
    xb = xf.astype(BF16)
    sp = _softplus(-lam_ref[...])
    xh = 0.5 * xf
    for d, (a_ref, u_ref) in enumerate(((af_ref, uf_ref), (ab_ref, ub_ref))):
        bg = 0.5 * bg_ref[d]
        t_r = jnp.tanh(_dot(xb, (0.5 * wg_ref[d, 0]).astype(BF16)) + bg[0:1])
        t_i = jnp.tanh(_dot(xb, (0.5 * wg_ref[d, 1]).astype(BF16)) + bg[1:2])
        decay = (-0.5 * LRU_C * LOG2E) * sp[d:d + 1]
        a = jnp.exp2(decay * (t_r + 1.0))
        u = jnp.sqrt(1.0 - a * a) * (t_i + 1.0) * xh
        for c, cols in enumerate(halves):
            for src, dst in seqs:
                a_ref[c, dst, :] = a[src, cols]
                u_ref[c, dst, :] = u[src, cols]

    def step(i, carry):
        rf = pl.ds(i, n_seq, stride=pitch)
        rb = pl.ds(seq_len - 1 - i, n_seq, stride=pitch)
        out = []
        for c in range(len(halves)):
            hf = af_ref[c, rf, :] * carry[2 * c] + uf_ref[c, rf, :]
            hf_ref[c, rf, :] = hf
            hb = ab_ref[c, rb, :] * carry[2 * c + 1] + ub_ref[c, rb, :]
            hb_ref[c, rb, :] = hb
            out += [hf, hb]
        return tuple(out)

    init = tuple(h0_ref[d, 0:n_seq, cols] for cols in halves for d in range(2))
    fin = lax.fori_loop(0, seq_len, step, init, unroll=4)
    fin_ref[...] = jnp.zeros(fin_ref.shape, F32)
    gate = jax.nn.gelu(gb_ref[...])
    for c, cols in enumerate(halves):
        fin_ref[0, 0:n_seq, cols] = fin[2 * c]
        fin_ref[1, 0:n_seq, cols] = fin[2 * c + 1]
        for src, dst in seqs:
            h = hf_ref[c, dst, :] + hb_ref[c, dst, :]
            y_ref[src, cols] = (h * gate[src, cols]).astype(y_ref.dtype)


def _lru_kernel(*refs):
    i = pl.program_id(0)

    @pl.when(i < LRU_CTX_STEPS)
    def _():
        _lru_body(*refs, n_seq=LRU_ROWS // SEQ, seq_len=SEQ)

    @pl.when(i >= LRU_CTX_STEPS)
    def _():
        _lru_body(*refs, n_seq=LRU_ROWS // DEC_SEQ, seq_len=DEC_SEQ)


def rglru_mix(proj, state, conv_w, conv_b, w_gates, b_gates, lam):
    assert LRU_ROWS // SEQ == LRU_SEQ_PAD and LRU_ROWS // DEC_SEQ == DEC_BATCH
    n_steps = N_TOK // LRU_ROWS
    h0 = jnp.zeros((2, n_steps * LRU_SEQ_PAD, D_RNN), F32)
    h0 = h0.at[:, BATCH:BATCH + DEC_BATCH].set(state.transpose(1, 0, 2))
    blk = lambda i, n: (i, n)
    chan = lambda i, n: (0, n)
    y, fin = pl.pallas_call(
        _lru_kernel,
        grid=(n_steps, LRU_BLOCKS),
        in_specs=[pl.BlockSpec((LRU_ROWS, LRU_BW), blk),
                  pl.BlockSpec((LRU_ROWS, LRU_BW), lambda i, n: (i, LRU_BLOCKS + n)),
                  pl.BlockSpec((CONV_W, LRU_BW), chan),
                  pl.BlockSpec((1, LRU_BW), chan),
                  pl.BlockSpec((2, 2, None, LRU_BW, LRU_BW), lambda i, n: (0, 0, n, 0, 0)),
                  pl.BlockSpec((2, 2, LRU_BW), lambda i, n: (0, 0, n)),
                  pl.BlockSpec((2, LRU_BW), chan),
                  pl.BlockSpec((2, LRU_SEQ_PAD, LRU_BW), lambda i, n: (0, i, n))],
        out_specs=[pl.BlockSpec((LRU_ROWS, LRU_BW), blk),
                   pl.BlockSpec((2, LRU_SEQ_PAD, LRU_BW), lambda i, n: (0, i, n))],
        out_shape=[jax.ShapeDtypeStruct((N_TOK, D_RNN), BF16),
                   jax.ShapeDtypeStruct((2, n_steps * LRU_SEQ_PAD, D_RNN), F32)],
        scratch_shapes=[pltpu.VMEM((LRU_BW // LANES, LRU_SCAN_ROWS, LANES), F32)] * 6,
        compiler_params=_cparams(("parallel", "arbitrary")),
        name="rglru_mix",
    )(proj, proj, conv_w, conv_b.reshape(1, D_RNN), w_gates, b_gates, lam, h0)
    return y, fin[:, :BATCH].transpose(1, 0, 2)


ML_GATE_COLS = 4 * ML_HEADS


def _pick_lane(x, idx):
    lane = lax.broadcasted_iota(jnp.int32, x.shape, 1)
    return jnp.sum(jnp.where(lane == idx, x, 0.0), axis=1, keepdims=True)


def _pick_row(x, idx):
    row = lax.broadcasted_iota(jnp.int32, x.shape, 0)
    return jnp.sum(jnp.where(row == idx, x, 0.0), axis=0, keepdims=True)


def _mlstm_kernel(q_ref, k_ref, v_ref, og_ref, g_ref, gt_ref, bg_ref, bgt_ref, ng_ref, *rest,
                  seq_len, zero_init, with_state):
    rest = list(rest)
    if not zero_init:
        c0_ref, n0_ref, m0_ref = rest[:3]
        rest = rest[3:]
    y_ref = rest.pop(0)
    if with_state:
        co_ref, no_ref, mo_ref = rest[:3]
        rest = rest[3:]
    (h_ref, c_ref, n_ref, dmat_ref, qk_ref, bcum_ref, rmax_ref, dec_ref, blast_ref, dmax_ref) = rest
    head = pl.program_id(1)
    n_chunks = seq_len // ML_CHUNK
    k_scale = ML_DK ** -0.5
    sub = lax.broadcasted_iota(jnp.int32, (ML_CHUNK, ML_CHUNK), 0)
    lane = lax.broadcasted_iota(jnp.int32, (ML_CHUNK, ML_CHUNK), 1)

    if zero_init:
        c_ref[...] = jnp.zeros(c_ref.shape, F32)
        n_ref[...] = jnp.zeros(n_ref.shape, F32)
        m_init = (jnp.zeros((1, 1), F32),) * 2
    else:
        c_ref[...] = c0_ref[...]
        n_ref[...] = n0_ref[...]
        m_init = (m0_ref[0], m0_ref[1])

    for c in range(n_chunks):
        rows = slice(c * ML_CHUNK, (c + 1) * ML_CHUNK)
        g = g_ref[rows, :] + bg_ref[...]
        gt = gt_ref[:, rows] + bgt_ref[...]
        qk_ref[c] = _dot_nt(q_ref[rows, :].astype(BF16), (k_ref[rows, :] * k_scale).astype(BF16))
        for d in range(2):
            valid = (lane <= sub) if d == 0 else (lane >= sub)
            valid_t = (sub <= lane) if d == 0 else (sub >= lane)
            i_col = d * 2 * ML_HEADS + head
            f_col = i_col + ML_HEADS
            li_col = _pick_lane(g, i_col)
            lf_col = -_softplus(-_pick_lane(g, f_col))
            li_row = _pick_row(gt, i_col)
            lf_row = -_softplus(-_pick_row(gt, f_col))
            bcum_col = jnp.sum(jnp.where(valid, lf_row, 0.0), axis=1, keepdims=True)
            bcum_row = jnp.sum(jnp.where(valid_t, lf_col, 0.0), axis=0, keepdims=True)
            b_last = jnp.sum(lf_row, axis=1, keepdims=True)
            dmat = jnp.where(valid, bcum_col - bcum_row + li_row, NEG_BIG)
            dec = b_last - bcum_col + li_col
            idx = d * n_chunks + c
            dmat_ref[idx] = dmat
            rmax_ref[idx] = jnp.max(dmat, axis=1, keepdims=True)
            bcum_ref[idx] = bcum_col
            dec_ref[idx] = dec
            blast_ref[idx] = b_last
            dmax_ref[idx] = jnp.max(dec, axis=0, keepdims=True)

    def one_chunk(d, c_idx, m_prev):
        idx = d * n_chunks + c_idx
        rows = pl.ds(pl.multiple_of(c_idx * ML_CHUNK, ML_CHUNK), ML_CHUNK)
        m_inter = bcum_ref[idx] + m_prev
        m_t = jnp.maximum(m_inter, rmax_ref[idx])
        qf = q_ref[rows, :]
        qb = qf.astype(BF16)
        kf = k_ref[rows, :] * k_scale
        vb = v_ref[rows, :].astype(BF16)
        s = qk_ref[c_idx] * jnp.exp(dmat_ref[idx] - m_t)
        inter = jnp.exp(m_inter - m_t)
        num = _dot(s.astype(BF16), vb) + inter * _dot(qb, c_ref[d].astype(BF16))
        den = (jnp.sum(s, axis=1, keepdims=True)
               + inter * jnp.sum(qf * n_ref[d], axis=1, keepdims=True))
        h_ref[d, rows, :] = num / jnp.maximum(jnp.abs(den), jnp.exp(-m_t))
        b_last = blast_ref[idx]
        m_new = jnp.maximum(b_last + m_prev, dmax_ref[idx])
        kw = kf * jnp.exp(dec_ref[idx] - m_new)
        carry_scale = jnp.exp(b_last + m_prev - m_new)
        c_ref[d] = carry_scale * c_ref[d] + _dot_tn(kw.astype(BF16), vb)
        n_ref[d] = carry_scale * n_ref[d] + jnp.sum(kw, axis=0, keepdims=True)
        return m_new

    def both(ci, m_prev):
        return one_chunk(0, ci, m_prev[0]), one_chunk(1, n_chunks - 1 - ci, m_prev[1])

    m_fin = lax.fori_loop(0, n_chunks, both, m_init)
    if with_state:
        co_ref[...] = c_ref[...]
        no_ref[...] = n_ref[...]
        mo_ref[0] = m_fin[0]
        mo_ref[1] = m_fin[1]

    hs = h_ref[0] + h_ref[1]
    hn = hs * lax.rsqrt(jnp.mean(hs * hs, axis=-1, keepdims=True) + EPS) * ng_ref[...]
    y_ref[...] = (_sigmoid(og_ref[...]) * hn).astype(y_ref.dtype)


def mlstm_mix(proj, gates, gates_t, b_gates, norm_g, *, row0, n_seq, seq_len, state=None, with_state):
    rb0 = row0 // seq_len
    kq = ML_HEADS * ML_DK // ML_DK
    v0 = 2 * ML_HEADS * ML_DK // ML_DV
    o0 = v0 + ML_HEADS
    zero_init = state is None
    n_chunks = seq_len // ML_CHUNK
    bias = jnp.pad(b_gates.reshape(1, ML_GATE_COLS), ((0, 0), (0, LANES - ML_GATE_COLS)))
    in_specs = [pl.BlockSpec((seq_len, ML_DK), lambda b, h: (rb0 + b, h)),
                pl.BlockSpec((seq_len, ML_DK), lambda b, h: (rb0 + b, kq + h)),
                pl.BlockSpec((seq_len, ML_DV), lambda b, h: (rb0 + b, v0 + h)),
                pl.BlockSpec((seq_len, ML_DV), lambda b, h: (rb0 + b, o0 + h)),
                pl.BlockSpec((seq_len, LANES), lambda b, h: (rb0 + b, 0)),
                pl.BlockSpec((ML_GATE_COLS, seq_len), lambda b, h: (0, rb0 + b)),
                pl.BlockSpec((1, LANES), lambda b, h: (0, 0)),
                pl.BlockSpec((ML_GATE_COLS, 1), lambda b, h: (0, 0)),
                pl.BlockSpec((1, ML_DV), lambda b, h: (0, h))]
    args = [proj, proj, proj, proj, gates, gates_t, bias, b_gates.reshape(ML_GATE_COLS, 1),
            norm_g.reshape(1, ML_HEADS * ML_DV)]
    c_spec = pl.BlockSpec((None, 2, None, ML_DK, ML_DV), lambda b, h: (b, 0, h, 0, 0))
    n_spec = pl.BlockSpec((None, 2, None, 1, ML_DK), lambda b, h: (b, 0, h, 0, 0))
    m_spec = pl.BlockSpec((None, 2, None, 1, 1), lambda b, h: (b, 0, h, 0, 0))
    if not zero_init:
        c0, n0, m0 = state
        in_specs += [c_spec, n_spec, m_spec]
        args += [c0, n0.reshape(n_seq, 2, ML_HEADS, 1, ML_DK), m0.reshape(n_seq, 2, ML_HEADS, 1, 1)]
    out_specs = [pl.BlockSpec((seq_len, ML_DV), lambda b, h: (b, h))]
    out_shape = [jax.ShapeDtypeStruct((n_seq * seq_len, ML_HEADS * ML_DV), BF16)]
    if with_state:
        out_specs += [c_spec, n_spec, m_spec]
        out_shape += [jax.ShapeDtypeStruct((n_seq, 2, ML_HEADS, ML_DK, ML_DV), F32),
                      jax.ShapeDtypeStruct((n_seq, 2, ML_HEADS, 1, ML_DK), F32),
                      jax.ShapeDtypeStruct((n_seq, 2, ML_HEADS, 1, 1), F32)]
    res = pl.pallas_call(
        functools.partial(_mlstm_kernel, seq_len=seq_len, zero_init=zero_init, with_state=with_state),
        grid=(n_seq, ML_HEADS),
        in_specs=in_specs,
        out_specs=out_specs,
        out_shape=out_shape,
        scratch_shapes=[pltpu.VMEM((2, seq_len, ML_DV), F32), pltpu.VMEM((2, ML_DK, ML_DV), F32),
                        pltpu.VMEM((2, 1, ML_DK), F32),
                        pltpu.VMEM((2 * n_chunks, ML_CHUNK, ML_CHUNK), F32),
                        pltpu.VMEM((n_chunks, ML_CHUNK, ML_CHUNK), F32)]
                       + [pltpu.VMEM((2 * n_chunks, ML_CHUNK, 1), F32)] * 3
                       + [pltpu.VMEM((2 * n_chunks, 1, 1), F32)] * 2,
        compiler_params=_cparams(("parallel", "arbitrary")),
        name="mlstm_mix",
    )(*args)
    if not with_state:
        return res[0]
    y, c_fin, n_fin, m_fin = res
    return y, c_fin, n_fin.reshape(n_seq, 2, ML_HEADS, ML_DK), m_fin.reshape(n_seq, 2, ML_HEADS)


def kernel(x_prompt, x_sample, cache_na_k, cache_na_v, state_lru, state_mlstm_C, state_mlstm_n, state_mlstm_m, cache_gqa_k, cache_gqa_v, c, c_ctx, ada_w, ada_b, norm_g, mlp_up, mlp_down, na_w_qkv, na_rpb, na_w_o, lru_w_in, lru_conv_w, lru_conv_b, lru_w_gates, lru_b_gates, lru_lambda, lru_w_o, ml_w_in, ml_w_gates, ml_b_gates, ml_norm_g, ml_w_o, gqa_w_qkv, gqa_q_norm, gqa_k_norm, gqa_w_o):
    x = (x_prompt.reshape(N_CTX, D_MODEL), x_sample.reshape(N_LAT, D_MODEL))
    cond = jnp.concatenate([c_ctx[None], c, jnp.zeros((MOD_ROWS - 1 - DEC_BATCH, D_MODEL), F32)], axis=0)
    act = silu_rows(cond)
    mods = [adaln_layer(act, ada_w, ada_b, 0)]
    act_b = jnp.broadcast_to(act[:N_COND, :, None], (N_COND, D_MODEL, LANES))
    gains = norm_g.reshape(DEPTH, 4, 1, D_MODEL)

    outs = {}
    h = norm_mod(x, gains, mods, 0)
    for i in range(DEPTH):
        kind = i % 4
        if kind == 0:
            qkv = matmul(h, na_w_qkv[0], name="na_qkv", **WIDE)
            shp = (BATCH, 1, SEQ, NA_HEADS, HEAD_DIM)
            yp, kp, vp = na_context_attn(qkv)
            outs['na_k'], outs['na_v'] = kp.reshape(shp), vp.reshape(shp)
            yl = na_latent_attn(qkv, cache_na_k[:, 0].reshape(DEC_BATCH, PAST_LEN, D_MODEL),
                                cache_na_v[:, 0].reshape(DEC_BATCH, PAST_LEN, D_MODEL), na_rpb[0])
            y = matmul(yp, na_w_o[0], a2=yl, name="na_out")
        elif kind == 1:
            proj = matmul(h, lru_w_in[0], name="lru_in", **WIDE)
            pre, fin = rglru_mix(proj, state_lru[:, 0], lru_conv_w[0], lru_conv_b[0], lru_w_gates[0],
                                 lru_b_gates[0], lru_lambda[0])
            outs['lru'] = fin[:, None]
            y = matmul(pre, lru_w_o[0], name="lru_out")
        elif kind == 2:
            proj = matmul(h, ml_w_in[0], name="ml_in", **WIDE)
            wg = jnp.pad(ml_w_gates[0], ((0, 0), (0, LANES - ML_GATE_COLS)))
            gates = matmul(h, wg, tn=LANES, name="ml_gates")
            gates_t = gates[:, :ML_GATE_COLS].T
            yp, cp, np_, mp = mlstm_mix(proj, gates, gates_t, ml_b_gates[0], ml_norm_g[0], row0=0,
                                        n_seq=BATCH, seq_len=SEQ, with_state=True)
            yl = mlstm_mix(proj, gates, gates_t, ml_b_gates[0], ml_norm_g[0], row0=N_CTX,
                           n_seq=DEC_BATCH, seq_len=DEC_SEQ, with_state=False,
                           state=(state_mlstm_C[:, 0], state_mlstm_n[:, 0], state_mlstm_m[:, 0]))
            outs['mc'], outs['mn'], outs['mm'] = cp[:, None], np_[:, None], mp[:, None]
            y = matmul(yp, ml_w_o[0], a2=yl, name="ml_out")
        else:
            qkv = matmul(h, gqa_w_qkv[0], name="gqa_qkv", **WIDE)
            kv_cols = GQA_KV_HEADS * HEAD_DIM
            yp, kp, vp = gqa_context_attn(qkv, gqa_q_norm[0], gqa_k_norm[0])
            yl = gqa_latent_attn(qkv, cache_gqa_k[:, 0].reshape(DEC_BATCH, PAST_LEN, kv_cols),
                                 cache_gqa_v[:, 0].reshape(DEC_BATCH, PAST_LEN, kv_cols),
                                 gqa_q_norm[0], gqa_k_norm[0])
            shp = (BATCH, 1, SEQ, GQA_KV_HEADS, HEAD_DIM)
            outs['gk'], outs['gv'] = kp.reshape(shp), vp.reshape(shp)
            y = matmul(yp, gqa_w_o[0], a2=yl, name="gqa_out")

        x, h2 = resid_norm_mod(x, y, gains, mods, layer=i, ga=1, gate=2, nxt=(i, 2, 3, 4))
        last = i + 1 == DEPTH
        if last:
            u = matmul(h2, mlp_up, layer=i, out_dtype=BF16, act="relu2", name="mlp_up", **WIDE)
        else:
            u, mod = matmul(h2, mlp_up, layer=i, out_dtype=BF16, act="relu2", name="mlp_up_ada",
                            ada=(ada_w, ada_b, act_b, i + 1))
            mods.append(mod.reshape(MOD_ROWS, 6, 1, D_MODEL))
        z = matmul(u, mlp_down, layer=i, tn=1024, tk=2048, name="mlp_down")
        nxt = None if last else (i + 1, 0, 0, 1)
        x, h = resid_norm_mod(x, z, gains, mods, layer=i, ga=3, gate=5, nxt=nxt, split_out=last)

    return (x[0].reshape(BATCH, SEQ, D_MODEL), x[1].reshape(DEC_BATCH, DEC_SEQ, D_MODEL),
            outs['na_k'], outs['na_v'], outs['lru'], outs['mc'], outs['mn'], outs['mm'], outs['gk'], outs['gv'])
```
